```python
import jax, jax.numpy as jnp
from jax import lax
import numpy as np

D_MODEL = 1024
BATCH = 4
SEQ = 8192
DEPTH = 2

HEAD_DIM = 64
BRANCH_WIDTH = 256
N_HEADS_A = 4
IDX_HEADS = 4
IDX_DIM = 32
TOPK_MAX = 256
N_GROUPS_B = 4
GROUP_DIM_B = 64
CHUNK = 128
N_HEADS_C = 4
CONV_WIDTH = 3
Q_BLOCK = 128
ROPE_THETA = 10000.0
N_BRANCH = 4
N_EXPERT_GROUPS = 4
EXPERTS_PER_GROUP = 8
N_EXPERTS = N_EXPERT_GROUPS * EXPERTS_PER_GROUP
TOPK_IN_GROUP = 2
D_EXPERT = 256
MOE_BLOCK = 128
EPS = 1e-6

SPLIT_SIZES = (BRANCH_WIDTH, BRANCH_WIDTH, BRANCH_WIDTH,
               IDX_HEADS * IDX_DIM, IDX_DIM, IDX_HEADS,
               BRANCH_WIDTH, BRANCH_WIDTH,
               BRANCH_WIDTH, BRANCH_WIDTH, BRANCH_WIDTH,
               BRANCH_WIDTH, BRANCH_WIDTH, BRANCH_WIDTH,
               N_BRANCH * D_MODEL)
W_IN_COLS = sum(SPLIT_SIZES)

kernel_name = 'hybrid_gated_mixer_hmoe'


def rmsnorm(x, g):
    xf = x.astype(jnp.float32)
    y = xf * lax.rsqrt(jnp.mean(xf * xf, axis=-1, keepdims=True) + EPS)
    return (y * g.astype(jnp.float32)).astype(x.dtype)


def rope_tables(positions, dim):
    inv = ROPE_THETA ** (-jnp.arange(0, dim, 2, dtype=jnp.float32) / dim)
    ang = positions.astype(jnp.float32)[..., None] * inv
    return jnp.cos(ang)[:, :, None, :], jnp.sin(ang)[:, :, None, :]


def apply_rope(x, cos, sin):
    xf = x.astype(jnp.float32)
    x1, x2 = jnp.split(xf, 2, axis=-1)
    out = jnp.concatenate([x1 * cos - x2 * sin, x2 * cos + x1 * sin], axis=-1)
    return out.astype(x.dtype)


def to_blocks(a, nb):
    return jnp.moveaxis(a.reshape(a.shape[0], nb, Q_BLOCK, *a.shape[2:]), 1, 0)


def dsa_attention(q, k, v, iq, ik, iw):
    B, S, H, Dh = q.shape
    nb = S // Q_BLOCK
    topk = min(TOPK_MAX, S // 4)
    scale = Dh ** -0.5
    key_pos = jnp.arange(S)

    def block(args):
        qb, iqb, iwb, blk = args
        tq = blk * Q_BLOCK + jnp.arange(Q_BLOCK)
        causal = key_pos[None, :] <= tq[:, None]
        rel = jax.nn.relu(jnp.einsum('bqhd,bsd->bqhs', iqb, ik))
        score = jnp.einsum('bqh,bqhs->bqs', iwb, rel).astype(jnp.float32)
        score = jnp.where(causal[None], score, -jnp.inf)
        _, idx = lax.top_k(score, topk)
        valid = idx <= tq[None, :, None]
        k_sel = jax.vmap(lambda kb, ib: kb[ib])(k, idx)
        v_sel = jax.vmap(lambda vb, ib: vb[ib])(v, idx)
        logits = jnp.einsum('bqhd,bqkhd->bhqk', qb, k_sel).astype(jnp.float32) * scale
        logits = jnp.where(valid[:, None], logits, -jnp.inf)
        p = jax.nn.softmax(logits, axis=-1).astype(v.dtype)
        return jnp.einsum('bhqk,bqkhd->bqhd', p, v_sel)

    out = lax.map(block, (to_blocks(q, nb), to_blocks(iq, nb), to_blocks(iw, nb),
                          jnp.arange(nb)))
    return jnp.moveaxis(out, 0, 1).reshape(B, S, H * Dh)


def chunked_sgu(u, v, ln_g, ln_b, w_s, b_s):
    B, S, W = v.shape
    nc = S // CHUNK
    u = jax.nn.gelu(u)
    v = jax.nn.gelu(v)
    vf = v.astype(jnp.float32)
    mu = jnp.mean(vf, axis=-1, keepdims=True)
    var = jnp.mean(jnp.square(vf - mu), axis=-1, keepdims=True)
    vn = ((vf - mu) * lax.rsqrt(var + EPS) * ln_g + ln_b).astype(v.dtype)
    vn = vn.reshape(B, nc, CHUNK, N_GROUPS_B, GROUP_DIM_B)
    tril = jnp.tril(jnp.ones((CHUNK, CHUNK), dtype=w_s.dtype))
    mixed = jnp.einsum('gts,bcsgd->bctgd', w_s * tril, vn) + b_s.T[:, :, None]
    return u * mixed.reshape(B, S, W)


def stick_breaking_attention(q, k, v):
    B, S, H, Dh = q.shape
    nb = S // Q_BLOCK
    scale = Dh ** -0.5
    key_pos = jnp.arange(S)

    def block(args):
        qb, blk = args
        tq = blk * Q_BLOCK + jnp.arange(Q_BLOCK)
        strict = key_pos[None, :] < tq[:, None]
        z = jnp.einsum('bqhd,bshd->bhqs', qb, k).astype(jnp.float32) * scale
        log_beta = jax.nn.log_sigmoid(z)
        log_keep = jnp.where(strict, jax.nn.log_sigmoid(-z), 0.0)
        later = lax.cumsum(log_keep, axis=3, reverse=True) - log_keep
        a = jnp.where(strict, jnp.exp(log_beta + later), 0.0).astype(v.dtype)
        return jnp.einsum('bhqs,bshd->bqhd', a, v)

    out = lax.map(block, (to_blocks(q, nb), jnp.arange(nb)))
    return jnp.moveaxis(out, 0, 1).reshape(B, S, H * Dh)


def short_gated_conv(b_gate, c_gate, xin, conv_w):
    S = xin.shape[1]
    z = c_gate * xin
    zp = jnp.pad(z, ((0, 0), (CONV_WIDTH - 1, 0), (0, 0)))
    y = conv_w[0] * zp[:, 0:S]
    for tap in range(1, CONV_WIDTH):
        y = y + conv_w[tap] * zp[:, tap:tap + S]
    return b_gate * y


def token_mixers(h, cos_a, sin_a, cos_i, sin_i, w_in, ln_g, ln_b, w_s, b_s, conv_w,
                 w_branch, w_out):
    B, S, _ = h.shape
    proj = h @ w_in
    offsets = [int(o) for o in np.cumsum(SPLIT_SIZES)[:-1]]
    (qa, ka, va, iq, ik, iw, ub, vb, qc, kc, vc, bd, cd, xd,
     gate_logits) = jnp.split(proj, offsets, axis=-1)

    qa = apply_rope(qa.reshape(B, S, N_HEADS_A, HEAD_DIM), cos_a, sin_a)
    ka = apply_rope(ka.reshape(B, S, N_HEADS_A, HEAD_DIM), cos_a, sin_a)
    va = va.reshape(B, S, N_HEADS_A, HEAD_DIM)
    iq = apply_rope(iq.reshape(B, S, IDX_HEADS, IDX_DIM), cos_i, sin_i)
    ik = apply_rope(ik[:, :, None, :], cos_i, sin_i)[:, :, 0]
    y_a = dsa_attention(qa, ka, va, iq, ik, iw)

    y_b = chunked_sgu(ub, vb, ln_g, ln_b, w_s, b_s)

    y_c = stick_breaking_attention(qc.reshape(B, S, N_HEADS_C, HEAD_DIM),
                                   kc.reshape(B, S, N_HEADS_C, HEAD_DIM),
                                   vc.reshape(B, S, N_HEADS_C, HEAD_DIM))

    y_d = short_gated_conv(bd, cd, xd, conv_w)

    gates = jax.nn.sigmoid(gate_logits.astype(jnp.float32)).astype(h.dtype)
    gates = gates.reshape(B, S, N_BRANCH, D_MODEL)
    branches = (y_a, y_b, y_c, y_d)
    merged = gates[:, :, 0] * (branches[0] @ w_branch[0])
    for n in range(1, N_BRANCH):
        merged = merged + gates[:, :, n] * (branches[n] @ w_branch[n])
    return merged @ w_out


def hierarchical_moe(h, w_rg, b_rg, w_re, b_re, w_g, w_u, w_d):
    B, S, D = h.shape
    hf = h.astype(jnp.float32)
    p_grp = jax.nn.softmax(hf @ w_rg.astype(jnp.float32) + b_rg, axis=-1)
    grp_p, grp_idx = lax.top_k(p_grp, 1)
    exp_logits = (hf @ w_re.astype(jnp.float32) + b_re).reshape(
        B, S, N_EXPERT_GROUPS, EXPERTS_PER_GROUP)
    sel = jax.nn.one_hot(grp_idx[..., 0], N_EXPERT_GROUPS, dtype=jnp.float32)
    in_grp = jnp.einsum('bsg,bsge->bse', sel, exp_logits)
    top_l, top_j = lax.top_k(in_grp, TOPK_IN_GROUP)
    weights = grp_p * jax.nn.softmax(top_l, axis=-1)
    eid = grp_idx * EXPERTS_PER_GROUP + top_j
    combine = jnp.sum(jax.nn.one_hot(eid, N_EXPERTS, dtype=jnp.float32) * weights[..., None],
                      axis=2).astype(h.dtype)
    n_tok = B * S
    tok = h.reshape(n_tok // MOE_BLOCK, MOE_BLOCK, D)
    comb = combine.reshape(n_tok // MOE_BLOCK, MOE_BLOCK, N_EXPERTS)

    def block(args):
        tb, cb = args
        a = jnp.einsum('td,edf->tef', tb, w_g)
        u = jnp.einsum('td,edf->tef', tb, w_u)
        hh = jax.nn.silu(a) * u * cb[:, :, None]
        return jnp.einsum('tef,efd->td', hh, w_d)

    return lax.map(block, (tok, comb)).reshape(B, S, D)


def setup_inputs(seed: int = 0) -> dict:
    key = jax.random.key(seed)
    ks = jax.random.split(key, 19)

    def nrm(k, shape, fan):
        return jax.random.normal(k, shape, jnp.float32) * fan ** -0.5

    def noise(k, shape, s):
        return jax.random.normal(k, shape, jnp.float32) * s

    return {
        'x': jax.random.normal(ks[0], (BATCH, SEQ, D_MODEL), jnp.float32),
        'positions': jnp.broadcast_to(jnp.arange(SEQ, dtype=jnp.int32), (BATCH, SEQ)),
        'norm1_g': 1.0 + noise(ks[1], (DEPTH, D_MODEL), 0.02),
        'w_in': nrm(ks[2], (DEPTH, D_MODEL, W_IN_COLS), D_MODEL),
        'sgu_ln_g': 1.0 + noise(ks[3], (DEPTH, BRANCH_WIDTH), 0.02),
        'sgu_ln_b': noise(ks[4], (DEPTH, BRANCH_WIDTH), 0.02),
        'w_spatial': nrm(ks[5], (DEPTH, N_GROUPS_B, CHUNK, CHUNK), CHUNK),
        'b_spatial': 1.0 + noise(ks[6], (DEPTH, N_GROUPS_B, CHUNK), 0.1),
        'conv_w': nrm(ks[7], (DEPTH, CONV_WIDTH, BRANCH_WIDTH), CONV_WIDTH),
        'w_branch': nrm(ks[8], (DEPTH, N_BRANCH, BRANCH_WIDTH, D_MODEL), BRANCH_WIDTH),
        'w_out': nrm(ks[9], (DEPTH, D_MODEL, D_MODEL), D_MODEL),
        'norm2_g': 1.0 + noise(ks[10], (DEPTH, D_MODEL), 0.02),
        'w_router_group': nrm(ks[11], (DEPTH, D_MODEL, N_EXPERT_GROUPS), D_MODEL),
        'b_router_group': noise(ks[12], (DEPTH, N_EXPERT_GROUPS), 0.01),
        'w_router_expert': nrm(ks[13], (DEPTH, D_MODEL, N_EXPERTS), D_MODEL),
        'b_router_expert': noise(ks[14], (DEPTH, N_EXPERTS), 0.01),
        'w_exp_gate': nrm(ks[15], (DEPTH, N_EXPERTS, D_MODEL, D_EXPERT), D_MODEL),
        'w_exp_up': nrm(ks[16], (DEPTH, N_EXPERTS, D_MODEL, D_EXPERT), D_MODEL),
        'w_exp_down': nrm(ks[17], (DEPTH, N_EXPERTS, D_EXPERT, D_MODEL), D_EXPERT),
        'norm_f_g': 1.0 + noise(ks[18], (D_MODEL,), 0.02),
    }


def reference(x, positions, norm1_g, w_in, sgu_ln_g, sgu_ln_b, w_spatial, b_spatial, conv_w,
              w_branch, w_out, norm2_g, w_router_group, b_router_group, w_router_expert,
              b_router_expert, w_exp_gate, w_exp_up, w_exp_down, norm_f_g):
    cos_a, sin_a = rope_tables(positions, HEAD_DIM)
    cos_i, sin_i = rope_tables(positions, IDX_DIM)
    for l in range(DEPTH):
        h = rmsnorm(x, norm1_g[l])
        x = x + token_mixers(h, cos_a, sin_a, cos_i, sin_i, w_in[l], sgu_ln_g[l], sgu_ln_b[l],
                             w_spatial[l], b_spatial[l], conv_w[l], w_branch[l], w_out[l])
        h = rmsnorm(x, norm2_g[l])
        x = x + hierarchical_moe(h, w_router_group[l], b_router_group[l], w_router_expert[l],
                                 b_router_expert[l], w_exp_gate[l], w_exp_up[l], w_exp_down[l])
    return rmsnorm(x, norm_f_g)
```

```python
import functools

import jax
import jax.numpy as jnp
from jax import lax
from jax.experimental import pallas as pl
from jax.experimental.pallas import tpu as pltpu

F32 = jnp.float32
BF16 = jnp.bfloat16

EPS = 1e-6
ROPE_THETA = 10000.0
HEAD_DIM = 64
BRANCH_WIDTH = 256
N_HEADS = 4
IDX_DIM = 32
TOPK_MAX = 256
CHUNK = 128
N_GROUPS_B = 4
CONV_WIDTH = 3
N_BRANCH = 4
N_EXPERT_GROUPS = 4
EXPERTS_PER_GROUP = 8
N_EXPERTS = N_EXPERT_GROUPS * EXPERTS_PER_GROUP
D_EXPERT = 256

LANES = 128
Q_BLOCK = 128
K_TILE = 256
COUNT_CHUNK = 512
PROJ_ROWS = 256
MOE_ROWS = 1024
CONV_HALO = 8
VMEM_LIMIT = 56 * 1024 * 1024
NEG_BIG = -1e30
INT_MIN = -2 ** 31

_OFF = {}
_o = 0
for _name, _w in (("q", 512), ("k", 512), ("v", 256), ("iq", 256), ("ik", 256), ("iw", 128),
                  ("ub", 256), ("vb", 256), ("qc", 256), ("kc", 256), ("vc", 256),
                  ("bd", 256), ("cd", 256), ("xd", 256)):
    _OFF[_name] = (_o, _w)
    _o += _w
W_A_COLS = _o


def _params(*sem):
    return pltpu.CompilerParams(dimension_semantics=sem, vmem_limit_bytes=VMEM_LIMIT)


def _rope_table_kernel(pos_ref, inv_a_ref, inv_i_ref, cos_a, sin_a, cos_i, sin_i):
    p = pos_ref[...]
    a = p * inv_a_ref[...]
    cos_a[...] = jnp.cos(a)
    sin_a[...] = jnp.sin(a)
    b = p * inv_i_ref[...]
    cos_i[...] = jnp.cos(b)
    sin_i[...] = jnp.sin(b)


def _rope_tables(positions):
    n = positions.size
    rows = PROJ_ROWS
    pos = positions.astype(F32).reshape(n, 1)
    inv_a = ROPE_THETA ** (-jnp.arange(0, HEAD_DIM, 2, dtype=F32) / HEAD_DIM)
    inv_i = ROPE_THETA ** (-jnp.arange(0, IDX_DIM, 2, dtype=F32) / IDX_DIM)
    inv_a = jnp.tile(inv_a, LANES // inv_a.size).reshape(1, LANES)
    inv_i = jnp.tile(inv_i, LANES // inv_i.size).reshape(1, LANES)
    tab = jax.ShapeDtypeStruct((n, LANES), F32)
    row_spec = pl.BlockSpec((rows, LANES), lambda i: (i, 0))
    const = pl.BlockSpec((1, LANES), lambda i: (0, 0))
    return pl.pallas_call(
        _rope_table_kernel,
        grid=(n // rows,),
        in_specs=[pl.BlockSpec((rows, 1), lambda i: (i, 0)), const, const],
        out_specs=[row_spec] * 4,
        out_shape=[tab] * 4,
        compiler_params=_params("arbitrary"),
        name="rope_tables",
    )(pos, inv_a, inv_i)


def _rms_bf16(x, g):
    ms = jnp.mean(x * x, axis=-1, keepdims=True)
    return (x * lax.rsqrt(ms + EPS) * g).astype(BF16)


def _proj_kernel(x_ref, g_ref, w_ref, cos_a, sin_a, cos_i, sin_i, lng_ref, lnb_ref, ws_ref,
                 bs_ref, cw_ref,
                 qa_o, ka_o, va_o, iq_o, ik_o, iw_o, yb_o, qc_o, kc_o, vc_o, yd_o,
                 zbuf, *, steps_per_seq):
    rows = x_ref.shape[0]
    h = _rms_bf16(x_ref[...], g_ref[...])

    def mm(name):
        off, width = _OFF[name]
        return jnp.dot(h, w_ref[:, off:off + width], preferred_element_type=F32)

    def rope(name, cos_ref, sin_ref, out_ref, scale):
        p = mm(name)
        w = p.shape[1] // 2
        c = cos_ref[...]
        s = sin_ref[...]
        for j in range(w // LANES):
            lo = j * LANES
            r = p[:, lo:lo + LANES] * c + p[:, w + lo:w + lo + LANES] * s
            out_ref[:, lo:lo + LANES] = (r * scale).astype(out_ref.dtype)

    att_scale = HEAD_DIM ** -0.5
    rope("q", cos_a, sin_a, qa_o, att_scale)
    rope("k", cos_a, sin_a, ka_o, 1.0)
    va_o[...] = mm("v").astype(BF16)
    rope("iq", cos_i, sin_i, iq_o, 1.0)
    rope("ik", cos_i, sin_i, ik_o, 1.0)
    iw_o[...] = mm("iw")
    qc_o[...] = (mm("qc") * att_scale).astype(BF16)
    kc_o[...] = mm("kc").astype(BF16)
    vc_o[...] = mm("vc").astype(BF16)

    u = jax.nn.gelu(mm("ub"))
    v = jax.nn.gelu(mm("vb"))
    mu = jnp.mean(v, axis=-1, keepdims=True)
    var = jnp.mean(jnp.square(v - mu), axis=-1, keepdims=True)
    vn = ((v - mu) * lax.rsqrt(var + EPS) * lng_ref[...] + lnb_ref[...]).astype(BF16)
    t_idx = lax.broadcasted_iota(jnp.int32, (CHUNK, CHUNK), 0)
    s_idx = lax.broadcasted_iota(jnp.int32, (CHUNK, CHUNK), 1)
    causal = s_idx <= t_idx
    lane_group = lax.broadcasted_iota(jnp.int32, (CHUNK, BRANCH_WIDTH), 1) // (BRANCH_WIDTH // N_GROUPS_B)
    w_tril = [jnp.where(causal, ws_ref[g], 0.0).astype(BF16) for g in range(N_GROUPS_B)]
    for c in range(rows // CHUNK):
        vc = vn[c * CHUNK:(c + 1) * CHUNK, :]
        mixed = bs_ref[...]
        for g in range(N_GROUPS_B):
            full = jnp.dot(w_tril[g], vc, preferred_element_type=F32)
            mixed = mixed + jnp.where(lane_group == g, full, 0.0)
        yb_o[c * CHUNK:(c + 1) * CHUNK, :] = (u[c * CHUNK:(c + 1) * CHUNK, :] * mixed).astype(BF16)

    bd = mm("bd")
    z = mm("cd") * mm("xd")

    @pl.when(pl.program_id(0) % steps_per_seq == 0)
    def _():
        zbuf[0:CONV_HALO, :] = jnp.zeros((CONV_HALO, BRANCH_WIDTH), F32)

    zbuf[CONV_HALO:CONV_HALO + rows, :] = z
    y = cw_ref[CONV_WIDTH - 1:CONV_WIDTH, :] * z
    for tap in range(CONV_WIDTH - 1):
        shift = CONV_WIDTH - 1 - tap
        y = y + cw_ref[tap:tap + 1, :] * zbuf[pl.ds(CONV_HALO - shift, rows), :]
    yd_o[...] = (bd * y).astype(BF16)
    zbuf[0:CONV_HALO, :] = z[rows - CONV_HALO:rows, :]


def _swap_halves(w, dim):
    d_in, cols = w.shape
    wh = w.reshape(d_in, cols // dim, 2, dim // 2)
    return jnp.stack([-wh[:, :, 1], wh[:, :, 0]], axis=2).reshape(d_in, cols)


def _proj_weight(w_in):
    d = w_in.shape[0]
    sizes = (BRANCH_WIDTH,) * 3 + (N_HEADS * IDX_DIM, IDX_DIM, N_HEADS) + (BRANCH_WIDTH,) * 8
    parts, o = [], 0
    for s in sizes:
        parts.append(w_in[:, o:o + s])
        o += s
    qa, ka, va, iq, ik, iw, ub, vb, qc, kc, vc, bd, cd, xd = parts
    ik_rep = jnp.tile(ik, (1, N_HEADS))
    iw_pad = jnp.concatenate([iw, jnp.zeros((d, LANES - N_HEADS), w_in.dtype)], axis=1)
    cols = [qa, _swap_halves(qa, HEAD_DIM), ka, _swap_halves(ka, HEAD_DIM), va,
            iq, _swap_halves(iq, IDX_DIM), ik_rep, _swap_halves(ik_rep, IDX_DIM), iw_pad,
            ub, vb, qc, kc, vc, bd, cd, xd]
    return jnp.concatenate(cols, axis=1).astype(BF16), o


def _project(x2, norm_g, w_a, tables, ln_g, ln_b, w_spatial, b_spatial, conv_w, seq):
    n, d = x2.shape
    rows = PROJ_ROWS
    cos_a, sin_a, cos_i, sin_i = tables
    bias = jnp.repeat(b_spatial.T, BRANCH_WIDTH // N_GROUPS_B, axis=1)
    row = lambda w: pl.BlockSpec((rows, w), lambda i: (i, 0))
    full = lambda a: pl.BlockSpec(a.shape, lambda i: (0,) * a.ndim)
    bf = lambda w: jax.ShapeDtypeStruct((n, w), BF16)
    args = (x2, norm_g.reshape(1, d), w_a, cos_a, sin_a, cos_i, sin_i, ln_g.reshape(1, -1),
            ln_b.reshape(1, -1), w_spatial, bias, conv_w)
    in_specs = [row(d), full(args[1]), full(w_a), row(LANES), row(LANES), row(LANES), row(LANES),
                full(args[7]), full(args[8]), full(w_spatial), full(bias), full(conv_w)]
    widths = (256, 256, 256, 128, 128)
    out_shape = [bf(w) for w in widths] + [jax.ShapeDtypeStruct((n, LANES), F32)] + [bf(256)] * 5
    out_specs = [row(w) for w in widths] + [row(LANES)] + [row(256)] * 5
    return pl.pallas_call(
        functools.partial(_proj_kernel, steps_per_seq=seq // rows),
        grid=(n // rows,),
        in_specs=in_specs,
        out_specs=out_specs,
        out_shape=out_shape,
        scratch_shapes=[pltpu.VMEM((CONV_HALO + rows, BRANCH_WIDTH), F32)],
        compiler_params=_params("arbitrary"),
        name="project",
    )(*args)


def _head_lane_mask(width, per_head, h, rows=1):
    lane = lax.broadcasted_iota(jnp.int32, (rows, width), 1)
    return (lane >= h * per_head) & (lane < (h + 1) * per_head)


def _key_to_float(key):
    bits = key ^ ((key >> 31) & jnp.int32(0x7FFFFFFF))
    return lax.bitcast_convert_type(bits, F32)


def _dsa_kernel(iq_ref, iw_ref, q_ref, ik_ref, k_ref, v_ref, o_ref, sc_ref, *, topk):
    qb = q_ref.shape[0]
    seq = k_ref.shape[0]
    i = pl.program_id(1)
    tiles_per_q = K_TILE // qb
    n_tiles = i // tiles_per_q + 1

    @pl.when(i == 0)
    def _():
        sc_ref[...] = jnp.full(sc_ref.shape, -jnp.inf, F32)

    iq = iq_ref[...]
    iq_heads = [jnp.where(_head_lane_mask(iq.shape[1], IDX_DIM, h), iq, jnp.zeros_like(iq))
                for h in range(N_HEADS)]
    iw = iw_ref[...]
    iw_heads = [iw[:, h:h + 1] for h in range(N_HEADS)]
    contract_last = (((1,), (1,)), ((), ()))

    def score_tile(j):
        start = pl.multiple_of(j * K_TILE, K_TILE)
        ik = ik_ref[pl.ds(start, K_TILE), :]
        sc = jnp.zeros((qb, K_TILE), F32)
        for h in range(N_HEADS):
            d = lax.dot_general(iq_heads[h], ik, contract_last, preferred_element_type=F32)
            sc = sc + iw_heads[h] * jnp.maximum(d, 0.0)
        return start, sc

    def full_tile(j, carry):
        start, sc = score_tile(j)
        sc_ref[:, pl.ds(start, K_TILE)] = sc
        return carry

    lax.fori_loop(0, n_tiles - 1, full_tile, 0)
    start, sc = score_tile(n_tiles - 1)
    row_pos = i * qb + lax.broadcasted_iota(jnp.int32, (qb, K_TILE), 0)
    col_pos = start + lax.broadcasted_iota(jnp.int32, (qb, K_TILE), 1)
    sc_ref[:, pl.ds(start, K_TILE)] = jnp.where(col_pos <= row_pos, sc, -jnp.inf)

    n_chunks = (n_tiles * K_TILE + COUNT_CHUNK - 1) // COUNT_CHUNK
    kf = jnp.float32(topk)

    def count_ge(thr):
        def body(c, acc):
            blk = sc_ref[:, pl.ds(pl.multiple_of(c * COUNT_CHUNK, COUNT_CHUNK), COUNT_CHUNK)]
            for p in range(COUNT_CHUNK // LANES):
                acc = acc + jnp.where(blk[:, p * LANES:(p + 1) * LANES] >= thr, 1.0, 0.0)
            return acc
        acc = lax.fori_loop(0, n_chunks, body, jnp.zeros((qb, LANES), F32))
        return jnp.sum(acc, axis=1, keepdims=True)

    def bit_step(t, carry):
        key, cnt_key = carry
        cand = key + lax.shift_left(jnp.int32(1), 31 - t)
        cnt = count_ge(_key_to_float(cand))
        ok = cnt >= kf
        return jnp.where(ok, cand, key), jnp.where(ok, cnt, cnt_key)

    key0 = jnp.full((qb, 1), INT_MIN, jnp.int32)
    key, cnt_key = lax.fori_loop(0, 32, bit_step, (key0, jnp.full((qb, 1), kf, F32)))
    few = key == INT_MIN
    thr = jnp.where(few, jnp.finfo(F32).min, _key_to_float(key))

    @pl.when(jnp.max(jnp.where(few, kf, cnt_key)) > kf)
    def _():
        def gt_body(c, acc):
            blk = sc_ref[:, pl.ds(pl.multiple_of(c * COUNT_CHUNK, COUNT_CHUNK), COUNT_CHUNK)]
            for p in range(COUNT_CHUNK // LANES):
                acc = acc + jnp.where(blk[:, p * LANES:(p + 1) * LANES] > thr, 1.0, 0.0)
            return acc
        acc = lax.fori_loop(0, n_chunks, gt_body, jnp.zeros((qb, LANES), F32))
        need = kf - jnp.sum(acc, axis=1, keepdims=True)
        before = (lax.broadcasted_iota(jnp.int32, (K_TILE, K_TILE), 0)
                  < lax.broadcasted_iota(jnp.int32, (K_TILE, K_TILE), 1))
        before = jnp.where(before, 1.0, 0.0).astype(BF16)

        def tie_body(j, seen):
            start = pl.multiple_of(j * K_TILE, K_TILE)
            blk = sc_ref[:, pl.ds(start, K_TILE)]
            eq = blk == thr
            eq_b = jnp.where(eq, 1.0, 0.0).astype(BF16)
            rank = seen + jnp.dot(eq_b, before, preferred_element_type=F32)
            sc_ref[:, pl.ds(start, K_TILE)] = jnp.where(eq & (rank >= need), -jnp.inf, blk)
            return seen + jnp.sum(jnp.where(eq, 1.0, 0.0), axis=1, keepdims=True)

        lax.fori_loop(0, n_tiles, tie_body, jnp.zeros((qb, 1), F32))

    q = q_ref[...]
    width = q.shape[1]
    q_heads = [jnp.where(_head_lane_mask(width, HEAD_DIM, h), q, jnp.zeros_like(q))
               for h in range(N_HEADS)]
    out_masks = [_head_lane_mask(width, HEAD_DIM, h) for h in range(N_HEADS)]

    def spread(cols):
        full = cols[N_HEADS - 1]
        for h in range(N_HEADS - 2, -1, -1):
            full = jnp.where(out_masks[h], cols[h], full)
        return jnp.broadcast_to(full, (qb, width))

    def att_tile(j, carry):
        m, l, acc = carry
        start = pl.multiple_of(j * K_TILE, K_TILE)
        sel = sc_ref[:, pl.ds(start, K_TILE)] >= thr
        kt = k_ref[pl.ds(start, K_TILE), :]
        vt = v_ref[pl.ds(start, K_TILE), :]
        m_new, l_new, alphas, pv = [], [], [], jnp.zeros((qb, width), F32)
        for h in range(N_HEADS):
            s = lax.dot_general(q_heads[h], kt, contract_last, preferred_element_type=F32)
            s = jnp.where(sel, s, NEG_BIG)
            mh = jnp.maximum(m[h], jnp.max(s, axis=1, keepdims=True))
            p = jnp.exp(s - mh)
            alpha = jnp.exp(m[h] - mh)
            m_new.append(mh)
            alphas.append(alpha)
            l_new.append(alpha * l[h] + jnp.sum(p, axis=1, keepdims=True))
            ph = jnp.dot(p.astype(BF16), vt, preferred_element_type=F32)
            pv = pv + jnp.where(out_masks[h], ph, 0.0)
        return tuple(m_new), tuple(l_new), acc * spread(alphas) + pv

    init = (tuple(jnp.full((qb, 1), NEG_BIG, F32) for _ in range(N_HEADS)),
            tuple(jnp.zeros((qb, 1), F32) for _ in range(N_HEADS)),
            jnp.zeros((qb, width), F32))
    _, l, acc = lax.fori_loop(0, n_tiles, att_tile, init)
    o_ref[...] = (acc / spread(list(l))).astype(o_ref.dtype)


def _dsa(iq, iw, qa, ik, ka, va, batch, seq):
    n = qa.shape[0]
    nq = seq // Q_BLOCK
    topk = min(TOPK_MAX, seq // 4)
    qrow = lambda w: pl.BlockSpec((Q_BLOCK, w), lambda b, i: (b * nq + i, 0))
    whole = lambda w: pl.BlockSpec((seq, w), lambda b, i: (b, 0))
    return pl.pallas_call(
        functools.partial(_dsa_kernel, topk=topk),
        grid=(batch, nq),
        in_specs=[qrow(LANES), qrow(LANES), qrow(BRANCH_WIDTH), whole(LANES), whole(BRANCH_WIDTH),
                  whole(BRANCH_WIDTH)],
        out_specs=qrow(BRANCH_WIDTH),
        out_shape=jax.ShapeDtypeStruct((n, BRANCH_WIDTH), BF16),
        scratch_shapes=[pltpu.VMEM((Q_BLOCK, seq), F32)],
        compiler_params=_params("arbitrary", "arbitrary"),
        name="dsa",
    )(iq, iw, qa, ik, ka, va)


def _stick_kernel(q_ref, k_ref, v_ref, o_ref):
    qb, width = q_ref.shape
    i = pl.program_id(1)
    tiles_per_q = K_TILE // qb
    n_tiles = i // tiles_per_q + 1
    q = q_ref[...]
    q_heads = [jnp.where(_head_lane_mask(width, HEAD_DIM, h), q, jnp.zeros_like(q))
               for h in range(N_HEADS)]
    out_masks = [_head_lane_mask(width, HEAD_DIM, h) for h in range(N_HEADS)]
    contract_last = (((1,), (1,)), ((), ()))
    after = (lax.broadcasted_iota(jnp.int32, (2 * K_TILE, K_TILE), 0) % K_TILE
             > lax.broadcasted_iota(jnp.int32, (2 * K_TILE, K_TILE), 1))
    after = jnp.where(after, 1.0, 0.0).astype(BF16)

    def tile(j, carry, strict):
        tails, acc = carry
        start = pl.multiple_of(j * K_TILE, K_TILE)
        kt = k_ref[pl.ds(start, K_TILE), :]
        vt = v_ref[pl.ds(start, K_TILE), :]
        new_tails = []
        for h in range(N_HEADS):
            z = lax.dot_general(q_heads[h], kt, contract_last, preferred_element_type=F32)
            log_beta = jnp.minimum(z, 0.0) - jnp.log1p(jnp.exp(-jnp.abs(z)))
            log_keep = log_beta - z
            if strict is not None:
                log_keep = jnp.where(strict, log_keep, 0.0)
            hi = log_keep.astype(BF16)
            lo = (log_keep - hi.astype(F32)).astype(BF16)
            later = jnp.dot(jnp.concatenate([hi, lo], axis=1), after, preferred_element_type=F32)
            a = jnp.exp(log_beta + later + tails[h])
            if strict is not None:
                a = jnp.where(strict, a, 0.0)
            new_tails.append(tails[h] + jnp.sum(log_keep, axis=1, keepdims=True))
            ph = jnp.dot(a.astype(BF16), vt, preferred_element_type=F32)
            acc = acc + jnp.where(out_masks[h], ph, 0.0)
        return tuple(new_tails), acc

    last = n_tiles - 1
    row_pos = i * qb + lax.broadcasted_iota(jnp.int32, (qb, K_TILE), 0)
    col_pos = last * K_TILE + lax.broadcasted_iota(jnp.int32, (qb, K_TILE), 1)
    init = (tuple(jnp.zeros((qb, 1), F32) for _ in range(N_HEADS)), jnp.zeros((qb, width), F32))
    carry = tile(last, init, col_pos < row_pos)
    _, acc = lax.fori_loop(0, last, lambda t, c: tile(last - 1 - t, c, None), carry)
    o_ref[...] = acc.astype(o_ref.dtype)


def _stick_breaking(qc, kc, vc, batch, seq):
    n = qc.shape[0]
    nq = seq // Q_BLOCK
    qrow = pl.BlockSpec((Q_BLOCK, BRANCH_WIDTH), lambda b, i: (b * nq + i, 0))
    whole = pl.BlockSpec((seq, BRANCH_WIDTH), lambda b, i: (b, 0))
    return pl.pallas_call(
        _stick_kernel,
        grid=(batch, nq),
        in_specs=[qrow, whole, whole],
        out_specs=qrow,
        out_shape=jax.ShapeDtypeStruct((n, BRANCH_WIDTH), BF16),
        compiler_params=_params("arbitrary", "arbitrary"),
        name="stick_breaking",
    )(qc, kc, vc)


def _merge_kernel(x_ref, g1_ref, ya_ref, yb_ref, yc_ref, yd_ref, wg_ref, wb_ref, wo_ref,
                  g2_ref, wr_ref, br_ref, x1_o, comb_o):
    x = x_ref[...]
    d = x.shape[1]
    h = _rms_bf16(x, g1_ref[...])
    ys = (ya_ref[...], yb_ref[...], yc_ref[...], yd_ref[...])
    col = 256
    pieces = []
    for c in range(d // col):
        m = None
        for n in range(N_BRANCH):
            logit = jnp.dot(h, wg_ref[:, n * d + c * col:n * d + (c + 1) * col],
                            preferred_element_type=F32)
            br = jnp.dot(ys[n], wb_ref[n, :, c * col:(c + 1) * col], preferred_element_type=F32)
            t = jax.nn.sigmoid(logit) * br
            m = t if m is None else m + t
        pieces.append(m.astype(BF16))
    merged = jnp.concatenate(pieces, axis=1)
    x1 = x + jnp.dot(merged, wo_ref[...], preferred_element_type=F32)
    x1_o[...] = x1

    ms = jnp.mean(x1 * x1, axis=-1, keepdims=True)
    h2 = x1 * lax.rsqrt(ms + EPS) * g2_ref[...]
    logits = jnp.dot(h2, wr_ref[...], preferred_element_type=F32,
                     precision=lax.Precision.HIGHEST) + br_ref[...]
    rows = logits.shape[0]
    lane = lax.broadcasted_iota(jnp.int32, (rows, LANES), 1)
    neg = -jnp.inf
    is_grp = (lane >= N_EXPERTS) & (lane < N_EXPERTS + N_EXPERT_GROUPS)
    gl = jnp.where(is_grp, logits, neg)
    gmax = jnp.max(gl, axis=1, keepdims=True)
    gidx = jnp.min(jnp.where(gl == gmax, lane, LANES), axis=1, keepdims=True) - N_EXPERTS
    grp_p = 1.0 / jnp.sum(jnp.where(is_grp, jnp.exp(gl - gmax), 0.0), axis=1, keepdims=True)
    in_grp = (lane >= gidx * EXPERTS_PER_GROUP) & (lane < (gidx + 1) * EXPERTS_PER_GROUP)
    el = jnp.where(in_grp, logits, neg)
    l1 = jnp.max(el, axis=1, keepdims=True)
    j1 = jnp.min(jnp.where(el == l1, lane, LANES), axis=1, keepdims=True)
    el2 = jnp.where(lane == j1, neg, el)
    l2 = jnp.max(el2, axis=1, keepdims=True)
    j2 = jnp.min(jnp.where(el2 == l2, lane, LANES), axis=1, keepdims=True)
    e2 = jnp.exp(l2 - l1)
    w1 = grp_p / (1.0 + e2)
    w2 = grp_p * e2 / (1.0 + e2)
    comb_o[...] = jnp.where(lane == j1, w1, 0.0) + jnp.where(lane == j2, w2, 0.0)


def _merge(x2, norm1_g, ys, w_gate, w_branch, w_out, norm2_g, w_router, b_router):
    n, d = x2.shape
    rows = PROJ_ROWS
    row = lambda w: pl.BlockSpec((rows, w), lambda i: (i, 0))
    full = lambda a: pl.BlockSpec(a.shape, lambda i: (0,) * a.ndim)
    args = (x2, norm1_g.reshape(1, d), *ys, w_gate, w_branch, w_out, norm2_g.reshape(1, d),
            w_router, b_router)
    in_specs = [row(d), full(args[1])] + [row(BRANCH_WIDTH)] * 4 + [full(a) for a in args[6:]]
    return pl.pallas_call(
        _merge_kernel,
        grid=(n // rows,),
        in_specs=in_specs,
        out_specs=[row(d), row(LANES)],
        out_shape=[jax.ShapeDtypeStruct((n, d), F32), jax.ShapeDtypeStruct((n, LANES), F32)],
        compiler_params=_params("arbitrary"),
        name="merge_router",
    )(*args)


def _moe_kernel(x_ref, g2_ref, comb_ref, wg_ref, wu_ref, wd_ref, gf_ref, o_ref, h_scr, acc_scr,
                *, final_norm):
    e = pl.program_id(1)

    @pl.when(e == 0)
    def _():
        h_scr[...] = _rms_bf16(x_ref[...], g2_ref[...])
        acc_scr[...] = jnp.zeros(acc_scr.shape, F32)

    h = h_scr[...]
    a = jnp.dot(h, wg_ref[0], preferred_element_type=F32)
    u = jnp.dot(h, wu_ref[0], preferred_element_type=F32)
    comb = comb_ref[...]
    lane = lax.broadcasted_iota(jnp.int32, comb.shape, 1)
    w = jnp.sum(jnp.where(lane == e, comb, 0.0), axis=1, keepdims=True)
    hh = (jax.nn.silu(a) * u * w).astype(BF16)
    acc_scr[...] += jnp.dot(hh, wd_ref[0], preferred_element_type=F32)

    @pl.when(e == pl.num_programs(1) - 1)
    def _():
        y = x_ref[...] + acc_scr[...]
        if final_norm:
            ms = jnp.mean(y * y, axis=-1, keepdims=True)
            y = y * lax.rsqrt(ms + EPS) * gf_ref[...]
        o_ref[...] = y


def _moe(x1, norm2_g, comb, w_g, w_u, w_d, norm_f_g, final_norm):
    n, d = x1.shape
    rows = min(MOE_ROWS, n)
    n_exp = w_g.shape[0]
    row = lambda w: pl.BlockSpec((rows, w), lambda i, e: (i, 0))
    vec = pl.BlockSpec((1, d), lambda i, e: (0, 0))
    return pl.pallas_call(
        functools.partial(_moe_kernel, final_norm=final_norm),
        grid=(n // rows, n_exp),
        in_specs=[row(d), vec, row(LANES),
                  pl.BlockSpec((1, d, D_EXPERT), lambda i, e: (e, 0, 0)),
                  pl.BlockSpec((1, d, D_EXPERT), lambda i, e: (e, 0, 0)),
                  pl.BlockSpec((1, D_EXPERT, d), lambda i, e: (e, 0, 0)),
                  vec],
        out_specs=row(d),
        out_shape=jax.ShapeDtypeStruct((n, d), F32),
        scratch_shapes=[pltpu.VMEM((rows, d), BF16), pltpu.VMEM((rows, d), F32)],
        compiler_params=_params("arbitrary", "arbitrary"),
        name="experts",
    )(x1, norm2_g.reshape(1, d), comb, w_g, w_u, w_d, norm_f_g.reshape(1, d))


def kernel(x, positions, norm1_g, w_in, sgu_ln_g, sgu_ln_b, w_spatial, b_spatial, conv_w, w_branch,
           w_out, norm2_g, w_router_group, b_router_group, w_router_expert, b_router_expert,
           w_exp_gate, w_exp_up, w_exp_down, norm_f_g):
    batch, seq, d = x.shape
    depth = w_in.shape[0]
    n = batch * seq
    assert seq % K_TILE == 0 and seq % COUNT_CHUNK == 0 and n % PROJ_ROWS == 0
    x2 = x.reshape(n, d)
    tables = _rope_tables(positions)
    for l in range(depth):
        w_a, gate_off = _proj_weight(w_in[l])
        w_gate = w_in[l][:, gate_off:].astype(BF16)
        pad = jnp.zeros((d, LANES - N_EXPERTS - N_EXPERT_GROUPS), F32)
        w_router = jnp.concatenate([w_router_expert[l], w_router_group[l], pad], axis=1)
        b_router = jnp.concatenate([b_router_expert[l], b_router_group[l], pad[0]]).reshape(1, LANES)
        qa, ka, va, iq, ik, iw, y_b, qc, kc, vc, y_d = _project(
            x2, norm1_g[l], w_a, tables, sgu_ln_g[l], sgu_ln_b[l], w_spatial[l], b_spatial[l],
            conv_w[l], seq)
        y_a = _dsa(iq, iw, qa, ik, ka, va, batch, seq)
        y_c = _stick_breaking(qc, kc, vc, batch, seq)
        x1, comb = _merge(x2, norm1_g[l], (y_a, y_b, y_c, y_d), w_gate, w_branch[l].astype(BF16),
                          w_out[l].astype(BF16), norm2_g[l], w_router, b_router)
        x2 = _moe(x1, norm2_g[l], comb, w_exp_gate[l].astype(BF16), w_exp_up[l].astype(BF16),
                  w_exp_down[l].astype(BF16), norm_f_g, final_norm=(l == depth - 1))
    return x2.reshape(batch, seq, d)
```

```python
import functools

import jax
import jax.numpy as jnp
from jax import lax
from jax.experimental import pallas as pl
from jax.experimental.pallas import tpu as pltpu

F32 = jnp.float32
BF16 = jnp.bfloat16

EPS = 1e-6
ROPE_THETA = 10000.0
HEAD_DIM = 64
BRANCH_WIDTH = 256
N_HEADS = 4
IDX_DIM = 32
TOPK_MAX = 256
CHUNK = 128
N_GROUPS_B = 4
CONV_WIDTH = 3
N_BRANCH = 4
N_EXPERT_GROUPS = 4
EXPERTS_PER_GROUP = 8
N_EXPERTS = N_EXPERT_GROUPS * EXPERTS_PER_GROUP
D_EXPERT = 256

LANES = 128
SUBLANES = 8
Q_BLOCK = 256
K_TILE = Q_BLOCK
COUNT_CHUNK = 512
COUNT_ROWS = 64
PROJ_ROWS = 256
MOE_ROWS = 1024
CONV_HALO = 8
VMEM_LIMIT = 56 * 1024 * 1024
NEG_BIG = -1e30
INT_MIN = -2 ** 31

_OFF = {}
_o = 0
for _name, _w in (("q", 512), ("k", 512), ("iq", 256), ("ik", 256),
                  ("ub", 256), ("vb", 256), ("qc", 256), ("kc", 256),
                  ("bd", 256), ("cd", 256), ("xd", 256)):
    _OFF[_name] = (_o, _w)
    _o += _w
W_A_COLS = _o
T_VA, T_VC, T_IW = 0, BRANCH_WIDTH, 2 * BRANCH_WIDTH
W_T_ROWS = 2 * BRANCH_WIDTH + 16

_CONTRACT_LAST = (((1,), (1,)), ((), ()))


def _params(*sem):
    return pltpu.CompilerParams(dimension_semantics=sem, vmem_limit_bytes=VMEM_LIMIT)


def _rope_table_kernel(pos_ref, inv_a_ref, inv_i_ref, cos_a, sin_a, cos_i, sin_i):
    p = pos_ref[...]
    a = p * inv_a_ref[...]
    cos_a[...] = jnp.cos(a)
    sin_a[...] = jnp.sin(a)
    b = p * inv_i_ref[...]
    cos_i[...] = jnp.cos(b)
    sin_i[...] = jnp.sin(b)


def _rope_tables(positions):
    n = positions.size
    rows = PROJ_ROWS
    pos = positions.astype(F32).reshape(n, 1)
    inv_a = ROPE_THETA ** (-jnp.arange(0, HEAD_DIM, 2, dtype=F32) / HEAD_DIM)
    inv_i = ROPE_THETA ** (-jnp.arange(0, IDX_DIM, 2, dtype=F32) / IDX_DIM)
    inv_a = jnp.tile(inv_a, LANES // inv_a.size).reshape(1, LANES)
    inv_i = jnp.tile(inv_i, LANES // inv_i.size).reshape(1, LANES)
    tab = jax.ShapeDtypeStruct((n, LANES), F32)
    row_spec = pl.BlockSpec((rows, LANES), lambda i: (i, 0))
    const = pl.BlockSpec((1, LANES), lambda i: (0, 0))
    return pl.pallas_call(
        _rope_table_kernel,
        grid=(n // rows,),
        in_specs=[pl.BlockSpec((rows, 1), lambda i: (i, 0)), const, const],
        out_specs=[row_spec] * 4,
        out_shape=[tab] * 4,
        compiler_params=_params("arbitrary"),
        name="rope_tables",
    )(pos, inv_a, inv_i)


def _rms_bf16(x, g):
    ms = jnp.mean(x * x, axis=-1, keepdims=True)
    return (x * lax.rsqrt(ms + EPS) * g).astype(BF16)


def _proj_kernel(x_ref, g_ref, w_ref, wt_ref, cos_a, sin_a, cos_i, sin_i, lng_ref, lnb_ref, ws_ref,
                 bs_ref, cw_ref,
                 qa_o, ka_o, vat_o, iq_o, ik_o, iwt_o, yb_o, qc_o, kc_o, vct_o, yd_o,
                 zbuf, *, steps_per_seq):
    rows = x_ref.shape[0]
    h = _rms_bf16(x_ref[...], g_ref[...])

    def mm(name):
        off, width = _OFF[name]
        return jnp.dot(h, w_ref[:, off:off + width], preferred_element_type=F32)

    def rope(name, cos_ref, sin_ref, out_ref, scale):
        p = mm(name)
        w = p.shape[1] // 2
        c = cos_ref[...]
        s = sin_ref[...]
        for j in range(w // LANES):
            lo = j * LANES
            r = p[:, lo:lo + LANES] * c + p[:, w + lo:w + lo + LANES] * s
            out_ref[:, lo:lo + LANES] = (r if scale == 1.0 else r * scale).astype(out_ref.dtype)

    att_scale = HEAD_DIM ** -0.5
    rope("q", cos_a, sin_a, qa_o, att_scale)
    rope("k", cos_a, sin_a, ka_o, 1.0)
    rope("iq", cos_i, sin_i, iq_o, 1.0)
    rope("ik", cos_i, sin_i, ik_o, 1.0)
    qc_o[...] = (mm("qc") * att_scale).astype(BF16)
    kc_o[...] = mm("kc").astype(BF16)
    t = lax.dot_general(wt_ref[...], h, _CONTRACT_LAST, preferred_element_type=F32)
    vat_o[...] = t[T_VA:T_VA + BRANCH_WIDTH, :].astype(BF16)
    vct_o[...] = t[T_VC:T_VC + BRANCH_WIDTH, :].astype(BF16)
    iwt_o[...] = t[T_IW:T_IW + SUBLANES, :]

    u = jax.nn.gelu(mm("ub"))
    v = jax.nn.gelu(mm("vb"))
    mu = jnp.mean(v, axis=-1, keepdims=True)
    var = jnp.mean(jnp.square(v - mu), axis=-1, keepdims=True)
    vn = ((v - mu) * lax.rsqrt(var + EPS) * lng_ref[...] + lnb_ref[...]).astype(BF16)
    t_idx = lax.broadcasted_iota(jnp.int32, (CHUNK, CHUNK), 0)
    s_idx = lax.broadcasted_iota(jnp.int32, (CHUNK, CHUNK), 1)
    causal = s_idx <= t_idx
    group_masks = [_head_lane_mask(BRANCH_WIDTH, BRANCH_WIDTH // N_GROUPS_B, g) for g in range(N_GROUPS_B)]
    w_tril = [jnp.where(causal, ws_ref[g], 0.0).astype(BF16) for g in range(N_GROUPS_B)]
    for c in range(rows // CHUNK):
        vc = vn[c * CHUNK:(c + 1) * CHUNK, :]
        mixed = bs_ref[...]
        for g in range(N_GROUPS_B):
            full = jnp.dot(w_tril[g], vc, preferred_element_type=F32)
            mixed = mixed + jnp.where(group_masks[g], full, 0.0)
        yb_o[c * CHUNK:(c + 1) * CHUNK, :] = (u[c * CHUNK:(c + 1) * CHUNK, :] * mixed).astype(BF16)

    bd = mm("bd")
    z = mm("cd") * mm("xd")

    @pl.when(pl.program_id(0) % steps_per_seq == 0)
    def _():
        zbuf[0:CONV_HALO, :] = jnp.zeros((CONV_HALO, BRANCH_WIDTH), F32)

    zbuf[CONV_HALO:CONV_HALO + rows, :] = z
    y = cw_ref[CONV_WIDTH - 1:CONV_WIDTH, :] * z
    for tap in range(CONV_WIDTH - 1):
        shift = CONV_WIDTH - 1 - tap
        y = y + cw_ref[tap:tap + 1, :] * zbuf[CONV_HALO - shift:CONV_HALO - shift + rows, :]
    yd_o[...] = (bd * y).astype(BF16)
    zbuf[0:CONV_HALO, :] = z[rows - CONV_HALO:rows, :]


def _swap_halves(w, dim):
    d_in, cols = w.shape
    wh = w.reshape(d_in, cols // dim, 2, dim // 2)
    return jnp.stack([-wh[:, :, 1], wh[:, :, 0]], axis=2).reshape(d_in, cols)


def _proj_weight(w_in):
    d = w_in.shape[0]
    sizes = (BRANCH_WIDTH,) * 3 + (N_HEADS * IDX_DIM, IDX_DIM, N_HEADS) + (BRANCH_WIDTH,) * 8
    parts, o = [], 0
    for s in sizes:
        parts.append(w_in[:, o:o + s])
        o += s
    qa, ka, va, iq, ik, iw, ub, vb, qc, kc, vc, bd, cd, xd = parts
    ik_rep = jnp.tile(ik, (1, N_HEADS))
    cols = [qa, _swap_halves(qa, HEAD_DIM), ka, _swap_halves(ka, HEAD_DIM),
            iq, _swap_halves(iq, IDX_DIM), ik_rep, _swap_halves(ik_rep, IDX_DIM),
            ub, vb, qc, kc, bd, cd, xd]
    iw_pad = jnp.concatenate([iw, jnp.zeros((d, W_T_ROWS - 2 * BRANCH_WIDTH - N_HEADS), w_in.dtype)], axis=1)
    w_t = jnp.concatenate([va, vc, iw_pad], axis=1).T
    return jnp.concatenate(cols, axis=1).astype(BF16), w_t.astype(BF16), o


def _project(x2, norm_g, w_a, w_t, tables, ln_g, ln_b, w_spatial, b_spatial, conv_w, seq):
    n, d = x2.shape
    rows = PROJ_ROWS
    cos_a, sin_a, cos_i, sin_i = tables
    bias = jnp.repeat(b_spatial.T, BRANCH_WIDTH // N_GROUPS_B, axis=1)
    row = lambda w: pl.BlockSpec((rows, w), lambda i: (i, 0))
    col = lambda r: pl.BlockSpec((r, rows), lambda i: (0, i))
    full = lambda a: pl.BlockSpec(a.shape, lambda i: (0,) * a.ndim)
    bf = lambda w: jax.ShapeDtypeStruct((n, w), BF16)
    bft = jax.ShapeDtypeStruct((BRANCH_WIDTH, n), BF16)
    args = (x2, norm_g.reshape(1, d), w_a, w_t, cos_a, sin_a, cos_i, sin_i, ln_g.reshape(1, -1),
            ln_b.reshape(1, -1), w_spatial, bias, conv_w)
    in_specs = [row(d), full(args[1]), full(w_a), full(w_t), row(LANES), row(LANES), row(LANES),
                row(LANES), full(args[8]), full(args[9]), full(w_spatial), full(bias), full(conv_w)]
    out_shape = [bf(256), bf(256), bft, bf(128), bf(128), jax.ShapeDtypeStruct((SUBLANES, n), F32),
                 bf(256), bf(256), bf(256), bft, bf(256)]
    out_specs = [row(256), row(256), col(BRANCH_WIDTH), row(128), row(128), col(SUBLANES),
                 row(256), row(256), row(256), col(BRANCH_WIDTH), row(256)]
    return pl.pallas_call(
        functools.partial(_proj_kernel, steps_per_seq=seq // rows),
        grid=(n // rows,),
        in_specs=in_specs,
        out_specs=out_specs,
        out_shape=out_shape,
        scratch_shapes=[pltpu.VMEM((CONV_HALO + rows, BRANCH_WIDTH), F32)],
        compiler_params=_params("arbitrary"),
        name="project",
    )(*args)


def _head_lane_mask(width, per_head, h, rows=1):
    lane = lax.broadcasted_iota(jnp.int32, (rows, width), 1)
    return (lane >= h * per_head) & (lane < (h + 1) * per_head)


def _masked_heads(x, per_head):
    return [jnp.where(_head_lane_mask(x.shape[1], per_head, h), x, jnp.zeros_like(x))
            for h in range(N_HEADS)]


def _key_to_float(key):
    bits = key ^ ((key >> 31) & jnp.int32(0x7FFFFFFF))
    return lax.bitcast_convert_type(bits, F32)


def _col_reduce(x, op, final):
    while x.shape[0] > SUBLANES:
        half = x.shape[0] // 2
        x = op(x[:half], x[half:])
    return final(x, axis=0, keepdims=True)


def _col_sum(x):
    return _col_reduce(x, jnp.add, jnp.sum)


def _col_max(x):
    return _col_reduce(x, jnp.maximum, jnp.max)


def _dsa_kernel(iq_ref, iwt_ref, q_ref, ik_ref, k_ref, vt_ref, o_ref, sc_ref, *, topk):
    qb = q_ref.shape[0]
    i = pl.program_id(1)
    n_tiles = i + 1

    @pl.when(i == 0)
    def _():
        sc_ref[...] = jnp.full(sc_ref.shape, -jnp.inf, F32)

    iq_heads = _masked_heads(iq_ref[...], IDX_DIM)
    iwt = iwt_ref[...]
    iw_heads = [iwt[h:h + 1, :] for h in range(N_HEADS)]

    def score_tile(j):
        start = pl.multiple_of(j * K_TILE, K_TILE)
        ik = ik_ref[pl.ds(start, K_TILE), :]
        sc = None
        for h in range(N_HEADS):
            d = lax.dot_general(ik, iq_heads[h], _CONTRACT_LAST, preferred_element_type=F32)
            t = iw_heads[h] * jnp.maximum(d, 0.0)
            sc = t if sc is None else sc + t
        return start, sc

    def full_tile(j, carry):
        start, sc = score_tile(j)
        sc_ref[pl.ds(start, K_TILE), :] = sc
        return carry

    lax.fori_loop(0, n_tiles - 1, full_tile, 0)
    start, sc = score_tile(n_tiles - 1)
    key_pos = lax.broadcasted_iota(jnp.int32, (K_TILE, qb), 0)
    qry_pos = lax.broadcasted_iota(jnp.int32, (K_TILE, qb), 1)
    sc_ref[pl.ds(start, K_TILE), :] = jnp.where(key_pos <= qry_pos, sc, -jnp.inf)

    n_chunks = (n_tiles * K_TILE + COUNT_CHUNK - 1) // COUNT_CHUNK
    kf = jnp.float32(topk)

    def count(pred):
        def body(c, acc):
            base = pl.multiple_of(c * COUNT_CHUNK, COUNT_CHUNK)
            for r in range(COUNT_CHUNK // COUNT_ROWS):
                blk = sc_ref[pl.ds(base + r * COUNT_ROWS, COUNT_ROWS), :]
                acc = acc + jnp.where(pred(blk), 1.0, 0.0)
            return acc
        acc = lax.fori_loop(0, n_chunks, body, jnp.zeros((COUNT_ROWS, qb), F32))
        return jnp.sum(acc, axis=0, keepdims=True)

    def bit_step(t, carry):
        key, cnt_key = carry
        cand = key + lax.shift_left(jnp.int32(1), 31 - t)
        cand_f = _key_to_float(cand)
        cnt = count(lambda blk: blk >= cand_f)
        ok = cnt >= kf
        return jnp.where(ok, cand, key), jnp.where(ok, cnt, cnt_key)

    key0 = jnp.full((1, qb), INT_MIN, jnp.int32)
    key, cnt_key = lax.fori_loop(0, 32, bit_step, (key0, jnp.full((1, qb), kf, F32)))
    few = key == INT_MIN
    thr = jnp.where(few, jnp.finfo(F32).min, _key_to_float(key))

    @pl.when(jnp.max(jnp.where(few, kf, cnt_key)) > kf)
    def _():
        need = kf - count(lambda blk: blk > thr)
        before = (lax.broadcasted_iota(jnp.int32, (K_TILE, K_TILE), 1)
                  < lax.broadcasted_iota(jnp.int32, (K_TILE, K_TILE), 0))
        before = jnp.where(before, 1.0, 0.0).astype(BF16)

        def tie_body(j, seen):
            start = pl.multiple_of(j * K_TILE, K_TILE)
            blk = sc_ref[pl.ds(start, K_TILE), :]
            eq = blk == thr
            eq_f = jnp.where(eq, 1.0, 0.0)
            rank = seen + jnp.dot(before, eq_f.astype(BF16), preferred_element_type=F32)
            sc_ref[pl.ds(start, K_TILE), :] = jnp.where(eq & (rank >= need), -jnp.inf, blk)
            return seen + _col_sum(eq_f)

        lax.fori_loop(0, n_tiles, tie_body, jnp.zeros((1, qb), F32))

    q_heads = _masked_heads(q_ref[...], HEAD_DIM)

    def att_tile(j, carry):
        m, l, acc = carry
        start = pl.multiple_of(j * K_TILE, K_TILE)
        sel = sc_ref[pl.ds(start, K_TILE), :] >= thr
        kt = k_ref[pl.ds(start, K_TILE), :]
        logits = [lax.dot_general(kt, q_heads[h], _CONTRACT_LAST, preferred_element_type=F32)
                  for h in range(N_HEADS)]
        m_new, l_new, alphas, probs = [], [], [], []
        for h in range(N_HEADS):
            s = jnp.where(sel, logits[h], NEG_BIG)
            mh = jnp.maximum(m[h], _col_max(s))
            p = jnp.exp(s - mh)
            alpha = jnp.exp(m[h] - mh)
            m_new.append(mh)
            alphas.append(alpha)
            l_new.append(alpha * l[h] + _col_sum(p))
            probs.append(p.astype(BF16))
        acc_new = []
        for h in range(N_HEADS):
            vt = vt_ref[h * HEAD_DIM:(h + 1) * HEAD_DIM, pl.ds(start, K_TILE)]
            pv = jnp.dot(vt, probs[h], preferred_element_type=F32)
            acc_new.append(alphas[h] * acc[h] + pv)
        return tuple(m_new), tuple(l_new), tuple(acc_new)

    init = (tuple(jnp.full((1, qb), NEG_BIG, F32) for _ in range(N_HEADS)),
            tuple(jnp.zeros((1, qb), F32) for _ in range(N_HEADS)),
            tuple(jnp.zeros((HEAD_DIM, qb), F32) for _ in range(N_HEADS)))
    _, l, acc = lax.fori_loop(0, n_tiles, att_tile, init)
    out_t = jnp.concatenate([acc[h] / l[h] for h in range(N_HEADS)], axis=0)
    o_ref[...] = out_t.T.astype(o_ref.dtype)


def _dsa(iq, iwt, qa, ik, ka, vat, batch, seq):
    n = qa.shape[0]
    nq = seq // Q_BLOCK
    topk = min(TOPK_MAX, seq // 4)
    qrow = lambda w: pl.BlockSpec((Q_BLOCK, w), lambda b, i: (b * nq + i, 0))
    whole = lambda w: pl.BlockSpec((seq, w), lambda b, i: (b, 0))
    return pl.pallas_call(
        functools.partial(_dsa_kernel, topk=topk),
        grid=(batch, nq),
        in_specs=[qrow(LANES), pl.BlockSpec((SUBLANES, Q_BLOCK), lambda b, i: (0, b * nq + i)),
                  qrow(BRANCH_WIDTH), whole(LANES), whole(BRANCH_WIDTH),
                  pl.BlockSpec((BRANCH_WIDTH, seq), lambda b, i: (0, b))],
        out_specs=qrow(BRANCH_WIDTH),
        out_shape=jax.ShapeDtypeStruct((n, BRANCH_WIDTH), BF16),
        scratch_shapes=[pltpu.VMEM((seq, Q_BLOCK), F32)],
        compiler_params=_params("arbitrary", "arbitrary"),
        name="dsa",
    )(iq, iwt, qa, ik, ka, vat)


def _stick_kernel(q_ref, k_ref, vt_ref, o_ref):
    qb = q_ref.shape[0]
    i = pl.program_id(1)
    q_heads = _masked_heads(q_ref[...], HEAD_DIM)
    after = (lax.broadcasted_iota(jnp.int32, (K_TILE, 2 * K_TILE), 1) % K_TILE
             > lax.broadcasted_iota(jnp.int32, (K_TILE, 2 * K_TILE), 0))
    after = jnp.where(after, 1.0, 0.0).astype(BF16)

    def tile(j, carry, strict):
        tails, acc = carry
        start = pl.multiple_of(j * K_TILE, K_TILE)
        kt = k_ref[pl.ds(start, K_TILE), :]
        zs = [lax.dot_general(kt, q_heads[h], _CONTRACT_LAST, preferred_element_type=F32)
              for h in range(N_HEADS)]
        log_betas, splits, new_tails = [], [], []
        for h in range(N_HEADS):
            z = zs[h]
            log_beta = jnp.minimum(z, 0.0) - jnp.log1p(jnp.exp(-jnp.abs(z)))
            log_keep = log_beta - z
            if strict is not None:
                log_keep = jnp.where(strict, log_keep, 0.0)
            hi = log_keep.astype(BF16)
            lo = (log_keep - hi.astype(F32)).astype(BF16)
            log_betas.append(log_beta)
            splits.append(jnp.concatenate([hi, lo], axis=0))
            new_tails.append(tails[h] + _col_sum(log_keep))
        laters = [jnp.dot(after, splits[h], preferred_element_type=F32) for h in range(N_HEADS)]
        weights = []
        for h in range(N_HEADS):
            a = jnp.exp(log_betas[h] + laters[h] + tails[h])
            if strict is not None:
                a = jnp.where(strict, a, 0.0)
            weights.append(a.astype(BF16))
        new_acc = []
        for h in range(N_HEADS):
            vt = vt_ref[h * HEAD_DIM:(h + 1) * HEAD_DIM, pl.ds(start, K_TILE)]
            new_acc.append(acc[h] + jnp.dot(vt, weights[h], preferred_element_type=F32))
        return tuple(new_tails), tuple(new_acc)

    key_pos = lax.broadcasted_iota(jnp.int32, (K_TILE, qb), 0)
    qry_pos = lax.broadcasted_iota(jnp.int32, (K_TILE, qb), 1)
    init = (tuple(jnp.zeros((1, qb), F32) for _ in range(N_HEADS)),
            tuple(jnp.zeros((HEAD_DIM, qb), F32) for _ in range(N_HEADS)))
    carry = tile(i, init, key_pos < qry_pos)
    _, acc = lax.fori_loop(0, i, lambda t, c: tile(i - 1 - t, c, None), carry)
    o_ref[...] = jnp.concatenate(acc, axis=0).T.astype(o_ref.dtype)


def _stick_breaking(qc, kc, vct, batch, seq):
    n = qc.shape[0]
    nq = seq // Q_BLOCK
    qrow = pl.BlockSpec((Q_BLOCK, BRANCH_WIDTH), lambda b, i: (b * nq + i, 0))
    return pl.pallas_call(
        _stick_kernel,
        grid=(batch, nq),
        in_specs=[qrow, pl.BlockSpec((seq, BRANCH_WIDTH), lambda b, i: (b, 0)),
                  pl.BlockSpec((BRANCH_WIDTH, seq), lambda b, i: (0, b))],
        out_specs=qrow,
        out_shape=jax.ShapeDtypeStruct((n, BRANCH_WIDTH), BF16),
        compiler_params=_params("arbitrary", "arbitrary"),
        name="stick_breaking",
    )(qc, kc, vct)


def _merge_kernel(x_ref, g1_ref, ya_ref, yb_ref, yc_ref, yd_ref, wg_ref, wb_ref, wo_ref,
                  g2_ref, wr_ref, br_ref, x1_o, comb_o):
    x = x_ref[...]
    d = x.shape[1]
    h = _rms_bf16(x, g1_ref[...])
    ys = (ya_ref[...], yb_ref[...], yc_ref[...], yd_ref[...])
    col = 256
    pieces = []
    for c in range(d // col):
        m = None
        for n in range(N_BRANCH):
            logit = jnp.dot(h, wg_ref[:, n * d + c * col:n * d + (c + 1) * col],
                            preferred_element_type=F32)
            br = jnp.dot(ys[n], wb_ref[n, :, c * col:(c + 1) * col], preferred_element_type=F32)
            t = jax.nn.sigmoid(logit) * br
            m = t if m is None else m + t
        pieces.append(m.astype(BF16))
    merged = jnp.concatenate(pieces, axis=1)
    x1 = x + jnp.dot(merged, wo_ref[...], preferred_element_type=F32)
    x1_o[...] = x1

    ms = jnp.mean(x1 * x1, axis=-1, keepdims=True)
    h2 = x1 * lax.rsqrt(ms + EPS) * g2_ref[...]
    logits = jnp.dot(h2, wr_ref[...], preferred_element_type=F32,
                     precision=lax.Precision.HIGHEST) + br_ref[...]
    rows = logits.shape[0]
    lane = lax.broadcasted_iota(jnp.int32, (rows, LANES), 1)
    neg = -jnp.inf
    is_grp = (lane >= N_EXPERTS) & (lane < N_EXPERTS + N_EXPERT_GROUPS)
    gl = jnp.where(is_grp, logits, neg)
    gmax = jnp.max(gl, axis=1, keepdims=True)
    gidx = jnp.min(jnp.where(gl == gmax, lane, LANES), axis=1, keepdims=True) - N_EXPERTS
    grp_p = 1.0 / jnp.sum(jnp.where(is_grp, jnp.exp(gl - gmax), 0.0), axis=1, keepdims=True)
    in_grp = (lane >= gidx * EXPERTS_PER_GROUP) & (lane < (gidx + 1) * EXPERTS_PER_GROUP)
    el = jnp.where(in_grp, logits, neg)
    l1 = jnp.max(el, axis=1, keepdims=True)
    j1 = jnp.min(jnp.where(el == l1, lane, LANES), axis=1, keepdims=True)
    el2 = jnp.where(lane == j1, neg, el)
    l2 = jnp.max(el2, axis=1, keepdims=True)
    j2 = jnp.min(jnp.where(el2 == l2, lane, LANES), axis=1, keepdims=True)
    e2 = jnp.exp(l2 - l1)
    w1 = grp_p / (1.0 + e2)
    w2 = grp_p * e2 / (1.0 + e2)
    comb_o[...] = jnp.where(lane == j1, w1, 0.0) + jnp.where(lane == j2, w2, 0.0)


def _merge(x2, norm1_g, ys, w_gate, w_branch, w_out, norm2_g, w_router, b_router):
    n, d = x2.shape
    rows = PROJ_ROWS
    row = lambda w: pl.BlockSpec((rows, w), lambda i: (i, 0))
    full = lambda a: pl.BlockSpec(a.shape, lambda i: (0,) * a.ndim)
    args = (x2, norm1_g.reshape(1, d), *ys, w_gate, w_branch, w_out, norm2_g.reshape(1, d),
            w_router, b_router)
    in_specs = [row(d), full(args[1])] + [row(BRANCH_WIDTH)] * 4 + [full(a) for a in args[6:]]
    return pl.pallas_call(
        _merge_kernel,
        grid=(n // rows,),
        in_specs=in_specs,
        out_specs=[row(d), row(LANES)],
        out_shape=[jax.ShapeDtypeStruct((n, d), F32), jax.ShapeDtypeStruct((n, LANES), F32)],
        compiler_params=_params("arbitrary"),
        name="merge_router",
    )(*args)


def _moe_kernel(x_ref, g2_ref, comb_ref, wg_ref, wu_ref, wd_ref, gf_ref, o_ref, h_scr, acc_scr,
                *, final_norm):
    e = pl.program_id(1)

    @pl.when(e == 0)
    def _():
        h_scr[...] = _rms_bf16(x_ref[...], g2_ref[...])
        acc_scr[...] = jnp.zeros(acc_scr.shape, F32)

    h = h_scr[...]
    a = jnp.dot(h, wg_ref[0], preferred_element_type=F32)
    u = jnp.dot(h, wu_ref[0], preferred_element_type=F32)
    comb = comb_ref[...]
    lane = lax.broadcasted_iota(jnp.int32, comb.shape, 1)
    w = jnp.sum(jnp.where(lane == e, comb, 0.0), axis=1, keepdims=True)
    hh = (jax.nn.silu(a) * u * w).astype(BF16)
    acc_scr[...] += jnp.dot(hh, wd_ref[0], preferred_element_type=F32)

    @pl.when(e == pl.num_programs(1) - 1)
    def _():
        y = x_ref[...] + acc_scr[...]
        if final_norm:
            ms = jnp.mean(y * y, axis=-1, keepdims=True)
            y = y * lax.rsqrt(ms + EPS) * gf_ref[...]
        o_ref[...] = y


def _moe(x1, norm2_g, comb, w_g, w_u, w_d, norm_f_g, final_norm):
    n, d = x1.shape
    rows = min(MOE_ROWS, n)
    n_exp = w_g.shape[0]
    row = lambda w: pl.BlockSpec((rows, w), lambda i, e: (i, 0))
    vec = pl.BlockSpec((1, d), lambda i, e: (0, 0))
    return pl.pallas_call(
        functools.partial(_moe_kernel, final_norm=final_norm),
        grid=(n // rows, n_exp),
        in_specs=[row(d), vec, row(LANES),
                  pl.BlockSpec((1, d, D_EXPERT), lambda i, e: (e, 0, 0)),
                  pl.BlockSpec((1, d, D_EXPERT), lambda i, e: (e, 0, 0)),
                  pl.BlockSpec((1, D_EXPERT, d), lambda i, e: (e, 0, 0)),
                  vec],
        out_specs=row(d),
        out_shape=jax.ShapeDtypeStruct((n, d), F32),
        scratch_shapes=[pltpu.VMEM((rows, d), BF16), pltpu.VMEM((rows, d), F32)],
        compiler_params=_params("arbitrary", "arbitrary"),
        name="experts",
    )(x1, norm2_g.reshape(1, d), comb, w_g, w_u, w_d, norm_f_g.reshape(1, d))


def kernel(x, positions, norm1_g, w_in, sgu_ln_g, sgu_ln_b, w_spatial, b_spatial, conv_w, w_branch,
           w_out, norm2_g, w_router_group, b_router_group, w_router_expert, b_router_expert,
           w_exp_gate, w_exp_up, w_exp_down, norm_f_g):
    batch, seq, d = x.shape
    depth = w_in.shape[0]
    n = batch * seq
    assert seq % COUNT_CHUNK == 0 and seq % Q_BLOCK == 0 and n % PROJ_ROWS == 0
    x2 = x.reshape(n, d)
    tables = _rope_tables(positions)
    for l in range(depth):
        w_a, w_t, gate_off = _proj_weight(w_in[l])
        w_gate = w_in[l][:, gate_off:].astype(BF16)
        pad = jnp.zeros((d, LANES - N_EXPERTS - N_EXPERT_GROUPS), F32)
        w_router = jnp.concatenate([w_router_expert[l], w_router_group[l], pad], axis=1)
        b_router = jnp.concatenate([b_router_expert[l], b_router_group[l], pad[0]]).reshape(1, LANES)
        qa, ka, vat, iq, ik, iwt, y_b, qc, kc, vct, y_d = _project(
            x2, norm1_g[l], w_a, w_t, tables, sgu_ln_g[l], sgu_ln_b[l], w_spatial[l], b_spatial[l],
            conv_w[l], seq)
        y_a = _dsa(iq, iwt, qa, ik, ka, vat, batch, seq)
        y_c = _stick_breaking(qc, kc, vct, batch, seq)
        x1, comb = _merge(x2, norm1_g[l], (y_a, y_b, y_c, y_d), w_gate, w_branch[l].astype(BF16),
                          w_out[l].astype(BF16), norm2_g[l], w_router, b_router)
        x2 = _moe(x1, norm2_g[l], comb, w_exp_gate[l].astype(BF16), w_exp_up[l].astype(BF16),
                  w_exp_down[l].astype(BF16), norm_f_g, final_norm=(l == depth - 1))
    return x2.reshape(batch, seq, d)
```

```python
import functools

import jax
import jax.numpy as jnp
from jax import lax
from jax.experimental import pallas as pl
from jax.experimental.pallas import tpu as pltpu

F32 = jnp.float32
BF16 = jnp.bfloat16

EPS = 1e-6
ROPE_THETA = 10000.0
HEAD_DIM = 64
BRANCH_WIDTH = 256
N_HEADS = 4
IDX_DIM = 32
TOPK_MAX = 256
CHUNK = 128
N_GROUPS_B = 4
CONV_WIDTH = 3
N_BRANCH = 4
N_EXPERT_GROUPS = 4
EXPERTS_PER_GROUP = 8
N_EXPERTS = N_EXPERT_GROUPS * EXPERTS_PER_GROUP
D_EXPERT = 256

LANES = 128
SUBLANES = 8
Q_BLOCK = 256
K_TILE = Q_BLOCK
COUNT_CHUNK = 512
COUNT_ROWS = 64
PROJ_ROWS = 256
MERGE_ROWS = 512
MOE_ROWS = 1024
CONV_HALO = 8
VMEM_LIMIT = 56 * 1024 * 1024
NEG_BIG = -1e30
INT_MIN = -2 ** 31

_OFF = {}
_o = 0
for _name, _w in (("q", 512), ("k", 512), ("iq", 256), ("ik", 256),
                  ("ub", 256), ("vb", 256), ("qc", 256), ("kc", 256),
                  ("bd", 256), ("cd", 256), ("xd", 256)):
    _OFF[_name] = (_o, _w)
    _o += _w
W_A_COLS = _o
T_VA, T_VC, T_IW = 0, BRANCH_WIDTH, 2 * BRANCH_WIDTH
W_T_ROWS = 2 * BRANCH_WIDTH + 16

_CONTRACT_LAST = (((1,), (1,)), ((), ()))


def _params(*sem):
    return pltpu.CompilerParams(dimension_semantics=sem, vmem_limit_bytes=VMEM_LIMIT)


def _rope_table_kernel(pos_ref, inv_a_ref, inv_i_ref, cos_a, sin_a, cos_i, sin_i):
    p = pos_ref[...]
    a = p * inv_a_ref[...]
    cos_a[...] = jnp.cos(a)
    sin_a[...] = jnp.sin(a)
    b = p * inv_i_ref[...]
    cos_i[...] = jnp.cos(b)
    sin_i[...] = jnp.sin(b)


def _rope_tables(positions):
    n = positions.size
    rows = PROJ_ROWS
    pos = positions.astype(F32).reshape(n, 1)
    inv_a = ROPE_THETA ** (-jnp.arange(0, HEAD_DIM, 2, dtype=F32) / HEAD_DIM)
    inv_i = ROPE_THETA ** (-jnp.arange(0, IDX_DIM, 2, dtype=F32) / IDX_DIM)
    inv_a = jnp.tile(inv_a, LANES // inv_a.size).reshape(1, LANES)
    inv_i = jnp.tile(inv_i, LANES // inv_i.size).reshape(1, LANES)
    tab = jax.ShapeDtypeStruct((n, LANES), F32)
    row_spec = pl.BlockSpec((rows, LANES), lambda i: (i, 0))
    const = pl.BlockSpec((1, LANES), lambda i: (0, 0))
    return pl.pallas_call(
        _rope_table_kernel,
        grid=(n // rows,),
        in_specs=[pl.BlockSpec((rows, 1), lambda i: (i, 0)), const, const],
        out_specs=[row_spec] * 4,
        out_shape=[tab] * 4,
        compiler_params=_params("arbitrary"),
        name="rope_tables",
    )(pos, inv_a, inv_i)


def _rms_bf16(x, g):
    ms = jnp.mean(x * x, axis=-1, keepdims=True)
    return (x * lax.rsqrt(ms + EPS) * g).astype(BF16)


def _proj_kernel(x_ref, g_ref, w_ref, wt_ref, cos_a, sin_a, cos_i, sin_i, lng_ref, lnb_ref, ws_ref,
                 bs_ref, cw_ref,
                 qa_o, ka_o, vat_o, iq_o, ik_o, iwt_o, yb_o, qc_o, kc_o, vct_o, yd_o,
                 zbuf, *, steps_per_seq):
    rows = x_ref.shape[0]
    h = _rms_bf16(x_ref[...], g_ref[...])

    def mm(name):
        off, width = _OFF[name]
        return jnp.dot(h, w_ref[:, off:off + width], preferred_element_type=F32)

    def rope(name, cos_ref, sin_ref, out_ref, scale):
        p = mm(name)
        w = p.shape[1] // 2
        c = cos_ref[...]
        s = sin_ref[...]
        for j in range(w // LANES):
            lo = j * LANES
            r = p[:, lo:lo + LANES] * c + p[:, w + lo:w + lo + LANES] * s
            out_ref[:, lo:lo + LANES] = (r if scale == 1.0 else r * scale).astype(out_ref.dtype)

    att_scale = HEAD_DIM ** -0.5
    rope("q", cos_a, sin_a, qa_o, att_scale)
    rope("k", cos_a, sin_a, ka_o, 1.0)
    rope("iq", cos_i, sin_i, iq_o, 1.0)
    rope("ik", cos_i, sin_i, ik_o, 1.0)
    qc_o[...] = (mm("qc") * att_scale).astype(BF16)
    kc_o[...] = mm("kc").astype(BF16)
    t = lax.dot_general(wt_ref[...], h, _CONTRACT_LAST, preferred_element_type=F32)
    vat_o[...] = t[T_VA:T_VA + BRANCH_WIDTH, :].astype(BF16)
    vct_o[...] = t[T_VC:T_VC + BRANCH_WIDTH, :].astype(BF16)
    iwt_o[...] = t[T_IW:T_IW + SUBLANES, :]

    u = jax.nn.gelu(mm("ub"))
    v = jax.nn.gelu(mm("vb"))
    mu = jnp.mean(v, axis=-1, keepdims=True)
    var = jnp.mean(jnp.square(v - mu), axis=-1, keepdims=True)
    vn = ((v - mu) * lax.rsqrt(var + EPS) * lng_ref[...] + lnb_ref[...]).astype(BF16)
    t_idx = lax.broadcasted_iota(jnp.int32, (CHUNK, CHUNK), 0)
    s_idx = lax.broadcasted_iota(jnp.int32, (CHUNK, CHUNK), 1)
    causal = s_idx <= t_idx
    group_masks = [_head_lane_mask(BRANCH_WIDTH, BRANCH_WIDTH // N_GROUPS_B, g) for g in range(N_GROUPS_B)]
    w_tril = [jnp.where(causal, ws_ref[g], 0.0).astype(BF16) for g in range(N_GROUPS_B)]
    for c in range(rows // CHUNK):
        vc = vn[c * CHUNK:(c + 1) * CHUNK, :]
        mixed = bs_ref[...]
        for g in range(N_GROUPS_B):
            full = jnp.dot(w_tril[g], vc, preferred_element_type=F32)
            mixed = mixed + jnp.where(group_masks[g], full, 0.0)
        yb_o[c * CHUNK:(c + 1) * CHUNK, :] = (u[c * CHUNK:(c + 1) * CHUNK, :] * mixed).astype(BF16)

    bd = mm("bd")
    z = mm("cd") * mm("xd")

    @pl.when(pl.program_id(0) % steps_per_seq == 0)
    def _():
        zbuf[0:CONV_HALO, :] = jnp.zeros((CONV_HALO, BRANCH_WIDTH), F32)

    zbuf[CONV_HALO:CONV_HALO + rows, :] = z
    y = cw_ref[CONV_WIDTH - 1:CONV_WIDTH, :] * z
    for tap in range(CONV_WIDTH - 1):
        shift = CONV_WIDTH - 1 - tap
        y = y + cw_ref[tap:tap + 1, :] * zbuf[CONV_HALO - shift:CONV_HALO - shift + rows, :]
    yd_o[...] = (bd * y).astype(BF16)
    zbuf[0:CONV_HALO, :] = z[rows - CONV_HALO:rows, :]


def _swap_halves(w, dim):
    d_in, cols = w.shape
    wh = w.reshape(d_in, cols // dim, 2, dim // 2)
    return jnp.stack([-wh[:, :, 1], wh[:, :, 0]], axis=2).reshape(d_in, cols)


def _proj_weight(w_in):
    d = w_in.shape[0]
    sizes = (BRANCH_WIDTH,) * 3 + (N_HEADS * IDX_DIM, IDX_DIM, N_HEADS) + (BRANCH_WIDTH,) * 8
    parts, o = [], 0
    for s in sizes:
        parts.append(w_in[:, o:o + s])
        o += s
    qa, ka, va, iq, ik, iw, ub, vb, qc, kc, vc, bd, cd, xd = parts
    ik_rep = jnp.tile(ik, (1, N_HEADS))
    cols = [qa, _swap_halves(qa, HEAD_DIM), ka, _swap_halves(ka, HEAD_DIM),
            iq, _swap_halves(iq, IDX_DIM), ik_rep, _swap_halves(ik_rep, IDX_DIM),
            ub, vb, qc, kc, bd, cd, xd]
    iw_pad = jnp.concatenate([iw, jnp.zeros((d, W_T_ROWS - 2 * BRANCH_WIDTH - N_HEADS), w_in.dtype)], axis=1)
    w_t = jnp.concatenate([va, vc, iw_pad], axis=1).T
    return jnp.concatenate(cols, axis=1).astype(BF16), w_t.astype(BF16), o


def _project(x2, norm_g, w_a, w_t, tables, ln_g, ln_b, w_spatial, b_spatial, conv_w, seq):
    n, d = x2.shape
    rows = PROJ_ROWS
    cos_a, sin_a, cos_i, sin_i = tables
    bias = jnp.repeat(b_spatial.T, BRANCH_WIDTH // N_GROUPS_B, axis=1)
    row = lambda w: pl.BlockSpec((rows, w), lambda i: (i, 0))
    col = lambda r: pl.BlockSpec((r, rows), lambda i: (0, i))
    full = lambda a: pl.BlockSpec(a.shape, lambda i: (0,) * a.ndim)
    bf = lambda w: jax.ShapeDtypeStruct((n, w), BF16)
    bft = jax.ShapeDtypeStruct((BRANCH_WIDTH, n), BF16)
    args = (x2, norm_g.reshape(1, d), w_a, w_t, cos_a, sin_a, cos_i, sin_i, ln_g.reshape(1, -1),
            ln_b.reshape(1, -1), w_spatial, bias, conv_w)
    in_specs = [row(d), full(args[1]), full(w_a), full(w_t), row(LANES), row(LANES), row(LANES),
                row(LANES), full(args[8]), full(args[9]), full(w_spatial), full(bias), full(conv_w)]
    out_shape = [bf(256), bf(256), bft, bf(128), bf(128), jax.ShapeDtypeStruct((SUBLANES, n), F32),
                 bf(256), bf(256), bf(256), bft, bf(256)]
    out_specs = [row(256), row(256), col(BRANCH_WIDTH), row(128), row(128), col(SUBLANES),
                 row(256), row(256), row(256), col(BRANCH_WIDTH), row(256)]
    return pl.pallas_call(
        functools.partial(_proj_kernel, steps_per_seq=seq // rows),
        grid=(n // rows,),
        in_specs=in_specs,
        out_specs=out_specs,
        out_shape=out_shape,
        scratch_shapes=[pltpu.VMEM((CONV_HALO + rows, BRANCH_WIDTH), F32)],
        compiler_params=_params("arbitrary"),
        name="project",
    )(*args)


def _head_lane_mask(width, per_head, h, rows=1):
    lane = lax.broadcasted_iota(jnp.int32, (rows, width), 1)
    return (lane >= h * per_head) & (lane < (h + 1) * per_head)


def _masked_heads(x, per_head):
    return [jnp.where(_head_lane_mask(x.shape[1], per_head, h), x, jnp.zeros_like(x))
            for h in range(N_HEADS)]


def _key_to_float(key):
    bits = key ^ ((key >> 31) & jnp.int32(0x7FFFFFFF))
    return lax.bitcast_convert_type(bits, F32)


def _col_reduce(x, op, final):
    while x.shape[0] > SUBLANES:
        half = x.shape[0] // 2
        x = op(x[:half], x[half:])
    return final(x, axis=0, keepdims=True)


def _col_sum(x):
    return _col_reduce(x, jnp.add, jnp.sum)


def _col_max(x):
    return _col_reduce(x, jnp.maximum, jnp.max)


def _dsa_kernel(iq_ref, iwt_ref, q_ref, ik_ref, k_ref, vt_ref, o_ref, sc_ref, *, topk):
    qb = q_ref.shape[0]
    i = pl.program_id(1)
    n_tiles = i + 1

    @pl.when(i == 0)
    def _():
        sc_ref[...] = jnp.full(sc_ref.shape, -jnp.inf, F32)

    iq_heads = _masked_heads(iq_ref[...], IDX_DIM)
    iwt = iwt_ref[...]
    iw_heads = [iwt[h:h + 1, :] for h in range(N_HEADS)]

    def score_tile(j):
        start = pl.multiple_of(j * K_TILE, K_TILE)
        ik = ik_ref[pl.ds(start, K_TILE), :]
        sc = None
        for h in range(N_HEADS):
            d = lax.dot_general(ik, iq_heads[h], _CONTRACT_LAST, preferred_element_type=F32)
            t = iw_heads[h] * jnp.maximum(d, 0.0)
            sc = t if sc is None else sc + t
        return start, sc

    def full_tile(j, carry):
        start, sc = score_tile(j)
        sc_ref[pl.ds(start, K_TILE), :] = sc
        return carry

    lax.fori_loop(0, n_tiles - 1, full_tile, 0)
    start, sc = score_tile(n_tiles - 1)
    key_pos = lax.broadcasted_iota(jnp.int32, (K_TILE, qb), 0)
    qry_pos = lax.broadcasted_iota(jnp.int32, (K_TILE, qb), 1)
    sc_ref[pl.ds(start, K_TILE), :] = jnp.where(key_pos <= qry_pos, sc, -jnp.inf)

    n_chunks = (n_tiles * K_TILE + COUNT_CHUNK - 1) // COUNT_CHUNK
    kf = jnp.float32(topk)

    def count(pred):
        def body(c, acc):
            base = pl.multiple_of(c * COUNT_CHUNK, COUNT_CHUNK)
            for r in range(COUNT_CHUNK // COUNT_ROWS):
                blk = sc_ref[pl.ds(base + r * COUNT_ROWS, COUNT_ROWS), :]
                acc = acc + jnp.where(pred(blk), 1.0, 0.0)
            return acc
        acc = lax.fori_loop(0, n_chunks, body, jnp.zeros((COUNT_ROWS, qb), F32))
        return jnp.sum(acc, axis=0, keepdims=True)

    def bit_step(t, carry):
        key, cnt_key = carry
        cand = key + lax.shift_left(jnp.int32(1), 31 - t)
        cand_f = _key_to_float(cand)
        cnt = count(lambda blk: blk >= cand_f)
        ok = cnt >= kf
        return jnp.where(ok, cand, key), jnp.where(ok, cnt, cnt_key)

    key0 = jnp.full((1, qb), INT_MIN, jnp.int32)
    key, cnt_key = lax.fori_loop(0, 32, bit_step, (key0, jnp.full((1, qb), kf, F32)))
    few = key == INT_MIN
    thr = jnp.where(few, jnp.finfo(F32).min, _key_to_float(key))

    @pl.when(jnp.max(jnp.where(few, kf, cnt_key)) > kf)
    def _():
        need = kf - count(lambda blk: blk > thr)
        before = (lax.broadcasted_iota(jnp.int32, (K_TILE, K_TILE), 1)
                  < lax.broadcasted_iota(jnp.int32, (K_TILE, K_TILE), 0))
        before = jnp.where(before, 1.0, 0.0).astype(BF16)

        def tie_body(j, seen):
            start = pl.multiple_of(j * K_TILE, K_TILE)
            blk = sc_ref[pl.ds(start, K_TILE), :]
            eq = blk == thr
            eq_f = jnp.where(eq, 1.0, 0.0)
            rank = seen + jnp.dot(before, eq_f.astype(BF16), preferred_element_type=F32)
            sc_ref[pl.ds(start, K_TILE), :] = jnp.where(eq & (rank >= need), -jnp.inf, blk)
            return seen + _col_sum(eq_f)

        lax.fori_loop(0, n_tiles, tie_body, jnp.zeros((1, qb), F32))

    q_heads = _masked_heads(q_ref[...], HEAD_DIM)

    def att_tile(j, carry):
        m, l, acc = carry
        start = pl.multiple_of(j * K_TILE, K_TILE)
        sel = sc_ref[pl.ds(start, K_TILE), :] >= thr
        kt = k_ref[pl.ds(start, K_TILE), :]
        logits = [lax.dot_general(kt, q_heads[h], _CONTRACT_LAST, preferred_element_type=F32)
                  for h in range(N_HEADS)]
        m_new, l_new, alphas, probs = [], [], [], []
        for h in range(N_HEADS):
            s = jnp.where(sel, logits[h], NEG_BIG)
            mh = jnp.maximum(m[h], _col_max(s))
            p = jnp.exp(s - mh)
            alpha = jnp.exp(m[h] - mh)
            m_new.append(mh)
            alphas.append(alpha)
            l_new.append(alpha * l[h] + _col_sum(p))
            probs.append(p.astype(BF16))
        acc_new = []
        for h in range(N_HEADS):
            vt = vt_ref[h * HEAD_DIM:(h + 1) * HEAD_DIM, pl.ds(start, K_TILE)]
            pv = jnp.dot(vt, probs[h], preferred_element_type=F32)
            acc_new.append(alphas[h] * acc[h] + pv)
        return tuple(m_new), tuple(l_new), tuple(acc_new)

    init = (tuple(jnp.full((1, qb), NEG_BIG, F32) for _ in range(N_HEADS)),
            tuple(jnp.zeros((1, qb), F32) for _ in range(N_HEADS)),
            tuple(jnp.zeros((HEAD_DIM, qb), F32) for _ in range(N_HEADS)))
    _, l, acc = lax.fori_loop(0, n_tiles, att_tile, init)
    out_t = jnp.concatenate([acc[h] / l[h] for h in range(N_HEADS)], axis=0)
    o_ref[...] = out_t.T.astype(o_ref.dtype)


def _dsa(iq, iwt, qa, ik, ka, vat, batch, seq):
    n = qa.shape[0]
    nq = seq // Q_BLOCK
    topk = min(TOPK_MAX, seq // 4)
    qrow = lambda w: pl.BlockSpec((Q_BLOCK, w), lambda b, i: (b * nq + i, 0))
    whole = lambda w: pl.BlockSpec((seq, w), lambda b, i: (b, 0))
    return pl.pallas_call(
        functools.partial(_dsa_kernel, topk=topk),
        grid=(batch, nq),
        in_specs=[qrow(LANES), pl.BlockSpec((SUBLANES, Q_BLOCK), lambda b, i: (0, b * nq + i)),
                  qrow(BRANCH_WIDTH), whole(LANES), whole(BRANCH_WIDTH),
                  pl.BlockSpec((BRANCH_WIDTH, seq), lambda b, i: (0, b))],
        out_specs=qrow(BRANCH_WIDTH),
        out_shape=jax.ShapeDtypeStruct((n, BRANCH_WIDTH), BF16),
        scratch_shapes=[pltpu.VMEM((seq, Q_BLOCK), F32)],
        compiler_params=_params("arbitrary", "arbitrary"),
        name="dsa",
    )(iq, iwt, qa, ik, ka, vat)


def _stick_kernel(q_ref, k_ref, vt_ref, o_ref):
    qb = q_ref.shape[0]
    i = pl.program_id(1)
    q_heads = _masked_heads(q_ref[...], HEAD_DIM)
    after = (lax.broadcasted_iota(jnp.int32, (K_TILE, 2 * K_TILE), 1) % K_TILE
             > lax.broadcasted_iota(jnp.int32, (K_TILE, 2 * K_TILE), 0))
    after = jnp.where(after, 1.0, 0.0).astype(BF16)

    def tile(j, carry, strict):
        tails, acc = carry
        start = pl.multiple_of(j * K_TILE, K_TILE)
        kt = k_ref[pl.ds(start, K_TILE), :]
        zs = [lax.dot_general(kt, q_heads[h], _CONTRACT_LAST, preferred_element_type=F32)
              for h in range(N_HEADS)]
        log_betas, splits, new_tails = [], [], []
        for h in range(N_HEADS):
            z = zs[h]
            log_beta = jnp.minimum(z, 0.0) - jnp.log(1.0 + jnp.exp(-jnp.abs(z)))
            log_keep = log_beta - z
            if strict is not None:
                log_keep = jnp.where(strict, log_keep, 0.0)
            hi = log_keep.astype(BF16)
            lo = (log_keep - hi.astype(F32)).astype(BF16)
            log_betas.append(log_beta)
            splits.append(jnp.concatenate([hi, lo], axis=0))
            new_tails.append(tails[h] + _col_sum(log_keep))
        laters = [jnp.dot(after, splits[h], preferred_element_type=F32) for h in range(N_HEADS)]
        weights = []
        for h in range(N_HEADS):
            a = jnp.exp(log_betas[h] + laters[h] + tails[h])
            if strict is not None:
                a = jnp.where(strict, a, 0.0)
            weights.append(a.astype(BF16))
        new_acc = []
        for h in range(N_HEADS):
            vt = vt_ref[h * HEAD_DIM:(h + 1) * HEAD_DIM, pl.ds(start, K_TILE)]
            new_acc.append(acc[h] + jnp.dot(vt, weights[h], preferred_element_type=F32))
        return tuple(new_tails), tuple(new_acc)

    key_pos = lax.broadcasted_iota(jnp.int32, (K_TILE, qb), 0)
    qry_pos = lax.broadcasted_iota(jnp.int32, (K_TILE, qb), 1)
    init = (tuple(jnp.zeros((1, qb), F32) for _ in range(N_HEADS)),
            tuple(jnp.zeros((HEAD_DIM, qb), F32) for _ in range(N_HEADS)))
    carry = tile(i, init, key_pos < qry_pos)
    _, acc = lax.fori_loop(0, i, lambda t, c: tile(i - 1 - t, c, None), carry)
    o_ref[...] = jnp.concatenate(acc, axis=0).T.astype(o_ref.dtype)


def _stick_breaking(qc, kc, vct, batch, seq):
    n = qc.shape[0]
    nq = seq // Q_BLOCK
    qrow = pl.BlockSpec((Q_BLOCK, BRANCH_WIDTH), lambda b, i: (b * nq + i, 0))
    return pl.pallas_call(
        _stick_kernel,
        grid=(batch, nq),
        in_specs=[qrow, pl.BlockSpec((seq, BRANCH_WIDTH), lambda b, i: (b, 0)),
                  pl.BlockSpec((BRANCH_WIDTH, seq), lambda b, i: (0, b))],
        out_specs=qrow,
        out_shape=jax.ShapeDtypeStruct((n, BRANCH_WIDTH), BF16),
        compiler_params=_params("arbitrary", "arbitrary"),
        name="stick_breaking",
    )(qc, kc, vct)


def _merge_kernel(x_ref, g1_ref, ya_ref, yb_ref, yc_ref, yd_ref, wg_ref, wb_ref, wo_ref,
                  g2_ref, wr_ref, br_ref, x1_o, comb_o):
    x = x_ref[...]
    d = x.shape[1]
    h = _rms_bf16(x, g1_ref[...])
    ys = (ya_ref[...], yb_ref[...], yc_ref[...], yd_ref[...])
    col = 256
    pieces = []
    for c in range(d // col):
        m = None
        for n in range(N_BRANCH):
            logit = jnp.dot(h, wg_ref[:, n * d + c * col:n * d + (c + 1) * col],
                            preferred_element_type=F32)
            br = jnp.dot(ys[n], wb_ref[n, :, c * col:(c + 1) * col], preferred_element_type=F32)
            t = jax.nn.sigmoid(logit) * br
            m = t if m is None else m + t
        pieces.append(m.astype(BF16))
    merged = jnp.concatenate(pieces, axis=1)
    x1 = x + jnp.dot(merged, wo_ref[...], preferred_element_type=F32)
    x1_o[...] = x1

    ms = jnp.mean(x1 * x1, axis=-1, keepdims=True)
    h2 = x1 * lax.rsqrt(ms + EPS) * g2_ref[...]
    logits = jnp.dot(h2, wr_ref[...], preferred_element_type=F32,
                     precision=lax.Precision.HIGHEST) + br_ref[...]
    rows = logits.shape[0]
    lane = lax.broadcasted_iota(jnp.int32, (rows, LANES), 1)
    neg = -jnp.inf
    is_grp = (lane >= N_EXPERTS) & (lane < N_EXPERTS + N_EXPERT_GROUPS)
    gl = jnp.where(is_grp, logits, neg)
    gmax = jnp.max(gl, axis=1, keepdims=True)
    gidx = jnp.min(jnp.where(gl == gmax, lane, LANES), axis=1, keepdims=True) - N_EXPERTS
    grp_p = 1.0 / jnp.sum(jnp.where(is_grp, jnp.exp(gl - gmax), 0.0), axis=1, keepdims=True)
    in_grp = (lane >= gidx * EXPERTS_PER_GROUP) & (lane < (gidx + 1) * EXPERTS_PER_GROUP)
    el = jnp.where(in_grp, logits, neg)
    l1 = jnp.max(el, axis=1, keepdims=True)
    j1 = jnp.min(jnp.where(el == l1, lane, LANES), axis=1, keepdims=True)
    el2 = jnp.where(lane == j1, neg, el)
    l2 = jnp.max(el2, axis=1, keepdims=True)
    j2 = jnp.min(jnp.where(el2 == l2, lane, LANES), axis=1, keepdims=True)
    e2 = jnp.exp(l2 - l1)
    w1 = grp_p / (1.0 + e2)
    w2 = grp_p * e2 / (1.0 + e2)
    comb_o[...] = jnp.where(lane == j1, w1, 0.0) + jnp.where(lane == j2, w2, 0.0)


def _merge(x2, norm1_g, ys, w_gate, w_branch, w_out, norm2_g, w_router, b_router):
    n, d = x2.shape
    rows = MERGE_ROWS
    row = lambda w: pl.BlockSpec((rows, w), lambda i: (i, 0))
    full = lambda a: pl.BlockSpec(a.shape, lambda i: (0,) * a.ndim)
    args = (x2, norm1_g.reshape(1, d), *ys, w_gate, w_branch, w_out, norm2_g.reshape(1, d),
            w_router, b_router)
    in_specs = [row(d), full(args[1])] + [row(BRANCH_WIDTH)] * 4 + [full(a) for a in args[6:]]
    return pl.pallas_call(
        _merge_kernel,
        grid=(n // rows,),
        in_specs=in_specs,
        out_specs=[row(d), row(LANES)],
        out_shape=[jax.ShapeDtypeStruct((n, d), F32), jax.ShapeDtypeStruct((n, LANES), F32)],
        compiler_params=_params("arbitrary"),
        name="merge_router",
    )(*args)


def _moe_kernel(x_ref, g2_ref, comb_ref, wg_ref, wu_ref, wd_ref, gf_ref, o_ref, h_scr, acc_scr,
                *, final_norm):
    e = pl.program_id(1)

    @pl.when(e == 0)
    def _():
        h_scr[...] = _rms_bf16(x_ref[...], g2_ref[...])
        acc_scr[...] = jnp.zeros(acc_scr.shape, F32)

    h = h_scr[...]
    a = jnp.dot(h, wg_ref[0], preferred_element_type=F32)
    u = jnp.dot(h, wu_ref[0], preferred_element_type=F32)
    comb = comb_ref[...]
    lane = lax.broadcasted_iota(jnp.int32, comb.shape, 1)
    w = jnp.sum(jnp.where(lane == e, comb, 0.0), axis=1, keepdims=True)
    hh = (jax.nn.silu(a) * u * w).astype(BF16)
    acc_scr[...] += jnp.dot(hh, wd_ref[0], preferred_element_type=F32)

    @pl.when(e == pl.num_programs(1) - 1)
    def _():
        y = x_ref[...] + acc_scr[...]
        if final_norm:
            ms = jnp.mean(y * y, axis=-1, keepdims=True)
            y = y * lax.rsqrt(ms + EPS) * gf_ref[...]
        o_ref[...] = y


def _moe(x1, norm2_g, comb, w_g, w_u, w_d, norm_f_g, final_norm):
    n, d = x1.shape
    rows = min(MOE_ROWS, n)
    n_exp = w_g.shape[0]
    row = lambda w: pl.BlockSpec((rows, w), lambda i, e: (i, 0))
    vec = pl.BlockSpec((1, d), lambda i, e: (0, 0))
    return pl.pallas_call(
        functools.partial(_moe_kernel, final_norm=final_norm),
        grid=(n // rows, n_exp),
        in_specs=[row(d), vec, row(LANES),
                  pl.BlockSpec((1, d, D_EXPERT), lambda i, e: (e, 0, 0)),
                  pl.BlockSpec((1, d, D_EXPERT), lambda i, e: (e, 0, 0)),
                  pl.BlockSpec((1, D_EXPERT, d), lambda i, e: (e, 0, 0)),
                  vec],
        out_specs=row(d),
        out_shape=jax.ShapeDtypeStruct((n, d), F32),
        scratch_shapes=[pltpu.VMEM((rows, d), BF16), pltpu.VMEM((rows, d), F32)],
        compiler_params=_params("arbitrary", "arbitrary"),
        name="experts",
    )(x1, norm2_g.reshape(1, d), comb, w_g, w_u, w_d, norm_f_g.reshape(1, d))


def kernel(x, positions, norm1_g, w_in, sgu_ln_g, sgu_ln_b, w_spatial, b_spatial, conv_w, w_branch,
           w_out, norm2_g, w_router_group, b_router_group, w_router_expert, b_router_expert,
           w_exp_gate, w_exp_up, w_exp_down, norm_f_g):
    batch, seq, d = x.shape
    depth = w_in.shape[0]
    n = batch * seq
    assert seq % COUNT_CHUNK == 0 and seq % Q_BLOCK == 0 and n % PROJ_ROWS == 0
    x2 = x.reshape(n, d)
    tables = _rope_tables(positions)
    for l in range(depth):
        w_a, w_t, gate_off = _proj_weight(w_in[l])
        w_gate = w_in[l][:, gate_off:].astype(BF16)
        pad = jnp.zeros((d, LANES - N_EXPERTS - N_EXPERT_GROUPS), F32)
        w_router = jnp.concatenate([w_router_expert[l], w_router_group[l], pad], axis=1)
        b_router = jnp.concatenate([b_router_expert[l], b_router_group[l], pad[0]]).reshape(1, LANES)
        qa, ka, vat, iq, ik, iwt, y_b, qc, kc, vct, y_d = _project(
            x2, norm1_g[l], w_a, w_t, tables, sgu_ln_g[l], sgu_ln_b[l], w_spatial[l], b_spatial[l],
            conv_w[l], seq)
        y_a = _dsa(iq, iwt, qa, ik, ka, vat, batch, seq)
        y_c = _stick_breaking(qc, kc, vct, batch, seq)
        x1, comb = _merge(x2, norm1_g[l], (y_a, y_b, y_c, y_d), w_gate, w_branch[l].astype(BF16),
                          w_out[l].astype(BF16), norm2_g[l], w_router, b_router)
        x2 = _moe(x1, norm2_g[l], comb, w_exp_gate[l].astype(BF16), w_exp_up[l].astype(BF16),
                  w_exp_down[l].astype(BF16), norm_f_g, final_norm=(l == depth - 1))
    return x2.reshape(batch, seq, d)
```

```python
import functools

import jax
import jax.numpy as jnp
from jax import lax
from jax.experimental import pallas as pl
from jax.experimental.pallas import tpu as pltpu

F32 = jnp.float32
BF16 = jnp.bfloat16

EPS = 1e-6
ROPE_THETA = 10000.0
HEAD_DIM = 64
BRANCH_WIDTH = 256
N_HEADS = 4
IDX_DIM = 32
TOPK_MAX = 256
CHUNK = 128
N_GROUPS_B = 4
CONV_WIDTH = 3
N_BRANCH = 4
N_EXPERT_GROUPS = 4
EXPERTS_PER_GROUP = 8
N_EXPERTS = N_EXPERT_GROUPS * EXPERTS_PER_GROUP
D_EXPERT = 256

LANES = 128
SUBLANES = 8
Q_BLOCK = 256
K_TILE = Q_BLOCK
TILES_PER_STEP = 2
COUNT_CHUNK = 512
COUNT_ROWS = 64
PROJ_ROWS = 256
MERGE_ROWS = 512
MOE_ROWS = 1024
CONV_HALO = 8
VMEM_LIMIT = 56 * 1024 * 1024
PHANTOM_LOGIT = -1e4
NEG_BIG = -1e30
INT_MIN = -2 ** 31
HALF_RANGE = 2 ** 15

_OFF = {}
_o = 0
for _name, _w in (("q", 512), ("k", 512), ("iq", 256), ("ik", 256),
                  ("ub", 256), ("vb", 256), ("qc", 256), ("kc", 256),
                  ("bd", 256), ("cd", 256), ("xd", 256)):
    _OFF[_name] = (_o, _w)
    _o += _w
W_A_COLS = _o
T_VA, T_VC, T_IW = 0, BRANCH_WIDTH, 2 * BRANCH_WIDTH
W_T_ROWS = 2 * BRANCH_WIDTH + 16

_CONTRACT_LAST = (((1,), (1,)), ((), ()))


def _params(*sem):
    return pltpu.CompilerParams(dimension_semantics=sem, vmem_limit_bytes=VMEM_LIMIT)


def _rope_table_kernel(pos_ref, inv_a_ref, inv_i_ref, cos_a, sin_a, cos_i, sin_i):
    p = pos_ref[...]
    a = p * inv_a_ref[...]
    cos_a[...] = jnp.cos(a)
    sin_a[...] = jnp.sin(a)
    b = p * inv_i_ref[...]
    cos_i[...] = jnp.cos(b)
    sin_i[...] = jnp.sin(b)


def _rope_tables(positions):
    n = positions.size
    rows = PROJ_ROWS
    pos = positions.astype(F32).reshape(n, 1)
    inv_a = ROPE_THETA ** (-jnp.arange(0, HEAD_DIM, 2, dtype=F32) / HEAD_DIM)
    inv_i = ROPE_THETA ** (-jnp.arange(0, IDX_DIM, 2, dtype=F32) / IDX_DIM)
    inv_a = jnp.tile(inv_a, LANES // inv_a.size).reshape(1, LANES)
    inv_i = jnp.tile(inv_i, LANES // inv_i.size).reshape(1, LANES)
    tab = jax.ShapeDtypeStruct((n, LANES), F32)
    row_spec = pl.BlockSpec((rows, LANES), lambda i: (i, 0))
    const = pl.BlockSpec((1, LANES), lambda i: (0, 0))
    return pl.pallas_call(
        _rope_table_kernel,
        grid=(n // rows,),
        in_specs=[pl.BlockSpec((rows, 1), lambda i: (i, 0)), const, const],
        out_specs=[row_spec] * 4,
        out_shape=[tab] * 4,
        compiler_params=_params("arbitrary"),
        name="rope_tables",
    )(pos, inv_a, inv_i)


def _rms_bf16(x, g):
    ms = jnp.mean(x * x, axis=-1, keepdims=True)
    return (x * lax.rsqrt(ms + EPS) * g).astype(BF16)


def _proj_kernel(x_ref, g_ref, w_ref, wt_ref, cos_a, sin_a, cos_i, sin_i, lng_ref, lnb_ref, ws_ref,
                 bs_ref, cw_ref,
                 qa_o, ka_o, vat_o, iq_o, ik_o, iwt_o, yb_o, qc_o, kc_o, vct_o, yd_o,
                 zbuf, *, steps_per_seq):
    rows = x_ref.shape[0]
    h = _rms_bf16(x_ref[...], g_ref[...])

    def mm(name):
        off, width = _OFF[name]
        return jnp.dot(h, w_ref[:, off:off + width], preferred_element_type=F32)

    def rope(name, cos_ref, sin_ref, out_ref, scale):
        p = mm(name)
        w = p.shape[1] // 2
        c = cos_ref[...]
        s = sin_ref[...]
        for j in range(w // LANES):
            lo = j * LANES
            r = p[:, lo:lo + LANES] * c + p[:, w + lo:w + lo + LANES] * s
            out_ref[:, lo:lo + LANES] = (r if scale == 1.0 else r * scale).astype(out_ref.dtype)

    att_scale = HEAD_DIM ** -0.5
    rope("q", cos_a, sin_a, qa_o, att_scale)
    rope("k", cos_a, sin_a, ka_o, 1.0)
    rope("iq", cos_i, sin_i, iq_o, 1.0)
    rope("ik", cos_i, sin_i, ik_o, 1.0)
    qc_o[...] = (mm("qc") * att_scale).astype(BF16)
    kc_o[...] = mm("kc").astype(BF16)
    t = lax.dot_general(wt_ref[...], h, _CONTRACT_LAST, preferred_element_type=F32)
    vat_o[...] = t[T_VA:T_VA + BRANCH_WIDTH, :].astype(BF16)
    vct_o[...] = t[T_VC:T_VC + BRANCH_WIDTH, :].astype(BF16)
    iwt_o[...] = t[T_IW:T_IW + SUBLANES, :]

    u = jax.nn.gelu(mm("ub"))
    v = jax.nn.gelu(mm("vb"))
    mu = jnp.mean(v, axis=-1, keepdims=True)
    var = jnp.mean(jnp.square(v - mu), axis=-1, keepdims=True)
    vn = ((v - mu) * lax.rsqrt(var + EPS) * lng_ref[...] + lnb_ref[...]).astype(BF16)
    t_idx = lax.broadcasted_iota(jnp.int32, (CHUNK, CHUNK), 0)
    s_idx = lax.broadcasted_iota(jnp.int32, (CHUNK, CHUNK), 1)
    causal = s_idx <= t_idx
    group_masks = [_head_lane_mask(BRANCH_WIDTH, BRANCH_WIDTH // N_GROUPS_B, g) for g in range(N_GROUPS_B)]
    w_tril = [jnp.where(causal, ws_ref[g], 0.0).astype(BF16) for g in range(N_GROUPS_B)]
    for c in range(rows // CHUNK):
        vc = vn[c * CHUNK:(c + 1) * CHUNK, :]
        mixed = bs_ref[...]
        for g in range(N_GROUPS_B):
            full = jnp.dot(w_tril[g], vc, preferred_element_type=F32)
            mixed = mixed + jnp.where(group_masks[g], full, 0.0)
        yb_o[c * CHUNK:(c + 1) * CHUNK, :] = (u[c * CHUNK:(c + 1) * CHUNK, :] * mixed).astype(BF16)

    bd = mm("bd")
    z = mm("cd") * mm("xd")

    @pl.when(pl.program_id(0) % steps_per_seq == 0)
    def _():
        zbuf[0:CONV_HALO, :] = jnp.zeros((CONV_HALO, BRANCH_WIDTH), F32)

    zbuf[CONV_HALO:CONV_HALO + rows, :] = z
    y = cw_ref[CONV_WIDTH - 1:CONV_WIDTH, :] * z
    for tap in range(CONV_WIDTH - 1):
        shift = CONV_WIDTH - 1 - tap
        y = y + cw_ref[tap:tap + 1, :] * zbuf[CONV_HALO - shift:CONV_HALO - shift + rows, :]
    yd_o[...] = (bd * y).astype(BF16)
    zbuf[0:CONV_HALO, :] = z[rows - CONV_HALO:rows, :]


def _swap_halves(w, dim):
    d_in, cols = w.shape
    wh = w.reshape(d_in, cols // dim, 2, dim // 2)
    return jnp.stack([-wh[:, :, 1], wh[:, :, 0]], axis=2).reshape(d_in, cols)


def _proj_weight(w_in):
    d = w_in.shape[0]
    sizes = (BRANCH_WIDTH,) * 3 + (N_HEADS * IDX_DIM, IDX_DIM, N_HEADS) + (BRANCH_WIDTH,) * 8
    parts, o = [], 0
    for s in sizes:
        parts.append(w_in[:, o:o + s])
        o += s
    qa, ka, va, iq, ik, iw, ub, vb, qc, kc, vc, bd, cd, xd = parts
    ik_rep = jnp.tile(ik, (1, N_HEADS))
    cols = [qa, _swap_halves(qa, HEAD_DIM), ka, _swap_halves(ka, HEAD_DIM),
            iq, _swap_halves(iq, IDX_DIM), ik_rep, _swap_halves(ik_rep, IDX_DIM),
            ub, vb, qc, kc, bd, cd, xd]
    iw_pad = jnp.concatenate([iw, jnp.zeros((d, W_T_ROWS - 2 * BRANCH_WIDTH - N_HEADS), w_in.dtype)], axis=1)
    w_t = jnp.concatenate([va, vc, iw_pad], axis=1).T
    return jnp.concatenate(cols, axis=1).astype(BF16), w_t.astype(BF16), o


def _project(x2, norm_g, w_a, w_t, tables, ln_g, ln_b, w_spatial, b_spatial, conv_w, seq):
    n, d = x2.shape
    rows = PROJ_ROWS
    cos_a, sin_a, cos_i, sin_i = tables
    bias = jnp.repeat(b_spatial.T, BRANCH_WIDTH // N_GROUPS_B, axis=1)
    row = lambda w: pl.BlockSpec((rows, w), lambda i: (i, 0))
    col = lambda r: pl.BlockSpec((r, rows), lambda i: (0, i))
    full = lambda a: pl.BlockSpec(a.shape, lambda i: (0,) * a.ndim)
    bf = lambda w: jax.ShapeDtypeStruct((n, w), BF16)
    bft = jax.ShapeDtypeStruct((BRANCH_WIDTH, n), BF16)
    args = (x2, norm_g.reshape(1, d), w_a, w_t, cos_a, sin_a, cos_i, sin_i, ln_g.reshape(1, -1),
            ln_b.reshape(1, -1), w_spatial, bias, conv_w)
    in_specs = [row(d), full(args[1]), full(w_a), full(w_t), row(LANES), row(LANES), row(LANES),
                row(LANES), full(args[8]), full(args[9]), full(w_spatial), full(bias), full(conv_w)]
    out_shape = [bf(256), bf(256), bft, bf(128), bf(128), jax.ShapeDtypeStruct((SUBLANES, n), F32),
                 bf(256), bf(256), bf(256), bft, bf(256)]
    out_specs = [row(256), row(256), col(BRANCH_WIDTH), row(128), row(128), col(SUBLANES),
                 row(256), row(256), row(256), col(BRANCH_WIDTH), row(256)]
    return pl.pallas_call(
        functools.partial(_proj_kernel, steps_per_seq=seq // rows),
        grid=(n // rows,),
        in_specs=in_specs,
        out_specs=out_specs,
        out_shape=out_shape,
        scratch_shapes=[pltpu.VMEM((CONV_HALO + rows, BRANCH_WIDTH), F32)],
        compiler_params=_params("arbitrary"),
        name="project",
    )(*args)


def _head_lane_mask(width, per_head, h, rows=1):
    lane = lax.broadcasted_iota(jnp.int32, (rows, width), 1)
    return (lane >= h * per_head) & (lane < (h + 1) * per_head)


def _masked_heads(x, per_head):
    return [jnp.where(_head_lane_mask(x.shape[1], per_head, h), x, jnp.zeros_like(x))
            for h in range(N_HEADS)]


def _float_to_key(x):
    bits = lax.bitcast_convert_type(x, jnp.int32)
    sign = bits >> 31
    return (bits ^ (sign & jnp.int32(0x7FFFFFFF))) - sign


def _key_to_float(key):
    bits = jnp.where(key < 0, (-key) | jnp.int32(INT_MIN), key)
    return lax.bitcast_convert_type(bits, F32)


def _col_reduce(x, op, final):
    while x.shape[0] > SUBLANES:
        half = x.shape[0] // 2
        x = op(x[:half], x[half:])
    return final(x, axis=0, keepdims=True)


def _col_sum(x):
    return _col_reduce(x, jnp.add, jnp.sum)


def _col_max(x):
    return _col_reduce(x, jnp.maximum, jnp.max)


def _step_starts(step):
    return [pl.multiple_of((step * TILES_PER_STEP + g) * K_TILE, K_TILE) for g in range(TILES_PER_STEP)]


def _dsa_kernel(iq_ref, iwt_ref, q_ref, ik_ref, k_ref, vt_ref, o_ref, sc_ref, hi_ref, lo_ref, *, topk):
    qb = q_ref.shape[0]
    i = pl.program_id(1)
    n_tiles = i + 1
    n_steps = (n_tiles + TILES_PER_STEP - 1) // TILES_PER_STEP

    @pl.when(i == 0)
    def _():
        sc_ref[...] = jnp.full(sc_ref.shape, -jnp.inf, F32)
        hi_ref[...] = jnp.full(hi_ref.shape, -HALF_RANGE, jnp.int16)
        lo_ref[...] = jnp.full(lo_ref.shape, -HALF_RANGE, jnp.int16)

    iq_heads = _masked_heads(iq_ref[...], IDX_DIM)
    iwt = iwt_ref[...]
    iw_heads = [iwt[h:h + 1, :] for h in range(N_HEADS)]

    def put_scores(start, sc):
        sc_ref[pl.ds(start, K_TILE), :] = sc
        key = _float_to_key(sc)
        hi_ref[pl.ds(start, K_TILE), :] = (key >> 16).astype(jnp.int16)
        lo_ref[pl.ds(start, K_TILE), :] = ((key & 0xFFFF) - HALF_RANGE).astype(jnp.int16)

    key_off = lax.broadcasted_iota(jnp.int32, (K_TILE, qb), 0)
    qry_pos = i * qb + lax.broadcasted_iota(jnp.int32, (K_TILE, qb), 1)

    def score_step(jj, masked):
        starts = _step_starts(jj)
        dots = [[lax.dot_general(ik_ref[pl.ds(st, K_TILE), :], iq_heads[h], _CONTRACT_LAST,
                                 preferred_element_type=F32) for h in range(N_HEADS)] for st in starts]
        for g in range(TILES_PER_STEP):
            sc = None
            for h in range(N_HEADS):
                t = iw_heads[h] * jnp.maximum(dots[g][h], 0.0)
                sc = t if sc is None else sc + t
            if masked:
                sc = jnp.where(starts[g] + key_off <= qry_pos, sc, -jnp.inf)
            put_scores(starts[g], sc)

    def full_step(jj, carry):
        score_step(jj, False)
        return carry

    lax.fori_loop(0, i // TILES_PER_STEP, full_step, 0)
    score_step(i // TILES_PER_STEP, True)

    n_chunks = (n_tiles * K_TILE + COUNT_CHUNK - 1) // COUNT_CHUNK
    kf = jnp.float32(topk)

    def chunk_rows(c):
        base = pl.multiple_of(c * COUNT_CHUNK, COUNT_CHUNK)
        return [pl.ds(base + r * COUNT_ROWS, COUNT_ROWS) for r in range(COUNT_CHUNK // COUNT_ROWS)]

    def count(ref, pred, dtype):
        one, zero = jnp.ones((), dtype), jnp.zeros((), dtype)

        def body(c, acc):
            for rows in chunk_rows(c):
                acc = acc + jnp.where(pred(ref[rows, :]), one, zero)
            return acc
        acc = lax.fori_loop(0, n_chunks, body, jnp.zeros((COUNT_ROWS, qb), dtype))
        return jnp.sum(acc.astype(F32), axis=0, keepdims=True)

    def bisect16(ref, target):
        def bit_step(b, carry):
            best, cnt_best = carry
            cand = best + lax.shift_left(jnp.int32(1), 15 - b)
            cand16 = cand.astype(jnp.int16)
            cnt = count(ref, lambda blk: blk >= cand16, jnp.int16)
            ok = cnt >= target
            return jnp.where(ok, cand, best), jnp.where(ok, cnt, cnt_best)
        init = (jnp.full((1, qb), -HALF_RANGE, jnp.int32), jnp.zeros((1, qb), F32))
        return lax.fori_loop(0, 16, bit_step, init)

    hi_t, cnt_hi = bisect16(hi_ref, kf)
    hi_t16 = hi_t.astype(jnp.int16)
    above = count(hi_ref, lambda blk: blk > hi_t16, jnp.int16)

    def mask_low(c, carry):
        for rows in chunk_rows(c):
            lo_ref[rows, :] = jnp.where(hi_ref[rows, :] == hi_t16, lo_ref[rows, :], jnp.int16(-HALF_RANGE))
        return carry

    lax.fori_loop(0, n_chunks, mask_low, 0)
    lo_t, cnt_lo = bisect16(lo_ref, kf - above)
    cnt_key = above + jnp.where(lo_t == -HALF_RANGE, cnt_hi - above, cnt_lo)
    key = hi_t * (2 * HALF_RANGE) + (lo_t + HALF_RANGE)
    few = i * qb + lax.broadcasted_iota(jnp.int32, (1, qb), 1) + 1 < topk
    thr = jnp.where(few, jnp.finfo(F32).min, _key_to_float(key))

    @pl.when(jnp.max(jnp.where(few, kf, cnt_key)) > kf)
    def _():
        need = kf - count(sc_ref, lambda blk: blk > thr, F32)
        before = (lax.broadcasted_iota(jnp.int32, (K_TILE, K_TILE), 1)
                  < lax.broadcasted_iota(jnp.int32, (K_TILE, K_TILE), 0))
        before = jnp.where(before, 1.0, 0.0).astype(BF16)

        def tie_body(jj, seen):
            starts = _step_starts(jj)
            blks = [sc_ref[pl.ds(st, K_TILE), :] for st in starts]
            eqs = [blk == thr for blk in blks]
            eq_fs = [jnp.where(eq, 1.0, 0.0) for eq in eqs]
            ranks = [jnp.dot(before, eq_f.astype(BF16), preferred_element_type=F32) for eq_f in eq_fs]
            for g in range(TILES_PER_STEP):
                drop = eqs[g] & (seen + ranks[g] >= need)
                sc_ref[pl.ds(starts[g], K_TILE), :] = jnp.where(drop, -jnp.inf, blks[g])
                seen = seen + _col_sum(eq_fs[g])
            return seen

        lax.fori_loop(0, n_steps, tie_body, jnp.zeros((1, qb), F32))

    q_heads = _masked_heads(q_ref[...], HEAD_DIM)

    def att_step(jj, carry):
        m, l, acc = carry
        starts = _step_starts(jj)
        sels = [sc_ref[pl.ds(st, K_TILE), :] >= thr for st in starts]
        logits = [[lax.dot_general(k_ref[pl.ds(st, K_TILE), :], q_heads[h], _CONTRACT_LAST,
                                   preferred_element_type=F32) for h in range(N_HEADS)] for st in starts]
        m_new, l_new, alphas, probs = [], [], [], []
        for h in range(N_HEADS):
            ss = [jnp.where(sels[g], logits[g][h], NEG_BIG) for g in range(TILES_PER_STEP)]
            mh = m[h]
            for s in ss:
                mh = jnp.maximum(mh, _col_max(s))
            ps = [jnp.exp(s - mh) for s in ss]
            alpha = jnp.exp(m[h] - mh)
            lh = alpha * l[h]
            for p in ps:
                lh = lh + _col_sum(p)
            m_new.append(mh)
            alphas.append(alpha)
            l_new.append(lh)
            probs.append([p.astype(BF16) for p in ps])
        acc_new = []
        for h in range(N_HEADS):
            pv = None
            for g, st in enumerate(starts):
                vt = vt_ref[h * HEAD_DIM:(h + 1) * HEAD_DIM, pl.ds(st, K_TILE)]
                t = jnp.dot(vt, probs[h][g], preferred_element_type=F32)
                pv = t if pv is None else pv + t
            acc_new.append(alphas[h] * acc[h] + pv)
        return tuple(m_new), tuple(l_new), tuple(acc_new)

    init = (tuple(jnp.full((1, qb), NEG_BIG, F32) for _ in range(N_HEADS)),
            tuple(jnp.zeros((1, qb), F32) for _ in range(N_HEADS)),
            tuple(jnp.zeros((HEAD_DIM, qb), F32) for _ in range(N_HEADS)))
    _, l, acc = lax.fori_loop(0, n_steps, att_step, init)
    out_t = jnp.concatenate([acc[h] / l[h] for h in range(N_HEADS)], axis=0)
    o_ref[...] = out_t.T.astype(o_ref.dtype)


def _dsa(iq, iwt, qa, ik, ka, vat, batch, seq):
    n = qa.shape[0]
    nq = seq // Q_BLOCK
    topk = min(TOPK_MAX, seq // 4)
    qrow = lambda w: pl.BlockSpec((Q_BLOCK, w), lambda b, i: (b * nq + i, 0))
    whole = lambda w: pl.BlockSpec((seq, w), lambda b, i: (b, 0))
    return pl.pallas_call(
        functools.partial(_dsa_kernel, topk=topk),
        grid=(batch, nq),
        in_specs=[qrow(LANES), pl.BlockSpec((SUBLANES, Q_BLOCK), lambda b, i: (0, b * nq + i)),
                  qrow(BRANCH_WIDTH), whole(LANES), whole(BRANCH_WIDTH),
                  pl.BlockSpec((BRANCH_WIDTH, seq), lambda b, i: (0, b))],
        out_specs=qrow(BRANCH_WIDTH),
        out_shape=jax.ShapeDtypeStruct((n, BRANCH_WIDTH), BF16),
        scratch_shapes=[pltpu.VMEM((seq, Q_BLOCK), F32), pltpu.VMEM((seq, Q_BLOCK), jnp.int16),
                        pltpu.VMEM((seq, Q_BLOCK), jnp.int16)],
        compiler_params=_params("arbitrary", "arbitrary"),
        name="dsa",
    )(iq, iwt, qa, ik, ka, vat)


def _stick_kernel(q_ref, k_ref, vt_ref, o_ref):
    qb = q_ref.shape[0]
    i = pl.program_id(1)
    q_heads = _masked_heads(q_ref[...], HEAD_DIM)
    after = (lax.broadcasted_iota(jnp.int32, (K_TILE, 2 * K_TILE), 1) % K_TILE
             > lax.broadcasted_iota(jnp.int32, (K_TILE, 2 * K_TILE), 0))
    after = jnp.where(after, 1.0, 0.0).astype(BF16)
    key_off = lax.broadcasted_iota(jnp.int32, (K_TILE, qb), 0)
    qry_pos = i * qb + lax.broadcasted_iota(jnp.int32, (K_TILE, qb), 1)

    def step(u, carry, masked):
        tails, acc = carry
        first = i - 2 * u
        tiles = [first, jnp.maximum(first - 1, 0)]
        bias = [None, jnp.where(first >= 1, 0.0, PHANTOM_LOGIT)]
        starts = [pl.multiple_of(t * K_TILE, K_TILE) for t in tiles]
        stricts = [starts[g] + key_off < qry_pos if masked else None for g in range(2)]
        zs = [[lax.dot_general(k_ref[pl.ds(st, K_TILE), :], q_heads[h], _CONTRACT_LAST,
                               preferred_element_type=F32) for h in range(N_HEADS)] for st in starts]
        log_betas, splits, befores = [], [], []
        tails = list(tails)
        for g in range(2):
            lbs, sps, bfs = [], [], []
            for h in range(N_HEADS):
                z = zs[g][h] if bias[g] is None else zs[g][h] + bias[g]
                log_beta = jnp.minimum(z, 0.0) - jnp.log(1.0 + jnp.exp(-jnp.abs(z)))
                log_keep = log_beta - z
                if masked:
                    log_keep = jnp.where(stricts[g], log_keep, 0.0)
                hi = log_keep.astype(BF16)
                lo = (log_keep - hi.astype(F32)).astype(BF16)
                lbs.append(log_beta)
                sps.append(jnp.concatenate([hi, lo], axis=0))
                bfs.append(tails[h])
                tails[h] = tails[h] + _col_sum(log_keep)
            log_betas.append(lbs)
            splits.append(sps)
            befores.append(bfs)
        laters = [[jnp.dot(after, splits[g][h], preferred_element_type=F32) for h in range(N_HEADS)]
                  for g in range(2)]
        new_acc = []
        for h in range(N_HEADS):
            total = acc[h]
            for g in range(2):
                a = jnp.exp(log_betas[g][h] + laters[g][h] + befores[g][h])
                if masked:
                    a = jnp.where(stricts[g], a, 0.0)
                vt = vt_ref[h * HEAD_DIM:(h + 1) * HEAD_DIM, pl.ds(starts[g], K_TILE)]
                total = total + jnp.dot(vt, a.astype(BF16), preferred_element_type=F32)
            new_acc.append(total)
        return tuple(tails), tuple(new_acc)

    init = (tuple(jnp.zeros((1, qb), F32) for _ in range(N_HEADS)),
            tuple(jnp.zeros((HEAD_DIM, qb), F32) for _ in range(N_HEADS)))
    carry = step(0, init, True)
    _, acc = lax.fori_loop(1, (i + 2) // 2, lambda u, c: step(u, c, False), carry)
    o_ref[...] = jnp.concatenate(acc, axis=0).T.astype(o_ref.dtype)


def _stick_breaking(qc, kc, vct, batch, seq):
    n = qc.shape[0]
    nq = seq // Q_BLOCK
    qrow = pl.BlockSpec((Q_BLOCK, BRANCH_WIDTH), lambda b, i: (b * nq + i, 0))
    return pl.pallas_call(
        _stick_kernel,
        grid=(batch, nq),
        in_specs=[qrow, pl.BlockSpec((seq, BRANCH_WIDTH), lambda b, i: (b, 0)),
                  pl.BlockSpec((BRANCH_WIDTH, seq), lambda b, i: (0, b))],
        out_specs=qrow,
        out_shape=jax.ShapeDtypeStruct((n, BRANCH_WIDTH), BF16),
        compiler_params=_params("arbitrary", "arbitrary"),
        name="stick_breaking",
    )(qc, kc, vct)


def _merge_kernel(x_ref, g1_ref, ya_ref, yb_ref, yc_ref, yd_ref, wg_ref, wb_ref, wo_ref,
                  g2_ref, wr_ref, br_ref, x1_o, comb_o):
    x = x_ref[...]
    d = x.shape[1]
    h = _rms_bf16(x, g1_ref[...])
    ys = (ya_ref[...], yb_ref[...], yc_ref[...], yd_ref[...])
    col = 256
    pieces = []
    for c in range(d // col):
        m = None
        for n in range(N_BRANCH):
            logit = jnp.dot(h, wg_ref[:, n * d + c * col:n * d + (c + 1) * col],
                            preferred_element_type=F32)
            br = jnp.dot(ys[n], wb_ref[n, :, c * col:(c + 1) * col], preferred_element_type=F32)
            t = jax.nn.sigmoid(logit) * br
            m = t if m is None else m + t
        pieces.append(m.astype(BF16))
    merged = jnp.concatenate(pieces, axis=1)
    x1 = x + jnp.dot(merged, wo_ref[...], preferred_element_type=F32)
    x1_o[...] = x1

    ms = jnp.mean(x1 * x1, axis=-1, keepdims=True)
    h2 = x1 * lax.rsqrt(ms + EPS) * g2_ref[...]
    logits = jnp.dot(h2, wr_ref[...], preferred_element_type=F32,
                     precision=lax.Precision.HIGHEST) + br_ref[...]
    rows = logits.shape[0]
    lane = lax.broadcasted_iota(jnp.int32, (rows, LANES), 1)
    neg = -jnp.inf
    is_grp = (lane >= N_EXPERTS) & (lane < N_EXPERTS + N_EXPERT_GROUPS)
    gl = jnp.where(is_grp, logits, neg)
    gmax = jnp.max(gl, axis=1, keepdims=True)
    gidx = jnp.min(jnp.where(gl == gmax, lane, LANES), axis=1, keepdims=True) - N_EXPERTS
    grp_p = 1.0 / jnp.sum(jnp.where(is_grp, jnp.exp(gl - gmax), 0.0), axis=1, keepdims=True)
    in_grp = (lane >= gidx * EXPERTS_PER_GROUP) & (lane < (gidx + 1) * EXPERTS_PER_GROUP)
    el = jnp.where(in_grp, logits, neg)
    l1 = jnp.max(el, axis=1, keepdims=True)
    j1 = jnp.min(jnp.where(el == l1, lane, LANES), axis=1, keepdims=True)
    el2 = jnp.where(lane == j1, neg, el)
    l2 = jnp.max(el2, axis=1, keepdims=True)
    j2 = jnp.min(jnp.where(el2 == l2, lane, LANES), axis=1, keepdims=True)
    e2 = jnp.exp(l2 - l1)
    w1 = grp_p / (1.0 + e2)
    w2 = grp_p * e2 / (1.0 + e2)
    comb_o[...] = jnp.where(lane == j1, w1, 0.0) + jnp.where(lane == j2, w2, 0.0)


def _merge(x2, norm1_g, ys, w_gate, w_branch, w_out, norm2_g, w_router, b_router):
    n, d = x2.shape
    rows = MERGE_ROWS
    row = lambda w: pl.BlockSpec((rows, w), lambda i: (i, 0))
    full = lambda a: pl.BlockSpec(a.shape, lambda i: (0,) * a.ndim)
    args = (x2, norm1_g.reshape(1, d), *ys, w_gate, w_branch, w_out, norm2_g.reshape(1, d),
            w_router, b_router)
    in_specs = [row(d), full(args[1])] + [row(BRANCH_WIDTH)] * 4 + [full(a) for a in args[6:]]
    return pl.pallas_call(
        _merge_kernel,
        grid=(n // rows,),
        in_specs=in_specs,
        out_specs=[row(d), row(LANES)],
        out_shape=[jax.ShapeDtypeStruct((n, d), F32), jax.ShapeDtypeStruct((n, LANES), F32)],
        compiler_params=_params("arbitrary"),
        name="merge_router",
    )(*args)


def _moe_kernel(x_ref, g2_ref, comb_ref, wg_ref, wu_ref, wd_ref, gf_ref, o_ref, h_scr, acc_scr,
                *, final_norm):
    e = pl.program_id(1)

    @pl.when(e == 0)
    def _():
        h_scr[...] = _rms_bf16(x_ref[...], g2_ref[...])
        acc_scr[...] = jnp.zeros(acc_scr.shape, F32)

    h = h_scr[...]
    a = jnp.dot(h, wg_ref[0], preferred_element_type=F32)
    u = jnp.dot(h, wu_ref[0], preferred_element_type=F32)
    comb = comb_ref[...]
    lane = lax.broadcasted_iota(jnp.int32, comb.shape, 1)
    w = jnp.sum(jnp.where(lane == e, comb, 0.0), axis=1, keepdims=True)
    hh = (jax.nn.silu(a) * u * w).astype(BF16)
    acc_scr[...] += jnp.dot(hh, wd_ref[0], preferred_element_type=F32)

    @pl.when(e == pl.num_programs(1) - 1)
    def _():
        y = x_ref[...] + acc_scr[...]
        if final_norm:
            ms = jnp.mean(y * y, axis=-1, keepdims=True)
            y = y * lax.rsqrt(ms + EPS) * gf_ref[...]
        o_ref[...] = y


def _moe(x1, norm2_g, comb, w_g, w_u, w_d, norm_f_g, final_norm):
    n, d = x1.shape
    rows = min(MOE_ROWS, n)
    n_exp = w_g.shape[0]
    row = lambda w: pl.BlockSpec((rows, w), lambda i, e: (i, 0))
    vec = pl.BlockSpec((1, d), lambda i, e: (0, 0))
    return pl.pallas_call(
        functools.partial(_moe_kernel, final_norm=final_norm),
        grid=(n // rows, n_exp),
        in_specs=[row(d), vec, row(LANES),
                  pl.BlockSpec((1, d, D_EXPERT), lambda i, e: (e, 0, 0)),
                  pl.BlockSpec((1, d, D_EXPERT), lambda i, e: (e, 0, 0)),
                  pl.BlockSpec((1, D_EXPERT, d), lambda i, e: (e, 0, 0)),
                  vec],
        out_specs=row(d),
        out_shape=jax.ShapeDtypeStruct((n, d), F32),
        scratch_shapes=[pltpu.VMEM((rows, d), BF16), pltpu.VMEM((rows, d), F32)],
        compiler_params=_params("arbitrary", "arbitrary"),
        name="experts",
    )(x1, norm2_g.reshape(1, d), comb, w_g, w_u, w_d, norm_f_g.reshape(1, d))


def kernel(x, positions, norm1_g, w_in, sgu_ln_g, sgu_ln_b, w_spatial, b_spatial, conv_w, w_branch,
           w_out, norm2_g, w_router_group, b_router_group, w_router_expert, b_router_expert,
           w_exp_gate, w_exp_up, w_exp_down, norm_f_g):
    batch, seq, d = x.shape
    depth = w_in.shape[0]
    n = batch * seq
    assert TILES_PER_STEP == 2
    assert seq % COUNT_CHUNK == 0 and seq % (TILES_PER_STEP * K_TILE) == 0
    assert n % PROJ_ROWS == 0 and n % MERGE_ROWS == 0
    x2 = x.reshape(n, d)
    tables = _rope_tables(positions)
    for l in range(depth):
        w_a, w_t, gate_off = _proj_weight(w_in[l])
        w_gate = w_in[l][:, gate_off:].astype(BF16)
        pad = jnp.zeros((d, LANES - N_EXPERTS - N_EXPERT_GROUPS), F32)
        w_router = jnp.concatenate([w_router_expert[l], w_router_group[l], pad], axis=1)
        b_router = jnp.concatenate([b_router_expert[l], b_router_group[l], pad[0]]).reshape(1, LANES)
        qa, ka, vat, iq, ik, iwt, y_b, qc, kc, vct, y_d = _project(
            x2, norm1_g[l], w_a, w_t, tables, sgu_ln_g[l], sgu_ln_b[l], w_spatial[l], b_spatial[l],
            conv_w[l], seq)
        y_a = _dsa(iq, iwt, qa, ik, ka, vat, batch, seq)
        y_c = _stick_breaking(qc, kc, vct, batch, seq)
        x1, comb = _merge(x2, norm1_g[l], (y_a, y_b, y_c, y_d), w_gate, w_branch[l].astype(BF16),
                          w_out[l].astype(BF16), norm2_g[l], w_router, b_router)
        x2 = _moe(x1, norm2_g[l], comb, w_exp_gate[l].astype(BF16), w_exp_up[l].astype(BF16),
                  w_exp_down[l].astype(BF16), norm_f_g, final_norm=(l == depth - 1))
    return x2.reshape(batch, seq, d)
```

```python
import functools

import jax
import jax.numpy as jnp
from jax import lax
from jax.experimental import pallas as pl
from jax.experimental.pallas import tpu as pltpu

F32 = jnp.float32
BF16 = jnp.bfloat16

EPS = 1e-6
ROPE_THETA = 10000.0
HEAD_DIM = 64
BRANCH_WIDTH = 256
N_HEADS = 4
IDX_DIM = 32
TOPK_MAX = 256
CHUNK = 128
N_GROUPS_B = 4
CONV_WIDTH = 3
N_BRANCH = 4
N_EXPERT_GROUPS = 4
EXPERTS_PER_GROUP = 8
N_EXPERTS = N_EXPERT_GROUPS * EXPERTS_PER_GROUP
D_EXPERT = 256

LANES = 128
SUBLANES = 8
Q_BLOCK = 256
K_TILE = Q_BLOCK
TILES_PER_STEP = 2
COUNT_CHUNK = 512
COUNT_ROWS = 64
PROJ_ROWS = 256
MERGE_ROWS = 512
MOE_ROWS = 1024
CONV_HALO = 8
VMEM_LIMIT = 56 * 1024 * 1024
PHANTOM_LOGIT = -1e4
NEG_BIG = -1e30
INT_MIN = -2 ** 31
HALF_RANGE = 2 ** 15

_OFF = {}
_o = 0
for _name, _w in (("q", 512), ("k", 512), ("iq", 256), ("ik", 256),
                  ("ub", 256), ("vb", 256), ("qc", 256), ("kc", 256),
                  ("bd", 256), ("cd", 256), ("xd", 256)):
    _OFF[_name] = (_o, _w)
    _o += _w
W_A_COLS = _o
T_VA, T_VC, T_IW = 0, BRANCH_WIDTH, 2 * BRANCH_WIDTH
W_T_ROWS = 2 * BRANCH_WIDTH + 16

_CONTRACT_LAST = (((1,), (1,)), ((), ()))


def _params(*sem):
    return pltpu.CompilerParams(dimension_semantics=sem, vmem_limit_bytes=VMEM_LIMIT)


def _rope_table_kernel(pos_ref, inv_a_ref, inv_i_ref, cos_a, sin_a, cos_i, sin_i):
    p = pos_ref[...]
    a = p * inv_a_ref[...]
    cos_a[...] = jnp.cos(a)
    sin_a[...] = jnp.sin(a)
    b = p * inv_i_ref[...]
    cos_i[...] = jnp.cos(b)
    sin_i[...] = jnp.sin(b)


def _rope_tables(positions):
    n = positions.size
    rows = PROJ_ROWS
    pos = positions.astype(F32).reshape(n, 1)
    inv_a = ROPE_THETA ** (-jnp.arange(0, HEAD_DIM, 2, dtype=F32) / HEAD_DIM)
    inv_i = ROPE_THETA ** (-jnp.arange(0, IDX_DIM, 2, dtype=F32) / IDX_DIM)
    inv_a = jnp.tile(inv_a, LANES // inv_a.size).reshape(1, LANES)
    inv_i = jnp.tile(inv_i, LANES // inv_i.size).reshape(1, LANES)
    tab = jax.ShapeDtypeStruct((n, LANES), F32)
    row_spec = pl.BlockSpec((rows, LANES), lambda i: (i, 0))
    const = pl.BlockSpec((1, LANES), lambda i: (0, 0))
    return pl.pallas_call(
        _rope_table_kernel,
        grid=(n // rows,),
        in_specs=[pl.BlockSpec((rows, 1), lambda i: (i, 0)), const, const],
        out_specs=[row_spec] * 4,
        out_shape=[tab] * 4,
        compiler_params=_params("arbitrary"),
        name="rope_tables",
    )(pos, inv_a, inv_i)


def _rms_bf16(x, g):
    ms = jnp.mean(x * x, axis=-1, keepdims=True)
    return (x * lax.rsqrt(ms + EPS) * g).astype(BF16)


def _proj_kernel(x_ref, g_ref, w_ref, wt_ref, cos_a, sin_a, cos_i, sin_i, lng_ref, lnb_ref, ws_ref,
                 bs_ref, cw_ref,
                 qa_o, ka_o, vat_o, iq_o, ik_o, iwt_o, yb_o, qc_o, kc_o, vct_o, yd_o,
                 zbuf, *, steps_per_seq):
    rows = x_ref.shape[0]
    h = _rms_bf16(x_ref[...], g_ref[...])

    def mm(name):
        off, width = _OFF[name]
        return jnp.dot(h, w_ref[:, off:off + width], preferred_element_type=F32)

    def rope(name, cos_ref, sin_ref, out_ref, scale):
        p = mm(name)
        w = p.shape[1] // 2
        c = cos_ref[...]
        s = sin_ref[...]
        for j in range(w // LANES):
            lo = j * LANES
            r = p[:, lo:lo + LANES] * c + p[:, w + lo:w + lo + LANES] * s
            out_ref[:, lo:lo + LANES] = (r if scale == 1.0 else r * scale).astype(out_ref.dtype)

    att_scale = HEAD_DIM ** -0.5
    rope("q", cos_a, sin_a, qa_o, att_scale)
    rope("k", cos_a, sin_a, ka_o, 1.0)
    rope("iq", cos_i, sin_i, iq_o, 1.0)
    rope("ik", cos_i, sin_i, ik_o, 1.0)
    qc_o[...] = (mm("qc") * att_scale).astype(BF16)
    kc_o[...] = mm("kc").astype(BF16)
    t = lax.dot_general(wt_ref[...], h, _CONTRACT_LAST, preferred_element_type=F32)
    vat_o[...] = t[T_VA:T_VA + BRANCH_WIDTH, :].astype(BF16)
    vct_o[...] = t[T_VC:T_VC + BRANCH_WIDTH, :].astype(BF16)
    iwt_o[...] = t[T_IW:T_IW + SUBLANES, :]

    u = jax.nn.gelu(mm("ub"))
    v = jax.nn.gelu(mm("vb"))
    mu = jnp.mean(v, axis=-1, keepdims=True)
    var = jnp.mean(jnp.square(v - mu), axis=-1, keepdims=True)
    vn = ((v - mu) * lax.rsqrt(var + EPS) * lng_ref[...] + lnb_ref[...]).astype(BF16)
    t_idx = lax.broadcasted_iota(jnp.int32, (CHUNK, CHUNK), 0)
    s_idx = lax.broadcasted_iota(jnp.int32, (CHUNK, CHUNK), 1)
    causal = s_idx <= t_idx
    group_masks = [_head_lane_mask(BRANCH_WIDTH, BRANCH_WIDTH // N_GROUPS_B, g) for g in range(N_GROUPS_B)]
    w_tril = [jnp.where(causal, ws_ref[g], 0.0).astype(BF16) for g in range(N_GROUPS_B)]
    for c in range(rows // CHUNK):
        vc = vn[c * CHUNK:(c + 1) * CHUNK, :]
        mixed = bs_ref[...]
        for g in range(N_GROUPS_B):
            full = jnp.dot(w_tril[g], vc, preferred_element_type=F32)
            mixed = mixed + jnp.where(group_masks[g], full, 0.0)
        yb_o[c * CHUNK:(c + 1) * CHUNK, :] = (u[c * CHUNK:(c + 1) * CHUNK, :] * mixed).astype(BF16)

    bd = mm("bd")
    z = mm("cd") * mm("xd")

    @pl.when(pl.program_id(0) % steps_per_seq == 0)
    def _():
        zbuf[0:CONV_HALO, :] = jnp.zeros((CONV_HALO, BRANCH_WIDTH), F32)

    zbuf[CONV_HALO:CONV_HALO + rows, :] = z
    y = cw_ref[CONV_WIDTH - 1:CONV_WIDTH, :] * z
    for tap in range(CONV_WIDTH - 1):
        shift = CONV_WIDTH - 1 - tap
        y = y + cw_ref[tap:tap + 1, :] * zbuf[CONV_HALO - shift:CONV_HALO - shift + rows, :]
    yd_o[...] = (bd * y).astype(BF16)
    zbuf[0:CONV_HALO, :] = z[rows - CONV_HALO:rows, :]


def _swap_halves(w, dim):
    d_in, cols = w.shape
    wh = w.reshape(d_in, cols // dim, 2, dim // 2)
    return jnp.stack([-wh[:, :, 1], wh[:, :, 0]], axis=2).reshape(d_in, cols)


def _proj_weight(w_in):
    d = w_in.shape[0]
    sizes = (BRANCH_WIDTH,) * 3 + (N_HEADS * IDX_DIM, IDX_DIM, N_HEADS) + (BRANCH_WIDTH,) * 8
    parts, o = [], 0
    for s in sizes:
        parts.append(w_in[:, o:o + s])
        o += s
    qa, ka, va, iq, ik, iw, ub, vb, qc, kc, vc, bd, cd, xd = parts
    ik_rep = jnp.tile(ik, (1, N_HEADS))
    cols = [qa, _swap_halves(qa, HEAD_DIM), ka, _swap_halves(ka, HEAD_DIM),
            iq, _swap_halves(iq, IDX_DIM), ik_rep, _swap_halves(ik_rep, IDX_DIM),
            ub, vb, qc, kc, bd, cd, xd]
    iw_pad = jnp.concatenate([iw, jnp.zeros((d, W_T_ROWS - 2 * BRANCH_WIDTH - N_HEADS), w_in.dtype)], axis=1)
    w_t = jnp.concatenate([va, vc, iw_pad], axis=1).T
    return jnp.concatenate(cols, axis=1).astype(BF16), w_t.astype(BF16), o


def _project(x2, norm_g, w_a, w_t, tables, ln_g, ln_b, w_spatial, b_spatial, conv_w, seq):
    n, d = x2.shape
    rows = PROJ_ROWS
    cos_a, sin_a, cos_i, sin_i = tables
    bias = jnp.repeat(b_spatial.T, BRANCH_WIDTH // N_GROUPS_B, axis=1)
    row = lambda w: pl.BlockSpec((rows, w), lambda i: (i, 0))
    col = lambda r: pl.BlockSpec((r, rows), lambda i: (0, i))
    full = lambda a: pl.BlockSpec(a.shape, lambda i: (0,) * a.ndim)
    bf = lambda w: jax.ShapeDtypeStruct((n, w), BF16)
    bft = jax.ShapeDtypeStruct((BRANCH_WIDTH, n), BF16)
    args = (x2, norm_g.reshape(1, d), w_a, w_t, cos_a, sin_a, cos_i, sin_i, ln_g.reshape(1, -1),
            ln_b.reshape(1, -1), w_spatial, bias, conv_w)
    in_specs = [row(d), full(args[1]), full(w_a), full(w_t), row(LANES), row(LANES), row(LANES),
                row(LANES), full(args[8]), full(args[9]), full(w_spatial), full(bias), full(conv_w)]
    out_shape = [bf(256), bf(256), bft, bf(128), bf(128), jax.ShapeDtypeStruct((SUBLANES, n), F32),
                 bf(256), bf(256), bf(256), bft, bf(256)]
    out_specs = [row(256), row(256), col(BRANCH_WIDTH), row(128), row(128), col(SUBLANES),
                 row(256), row(256), row(256), col(BRANCH_WIDTH), row(256)]
    return pl.pallas_call(
        functools.partial(_proj_kernel, steps_per_seq=seq // rows),
        grid=(n // rows,),
        in_specs=in_specs,
        out_specs=out_specs,
        out_shape=out_shape,
        scratch_shapes=[pltpu.VMEM((CONV_HALO + rows, BRANCH_WIDTH), F32)],
        compiler_params=_params("arbitrary"),
        name="project",
    )(*args)


def _head_lane_mask(width, per_head, h, rows=1):
    lane = lax.broadcasted_iota(jnp.int32, (rows, width), 1)
    return (lane >= h * per_head) & (lane < (h + 1) * per_head)


def _masked_heads(x, per_head):
    return [jnp.where(_head_lane_mask(x.shape[1], per_head, h), x, jnp.zeros_like(x))
            for h in range(N_HEADS)]


def _float_to_key(x):
    bits = lax.bitcast_convert_type(x, jnp.int32)
    sign = bits >> 31
    return (bits ^ (sign & jnp.int32(0x7FFFFFFF))) - sign


def _key_to_float(key):
    bits = jnp.where(key < 0, (-key) | jnp.int32(INT_MIN), key)
    return lax.bitcast_convert_type(bits, F32)


def _col_reduce(x, op, final):
    while x.shape[0] > SUBLANES:
        half = x.shape[0] // 2
        x = op(x[:half], x[half:])
    return final(x, axis=0, keepdims=True)


def _col_sum(x):
    return _col_reduce(x, jnp.add, jnp.sum)


def _col_max(x):
    return _col_reduce(x, jnp.maximum, jnp.max)


def _step_starts(step):
    return [pl.multiple_of((step * TILES_PER_STEP + g) * K_TILE, K_TILE) for g in range(TILES_PER_STEP)]


def _dsa_kernel(iq_ref, iwt_ref, q_ref, ik_ref, k_ref, vt_ref, o_ref, sc_ref, hi_ref, lo_ref, *, topk):
    qb = q_ref.shape[0]
    i = pl.program_id(1)
    n_tiles = i + 1
    n_steps = (n_tiles + TILES_PER_STEP - 1) // TILES_PER_STEP

    @pl.when(i == 0)
    def _():
        sc_ref[...] = jnp.full(sc_ref.shape, -jnp.inf, F32)
        hi_ref[...] = jnp.full(hi_ref.shape, -HALF_RANGE, jnp.int16)
        lo_ref[...] = jnp.full(lo_ref.shape, -HALF_RANGE, jnp.int16)

    iq_heads = _masked_heads(iq_ref[...], IDX_DIM)
    iwt = iwt_ref[...]
    iw_heads = [iwt[h:h + 1, :] for h in range(N_HEADS)]

    def put_scores(start, sc):
        sc_ref[pl.ds(start, K_TILE), :] = sc
        key = _float_to_key(sc)
        hi_ref[pl.ds(start, K_TILE), :] = (key >> 16).astype(jnp.int16)
        lo_ref[pl.ds(start, K_TILE), :] = ((key & 0xFFFF) - HALF_RANGE).astype(jnp.int16)

    key_off = lax.broadcasted_iota(jnp.int32, (K_TILE, qb), 0)
    qry_pos = i * qb + lax.broadcasted_iota(jnp.int32, (K_TILE, qb), 1)

    def score_step(jj, masked):
        starts = _step_starts(jj)
        dots = [[lax.dot_general(ik_ref[pl.ds(st, K_TILE), :], iq_heads[h], _CONTRACT_LAST,
                                 preferred_element_type=F32) for h in range(N_HEADS)] for st in starts]
        for g in range(TILES_PER_STEP):
            sc = None
            for h in range(N_HEADS):
                t = iw_heads[h] * jnp.maximum(dots[g][h], 0.0)
                sc = t if sc is None else sc + t
            if masked:
                sc = jnp.where(starts[g] + key_off <= qry_pos, sc, -jnp.inf)
            put_scores(starts[g], sc)

    def full_step(jj, carry):
        score_step(jj, False)
        return carry

    lax.fori_loop(0, i // TILES_PER_STEP, full_step, 0)
    score_step(i // TILES_PER_STEP, True)

    n_chunks = (n_tiles * K_TILE + COUNT_CHUNK - 1) // COUNT_CHUNK
    kf = jnp.float32(topk)

    def chunk_rows(c):
        base = pl.multiple_of(c * COUNT_CHUNK, COUNT_CHUNK)
        return [pl.ds(base + r * COUNT_ROWS, COUNT_ROWS) for r in range(COUNT_CHUNK // COUNT_ROWS)]

    def count(ref, pred, dtype):
        one, zero = jnp.ones((), dtype), jnp.zeros((), dtype)

        def body(c, acc):
            for rows in chunk_rows(c):
                acc = acc + jnp.where(pred(ref[rows, :]), one, zero)
            return acc
        acc = lax.fori_loop(0, n_chunks, body, jnp.zeros((COUNT_ROWS, qb), dtype))
        return jnp.sum(acc.astype(F32), axis=0, keepdims=True)

    def bisect16(ref, target):
        def bit_step(b, carry):
            best, cnt_best = carry
            cand = best + lax.shift_left(jnp.int32(1), 15 - b)
            cand16 = cand.astype(jnp.int16)
            cnt = count(ref, lambda blk: blk >= cand16, jnp.int16)
            ok = cnt >= target
            return jnp.where(ok, cand, best), jnp.where(ok, cnt, cnt_best)
        init = (jnp.full((1, qb), -HALF_RANGE, jnp.int32), jnp.zeros((1, qb), F32))
        return lax.fori_loop(0, 16, bit_step, init)

    hi_t, cnt_hi = bisect16(hi_ref, kf)
    hi_t16 = hi_t.astype(jnp.int16)
    above = count(hi_ref, lambda blk: blk > hi_t16, jnp.int16)

    def mask_low(c, carry):
        for rows in chunk_rows(c):
            lo_ref[rows, :] = jnp.where(hi_ref[rows, :] == hi_t16, lo_ref[rows, :], jnp.int16(-HALF_RANGE))
        return carry

    lax.fori_loop(0, n_chunks, mask_low, 0)
    lo_t, cnt_lo = bisect16(lo_ref, kf - above)
    cnt_key = above + jnp.where(lo_t == -HALF_RANGE, cnt_hi - above, cnt_lo)
    key = hi_t * (2 * HALF_RANGE) + (lo_t + HALF_RANGE)
    few = i * qb + lax.broadcasted_iota(jnp.int32, (1, qb), 1) + 1 < topk
    thr = jnp.where(few, jnp.finfo(F32).min, _key_to_float(key))

    @pl.when(jnp.max(jnp.where(few, kf, cnt_key)) > kf)
    def _():
        need = kf - count(sc_ref, lambda blk: blk > thr, F32)
        before = (lax.broadcasted_iota(jnp.int32, (K_TILE, K_TILE), 1)
                  < lax.broadcasted_iota(jnp.int32, (K_TILE, K_TILE), 0))
        before = jnp.where(before, 1.0, 0.0).astype(BF16)

        def tie_body(jj, seen):
            starts = _step_starts(jj)
            blks = [sc_ref[pl.ds(st, K_TILE), :] for st in starts]
            eqs = [blk == thr for blk in blks]
            eq_fs = [jnp.where(eq, 1.0, 0.0) for eq in eqs]
            ranks = [jnp.dot(before, eq_f.astype(BF16), preferred_element_type=F32) for eq_f in eq_fs]
            for g in range(TILES_PER_STEP):
                drop = eqs[g] & (seen + ranks[g] >= need)
                sc_ref[pl.ds(starts[g], K_TILE), :] = jnp.where(drop, -jnp.inf, blks[g])
                seen = seen + _col_sum(eq_fs[g])
            return seen

        lax.fori_loop(0, n_steps, tie_body, jnp.zeros((1, qb), F32))

    q_heads = _masked_heads(q_ref[...], HEAD_DIM)

    def att_step(jj, carry):
        m, l, acc = carry
        starts = _step_starts(jj)
        sels = [sc_ref[pl.ds(st, K_TILE), :] >= thr for st in starts]
        logits = [[lax.dot_general(k_ref[pl.ds(st, K_TILE), :], q_heads[h], _CONTRACT_LAST,
                                   preferred_element_type=F32) for h in range(N_HEADS)] for st in starts]
        m_new, l_new, alphas, probs = [], [], [], []
        for h in range(N_HEADS):
            ss = [jnp.where(sels[g], logits[g][h], NEG_BIG) for g in range(TILES_PER_STEP)]
            mh = m[h]
            for s in ss:
                mh = jnp.maximum(mh, _col_max(s))
            ps = [jnp.exp(s - mh) for s in ss]
            alpha = jnp.exp(m[h] - mh)
            lh = alpha * l[h]
            for p in ps:
                lh = lh + _col_sum(p)
            m_new.append(mh)
            alphas.append(alpha)
            l_new.append(lh)
            probs.append([p.astype(BF16) for p in ps])
        acc_new = []
        for h in range(N_HEADS):
            pv = None
            for g, st in enumerate(starts):
                vt = vt_ref[h * HEAD_DIM:(h + 1) * HEAD_DIM, pl.ds(st, K_TILE)]
                t = jnp.dot(vt, probs[h][g], preferred_element_type=F32)
                pv = t if pv is None else pv + t
            acc_new.append(alphas[h] * acc[h] + pv)
        return tuple(m_new), tuple(l_new), tuple(acc_new)

    init = (tuple(jnp.full((1, qb), NEG_BIG, F32) for _ in range(N_HEADS)),
            tuple(jnp.zeros((1, qb), F32) for _ in range(N_HEADS)),
            tuple(jnp.zeros((HEAD_DIM, qb), F32) for _ in range(N_HEADS)))
    _, l, acc = lax.fori_loop(0, n_steps, att_step, init)
    out_t = jnp.concatenate([acc[h] / l[h] for h in range(N_HEADS)], axis=0)
    o_ref[...] = out_t.T.astype(o_ref.dtype)


def _dsa(iq, iwt, qa, ik, ka, vat, batch, seq):
    n = qa.shape[0]
    nq = seq // Q_BLOCK
    topk = min(TOPK_MAX, seq // 4)
    qrow = lambda w: pl.BlockSpec((Q_BLOCK, w), lambda b, i: (b * nq + i, 0))
    whole = lambda w: pl.BlockSpec((seq, w), lambda b, i: (b, 0))
    return pl.pallas_call(
        functools.partial(_dsa_kernel, topk=topk),
        grid=(batch, nq),
        in_specs=[qrow(LANES), pl.BlockSpec((SUBLANES, Q_BLOCK), lambda b, i: (0, b * nq + i)),
                  qrow(BRANCH_WIDTH), whole(LANES), whole(BRANCH_WIDTH),
                  pl.BlockSpec((BRANCH_WIDTH, seq), lambda b, i: (0, b))],
        out_specs=qrow(BRANCH_WIDTH),
        out_shape=jax.ShapeDtypeStruct((n, BRANCH_WIDTH), BF16),
        scratch_shapes=[pltpu.VMEM((seq, Q_BLOCK), F32), pltpu.VMEM((seq, Q_BLOCK), jnp.int16),
                        pltpu.VMEM((seq, Q_BLOCK), jnp.int16)],
        compiler_params=_params("arbitrary", "arbitrary"),
        name="dsa",
    )(iq, iwt, qa, ik, ka, vat)


def _stick_kernel(q_ref, k_ref, vt_ref, o_ref):
    qb = q_ref.shape[0]
    i = pl.program_id(1)
    q_heads = _masked_heads(q_ref[...], HEAD_DIM)
    after = (lax.broadcasted_iota(jnp.int32, (K_TILE, K_TILE), 1)
             > lax.broadcasted_iota(jnp.int32, (K_TILE, K_TILE), 0))
    after = jnp.where(after, 1.0, 0.0).astype(BF16)
    key_off = lax.broadcasted_iota(jnp.int32, (K_TILE, qb), 0)
    qry_pos = i * qb + lax.broadcasted_iota(jnp.int32, (K_TILE, qb), 1)

    def step(u, carry, masked):
        tails, acc = carry
        first = i - 2 * u
        tiles = [first, jnp.maximum(first - 1, 0)]
        bias = [None, jnp.where(first >= 1, 0.0, PHANTOM_LOGIT)]
        starts = [pl.multiple_of(t * K_TILE, K_TILE) for t in tiles]
        stricts = [starts[g] + key_off < qry_pos if masked else None for g in range(2)]
        zs = [[lax.dot_general(k_ref[pl.ds(st, K_TILE), :], q_heads[h], _CONTRACT_LAST,
                               preferred_element_type=F32) for h in range(N_HEADS)] for st in starts]
        log_betas, splits, befores = [], [], []
        tails = list(tails)
        for g in range(2):
            lbs, sps, bfs = [], [], []
            for h in range(N_HEADS):
                z = zs[g][h] if bias[g] is None else zs[g][h] + bias[g]
                log_beta = jnp.minimum(z, 0.0) - jnp.log(1.0 + jnp.exp(-jnp.abs(z)))
                log_keep = log_beta - z
                if masked:
                    log_keep = jnp.where(stricts[g], log_keep, 0.0)
                lbs.append(log_beta)
                sps.append(log_keep.astype(BF16))
                bfs.append(tails[h])
                tails[h] = tails[h] + _col_sum(log_keep)
            log_betas.append(lbs)
            splits.append(sps)
            befores.append(bfs)
        laters = [[jnp.dot(after, splits[g][h], preferred_element_type=F32) for h in range(N_HEADS)]
                  for g in range(2)]
        new_acc = []
        for h in range(N_HEADS):
            total = acc[h]
            for g in range(2):
                a = jnp.exp(log_betas[g][h] + laters[g][h] + befores[g][h])
                if masked:
                    a = jnp.where(stricts[g], a, 0.0)
                vt = vt_ref[h * HEAD_DIM:(h + 1) * HEAD_DIM, pl.ds(starts[g], K_TILE)]
                total = total + jnp.dot(vt, a.astype(BF16), preferred_element_type=F32)
            new_acc.append(total)
        return tuple(tails), tuple(new_acc)

    init = (tuple(jnp.zeros((1, qb), F32) for _ in range(N_HEADS)),
            tuple(jnp.zeros((HEAD_DIM, qb), F32) for _ in range(N_HEADS)))
    carry = step(0, init, True)
    _, acc = lax.fori_loop(1, (i + 2) // 2, lambda u, c: step(u, c, False), carry)
    o_ref[...] = jnp.concatenate(acc, axis=0).T.astype(o_ref.dtype)


def _stick_breaking(qc, kc, vct, batch, seq):
    n = qc.shape[0]
    nq = seq // Q_BLOCK
    qrow = pl.BlockSpec((Q_BLOCK, BRANCH_WIDTH), lambda b, i: (b * nq + i, 0))
    return pl.pallas_call(
        _stick_kernel,
        grid=(batch, nq),
        in_specs=[qrow, pl.BlockSpec((seq, BRANCH_WIDTH), lambda b, i: (b, 0)),
                  pl.BlockSpec((BRANCH_WIDTH, seq), lambda b, i: (0, b))],
        out_specs=qrow,
        out_shape=jax.ShapeDtypeStruct((n, BRANCH_WIDTH), BF16),
        compiler_params=_params("arbitrary", "arbitrary"),
        name="stick_breaking",
    )(qc, kc, vct)


def _merge_kernel(x_ref, g1_ref, ya_ref, yb_ref, yc_ref, yd_ref, wg_ref, wb_ref, wo_ref,
                  g2_ref, wr_ref, br_ref, x1_o, comb_o):
    x = x_ref[...]
    d = x.shape[1]
    h = _rms_bf16(x, g1_ref[...])
    ys = (ya_ref[...], yb_ref[...], yc_ref[...], yd_ref[...])
    col = 256
    pieces = []
    for c in range(d // col):
        m = None
        for n in range(N_BRANCH):
            logit = jnp.dot(h, wg_ref[:, n * d + c * col:n * d + (c + 1) * col],
                            preferred_element_type=F32)
            br = jnp.dot(ys[n], wb_ref[n, :, c * col:(c + 1) * col], preferred_element_type=F32)
            t = jax.nn.sigmoid(logit) * br
            m = t if m is None else m + t
        pieces.append(m.astype(BF16))
    merged = jnp.concatenate(pieces, axis=1)
    x1 = x + jnp.dot(merged, wo_ref[...], preferred_element_type=F32)
    x1_o[...] = x1

    ms = jnp.mean(x1 * x1, axis=-1, keepdims=True)
    h2 = x1 * lax.rsqrt(ms + EPS) * g2_ref[...]
    h_hi = h2.astype(BF16)
    h_lo = (h2 - h_hi.astype(F32)).astype(BF16)
    both = jnp.dot(h_hi, wr_ref[...], preferred_element_type=F32)
    cross = jnp.dot(h_lo, wr_ref[:, 0:LANES], preferred_element_type=F32)
    logits = both[:, 0:LANES] + both[:, LANES:2 * LANES] + cross + br_ref[...]
    rows = logits.shape[0]
    lane = lax.broadcasted_iota(jnp.int32, (rows, LANES), 1)
    neg = -jnp.inf
    is_grp = (lane >= N_EXPERTS) & (lane < N_EXPERTS + N_EXPERT_GROUPS)
    gl = jnp.where(is_grp, logits, neg)
    gmax = jnp.max(gl, axis=1, keepdims=True)
    gidx = jnp.min(jnp.where(gl == gmax, lane, LANES), axis=1, keepdims=True) - N_EXPERTS
    grp_p = 1.0 / jnp.sum(jnp.where(is_grp, jnp.exp(gl - gmax), 0.0), axis=1, keepdims=True)
    in_grp = (lane >= gidx * EXPERTS_PER_GROUP) & (lane < (gidx + 1) * EXPERTS_PER_GROUP)
    el = jnp.where(in_grp, logits, neg)
    l1 = jnp.max(el, axis=1, keepdims=True)
    j1 = jnp.min(jnp.where(el == l1, lane, LANES), axis=1, keepdims=True)
    el2 = jnp.where(lane == j1, neg, el)
    l2 = jnp.max(el2, axis=1, keepdims=True)
    j2 = jnp.min(jnp.where(el2 == l2, lane, LANES), axis=1, keepdims=True)
    e2 = jnp.exp(l2 - l1)
    w1 = grp_p / (1.0 + e2)
    w2 = grp_p * e2 / (1.0 + e2)
    comb_o[...] = jnp.where(lane == j1, w1, 0.0) + jnp.where(lane == j2, w2, 0.0)


def _merge(x2, norm1_g, ys, w_gate, w_branch, w_out, norm2_g, w_router, b_router):
    n, d = x2.shape
    rows = MERGE_ROWS
    row = lambda w: pl.BlockSpec((rows, w), lambda i: (i, 0))
    full = lambda a: pl.BlockSpec(a.shape, lambda i: (0,) * a.ndim)
    args = (x2, norm1_g.reshape(1, d), *ys, w_gate, w_branch, w_out, norm2_g.reshape(1, d),
            w_router, b_router)
    in_specs = [row(d), full(args[1])] + [row(BRANCH_WIDTH)] * 4 + [full(a) for a in args[6:]]
    return pl.pallas_call(
        _merge_kernel,
        grid=(n // rows,),
        in_specs=in_specs,
        out_specs=[row(d), row(LANES)],
        out_shape=[jax.ShapeDtypeStruct((n, d), F32), jax.ShapeDtypeStruct((n, LANES), F32)],
        compiler_params=_params("arbitrary"),
        name="merge_router",
    )(*args)


def _moe_kernel(x_ref, g2_ref, comb_ref, wg_ref, wu_ref, wd_ref, gf_ref, o_ref, h_scr, acc_scr,
                *, final_norm):
    e = pl.program_id(1)

    @pl.when(e == 0)
    def _():
        h_scr[...] = _rms_bf16(x_ref[...], g2_ref[...])
        acc_scr[...] = jnp.zeros(acc_scr.shape, F32)

    h = h_scr[...]
    a = jnp.dot(h, wg_ref[0], preferred_element_type=F32)
    u = jnp.dot(h, wu_ref[0], preferred_element_type=F32)
    comb = comb_ref[...]
    lane = lax.broadcasted_iota(jnp.int32, comb.shape, 1)
    w = jnp.sum(jnp.where(lane == e, comb, 0.0), axis=1, keepdims=True)
    hh = (jax.nn.silu(a) * u * w).astype(BF16)
    acc_scr[...] += jnp.dot(hh, wd_ref[0], preferred_element_type=F32)

    @pl.when(e == pl.num_programs(1) - 1)
    def _():
        y = x_ref[...] + acc_scr[...]
        if final_norm:
            ms = jnp.mean(y * y, axis=-1, keepdims=True)
            y = y * lax.rsqrt(ms + EPS) * gf_ref[...]
        o_ref[...] = y


def _moe(x1, norm2_g, comb, w_g, w_u, w_d, norm_f_g, final_norm):
    n, d = x1.shape
    rows = min(MOE_ROWS, n)
    n_exp = w_g.shape[0]
    row = lambda w: pl.BlockSpec((rows, w), lambda i, e: (i, 0))
    vec = pl.BlockSpec((1, d), lambda i, e: (0, 0))
    return pl.pallas_call(
        functools.partial(_moe_kernel, final_norm=final_norm),
        grid=(n // rows, n_exp),
        in_specs=[row(d), vec, row(LANES),
                  pl.BlockSpec((1, d, D_EXPERT), lambda i, e: (e, 0, 0)),
                  pl.BlockSpec((1, d, D_EXPERT), lambda i, e: (e, 0, 0)),
                  pl.BlockSpec((1, D_EXPERT, d), lambda i, e: (e, 0, 0)),
                  vec],
        out_specs=row(d),
        out_shape=jax.ShapeDtypeStruct((n, d), F32),
        scratch_shapes=[pltpu.VMEM((rows, d), BF16), pltpu.VMEM((rows, d), F32)],
        compiler_params=_params("arbitrary", "arbitrary"),
        name="experts",
    )(x1, norm2_g.reshape(1, d), comb, w_g, w_u, w_d, norm_f_g.reshape(1, d))


def kernel(x, positions, norm1_g, w_in, sgu_ln_g, sgu_ln_b, w_spatial, b_spatial, conv_w, w_branch,
           w_out, norm2_g, w_router_group, b_router_group, w_router_expert, b_router_expert,
           w_exp_gate, w_exp_up, w_exp_down, norm_f_g):
    batch, seq, d = x.shape
    depth = w_in.shape[0]
    n = batch * seq
    assert TILES_PER_STEP == 2
    assert seq % COUNT_CHUNK == 0 and seq % (TILES_PER_STEP * K_TILE) == 0
    assert n % PROJ_ROWS == 0 and n % MERGE_ROWS == 0
    x2 = x.reshape(n, d)
    tables = _rope_tables(positions)
    for l in range(depth):
        w_a, w_t, gate_off = _proj_weight(w_in[l])
        w_gate = w_in[l][:, gate_off:].astype(BF16)
        pad = jnp.zeros((d, LANES - N_EXPERTS - N_EXPERT_GROUPS), F32)
        w_router = jnp.concatenate([w_router_expert[l], w_router_group[l], pad], axis=1)
        w_router_hi = w_router.astype(BF16)
        w_router_lo = (w_router - w_router_hi.astype(F32)).astype(BF16)
        w_router = jnp.concatenate([w_router_hi, w_router_lo], axis=1)
        b_router = jnp.concatenate([b_router_expert[l], b_router_group[l], pad[0]]).reshape(1, LANES)
        qa, ka, vat, iq, ik, iwt, y_b, qc, kc, vct, y_d = _project(
            x2, norm1_g[l], w_a, w_t, tables, sgu_ln_g[l], sgu_ln_b[l], w_spatial[l], b_spatial[l],
            conv_w[l], seq)
        y_a = _dsa(iq, iwt, qa, ik, ka, vat, batch, seq)
        y_c = _stick_breaking(qc, kc, vct, batch, seq)
        x1, comb = _merge(x2, norm1_g[l], (y_a, y_b, y_c, y_d), w_gate, w_branch[l].astype(BF16),
                          w_out[l].astype(BF16), norm2_g[l], w_router, b_router)
        x2 = _moe(x1, norm2_g[l], comb, w_exp_gate[l].astype(BF16), w_exp_up[l].astype(BF16),
                  w_exp_down[l].astype(BF16), norm_f_g, final_norm=(l == depth - 1))
    return x2.reshape(batch, seq, d)
```

```python
import functools

import jax
import jax.numpy as jnp
from jax import lax
from jax.experimental import pallas as pl
from jax.experimental.pallas import tpu as pltpu

F32 = jnp.float32
BF16 = jnp.bfloat16

EPS = 1e-6
ROPE_THETA = 10000.0
HEAD_DIM = 64
BRANCH_WIDTH = 256
N_HEADS = 4
IDX_DIM = 32
TOPK_MAX = 256
CHUNK = 128
N_GROUPS_B = 4
CONV_WIDTH = 3
N_BRANCH = 4
N_EXPERT_GROUPS = 4
EXPERTS_PER_GROUP = 8
N_EXPERTS = N_EXPERT_GROUPS * EXPERTS_PER_GROUP
D_EXPERT = 256

LANES = 128
SUBLANES = 8
Q_BLOCK = 256
K_TILE = Q_BLOCK
TILES_PER_STEP = 2
COUNT_CHUNK = 512
COUNT_ROWS = 64
PROJ_ROWS = 256
MERGE_ROWS = 512
MOE_ROWS = 1024
EXPERTS_PER_STEP = 4
CONV_HALO = 8
VMEM_LIMIT = 56 * 1024 * 1024
PHANTOM_LOGIT = -1e4
NEG_BIG = -1e30
INT_MIN = -2 ** 31
HALF_RANGE = 2 ** 15
ONES_ROWS = 16

_OFF = {}
_o = 0
for _name, _w in (("q", 512), ("k", 512), ("iq", 256), ("ik", 256),
                  ("ub", 256), ("vb", 256), ("qc", 256), ("kc", 256),
                  ("bd", 256), ("cd", 256), ("xd", 256)):
    _OFF[_name] = (_o, _w)
    _o += _w
W_A_COLS = _o
T_VA, T_VC, T_IW = 0, BRANCH_WIDTH, 2 * BRANCH_WIDTH
W_T_ROWS = 2 * BRANCH_WIDTH + 16

_CONTRACT_LAST = (((1,), (1,)), ((), ()))


def _params(*sem):
    return pltpu.CompilerParams(dimension_semantics=sem, vmem_limit_bytes=VMEM_LIMIT)


def _rope_table_kernel(pos_ref, inv_a_ref, inv_i_ref, cos_a, sin_a, cos_i, sin_i):
    p = pos_ref[...]
    a = p * inv_a_ref[...]
    cos_a[...] = jnp.cos(a)
    sin_a[...] = jnp.sin(a)
    b = p * inv_i_ref[...]
    cos_i[...] = jnp.cos(b)
    sin_i[...] = jnp.sin(b)


def _rope_tables(positions):
    n = positions.size
    rows = PROJ_ROWS
    pos = positions.astype(F32).reshape(n, 1)
    inv_a = ROPE_THETA ** (-jnp.arange(0, HEAD_DIM, 2, dtype=F32) / HEAD_DIM)
    inv_i = ROPE_THETA ** (-jnp.arange(0, IDX_DIM, 2, dtype=F32) / IDX_DIM)
    inv_a = jnp.tile(inv_a, LANES // inv_a.size).reshape(1, LANES)
    inv_i = jnp.tile(inv_i, LANES // inv_i.size).reshape(1, LANES)
    tab = jax.ShapeDtypeStruct((n, LANES), F32)
    row_spec = pl.BlockSpec((rows, LANES), lambda i: (i, 0))
    const = pl.BlockSpec((1, LANES), lambda i: (0, 0))
    return pl.pallas_call(
        _rope_table_kernel,
        grid=(n // rows,),
        in_specs=[pl.BlockSpec((rows, 1), lambda i: (i, 0)), const, const],
        out_specs=[row_spec] * 4,
        out_shape=[tab] * 4,
        compiler_params=_params("arbitrary"),
        name="rope_tables",
    )(pos, inv_a, inv_i)


def _rms_bf16(x, g):
    ms = jnp.mean(x * x, axis=-1, keepdims=True)
    return (x * lax.rsqrt(ms + EPS) * g).astype(BF16)


def _proj_kernel(x_ref, g_ref, w_ref, wt_ref, cos_a, sin_a, cos_i, sin_i, lng_ref, lnb_ref, ws_ref,
                 bs_ref, cw_ref,
                 qa_o, ka_o, vat_o, iq_o, ik_o, iwt_o, yb_o, qc_o, kc_o, vct_o, yd_o,
                 zbuf, *, steps_per_seq):
    rows = x_ref.shape[0]
    h = _rms_bf16(x_ref[...], g_ref[...])

    def mm(name):
        off, width = _OFF[name]
        return jnp.dot(h, w_ref[:, off:off + width], preferred_element_type=F32)

    def rope(name, cos_ref, sin_ref, out_ref, scale):
        p = mm(name)
        w = p.shape[1] // 2
        c = cos_ref[...]
        s = sin_ref[...]
        for j in range(w // LANES):
            lo = j * LANES
            r = p[:, lo:lo + LANES] * c + p[:, w + lo:w + lo + LANES] * s
            out_ref[:, lo:lo + LANES] = (r if scale == 1.0 else r * scale).astype(out_ref.dtype)

    att_scale = HEAD_DIM ** -0.5
    rope("q", cos_a, sin_a, qa_o, att_scale)
    rope("k", cos_a, sin_a, ka_o, 1.0)
    rope("iq", cos_i, sin_i, iq_o, 1.0)
    rope("ik", cos_i, sin_i, ik_o, 1.0)
    qc_o[...] = (mm("qc") * att_scale).astype(BF16)
    kc_o[...] = mm("kc").astype(BF16)
    t = lax.dot_general(wt_ref[...], h, _CONTRACT_LAST, preferred_element_type=F32)
    vat_o[...] = t[T_VA:T_VA + BRANCH_WIDTH, :].astype(BF16)
    vct_o[...] = t[T_VC:T_VC + BRANCH_WIDTH, :].astype(BF16)
    iwt_o[...] = t[T_IW:T_IW + SUBLANES, :]

    u = jax.nn.gelu(mm("ub"))
    v = jax.nn.gelu(mm("vb"))
    mu = jnp.mean(v, axis=-1, keepdims=True)
    var = jnp.mean(jnp.square(v - mu), axis=-1, keepdims=True)
    vn = ((v - mu) * lax.rsqrt(var + EPS) * lng_ref[...] + lnb_ref[...]).astype(BF16)
    t_idx = lax.broadcasted_iota(jnp.int32, (CHUNK, CHUNK), 0)
    s_idx = lax.broadcasted_iota(jnp.int32, (CHUNK, CHUNK), 1)
    causal = s_idx <= t_idx
    group_masks = [_head_lane_mask(BRANCH_WIDTH, BRANCH_WIDTH // N_GROUPS_B, g) for g in range(N_GROUPS_B)]
    w_tril = [jnp.where(causal, ws_ref[g], 0.0).astype(BF16) for g in range(N_GROUPS_B)]
    for c in range(rows // CHUNK):
        vc = vn[c * CHUNK:(c + 1) * CHUNK, :]
        mixed = bs_ref[...]
        for g in range(N_GROUPS_B):
            full = jnp.dot(w_tril[g], vc, preferred_element_type=F32)
            mixed = mixed + jnp.where(group_masks[g], full, 0.0)
        yb_o[c * CHUNK:(c + 1) * CHUNK, :] = (u[c * CHUNK:(c + 1) * CHUNK, :] * mixed).astype(BF16)

    bd = mm("bd")
    z = mm("cd") * mm("xd")

    @pl.when(pl.program_id(0) % steps_per_seq == 0)
    def _():
        zbuf[0:CONV_HALO, :] = jnp.zeros((CONV_HALO, BRANCH_WIDTH), F32)

    zbuf[CONV_HALO:CONV_HALO + rows, :] = z
    y = cw_ref[CONV_WIDTH - 1:CONV_WIDTH, :] * z
    for tap in range(CONV_WIDTH - 1):
        shift = CONV_WIDTH - 1 - tap
        y = y + cw_ref[tap:tap + 1, :] * zbuf[CONV_HALO - shift:CONV_HALO - shift + rows, :]
    yd_o[...] = (bd * y).astype(BF16)
    zbuf[0:CONV_HALO, :] = z[rows - CONV_HALO:rows, :]


def _swap_halves(w, dim):
    d_in, cols = w.shape
    wh = w.reshape(d_in, cols // dim, 2, dim // 2)
    return jnp.stack([-wh[:, :, 1], wh[:, :, 0]], axis=2).reshape(d_in, cols)


def _proj_weight(w_in):
    d = w_in.shape[0]
    sizes = (BRANCH_WIDTH,) * 3 + (N_HEADS * IDX_DIM, IDX_DIM, N_HEADS) + (BRANCH_WIDTH,) * 8
    parts, o = [], 0
    for s in sizes:
        parts.append(w_in[:, o:o + s])
        o += s
    qa, ka, va, iq, ik, iw, ub, vb, qc, kc, vc, bd, cd, xd = parts
    ik_rep = jnp.tile(ik, (1, N_HEADS))
    cols = [qa, _swap_halves(qa, HEAD_DIM), ka, _swap_halves(ka, HEAD_DIM),
            iq, _swap_halves(iq, IDX_DIM), ik_rep, _swap_halves(ik_rep, IDX_DIM),
            ub, vb, qc, kc, bd, cd, xd]
    iw_pad = jnp.concatenate([iw, jnp.zeros((d, W_T_ROWS - 2 * BRANCH_WIDTH - N_HEADS), w_in.dtype)], axis=1)
    w_t = jnp.concatenate([va, vc, iw_pad], axis=1).T
    return jnp.concatenate(cols, axis=1).astype(BF16), w_t.astype(BF16), o


def _project(x2, norm_g, w_a, w_t, tables, ln_g, ln_b, w_spatial, b_spatial, conv_w, seq):
    n, d = x2.shape
    rows = PROJ_ROWS
    cos_a, sin_a, cos_i, sin_i = tables
    bias = jnp.repeat(b_spatial.T, BRANCH_WIDTH // N_GROUPS_B, axis=1)
    row = lambda w: pl.BlockSpec((rows, w), lambda i: (i, 0))
    col = lambda r: pl.BlockSpec((r, rows), lambda i: (0, i))
    full = lambda a: pl.BlockSpec(a.shape, lambda i: (0,) * a.ndim)
    bf = lambda w: jax.ShapeDtypeStruct((n, w), BF16)
    bft = jax.ShapeDtypeStruct((BRANCH_WIDTH, n), BF16)
    args = (x2, norm_g.reshape(1, d), w_a, w_t, cos_a, sin_a, cos_i, sin_i, ln_g.reshape(1, -1),
            ln_b.reshape(1, -1), w_spatial, bias, conv_w)
    in_specs = [row(d), full(args[1]), full(w_a), full(w_t), row(LANES), row(LANES), row(LANES),
                row(LANES), full(args[8]), full(args[9]), full(w_spatial), full(bias), full(conv_w)]
    out_shape = [bf(256), bf(256), bft, bf(128), bf(128), jax.ShapeDtypeStruct((SUBLANES, n), F32),
                 bf(256), bf(256), bf(256), bft, bf(256)]
    out_specs = [row(256), row(256), col(BRANCH_WIDTH), row(128), row(128), col(SUBLANES),
                 row(256), row(256), row(256), col(BRANCH_WIDTH), row(256)]
    return pl.pallas_call(
        functools.partial(_proj_kernel, steps_per_seq=seq // rows),
        grid=(n // rows,),
        in_specs=in_specs,
        out_specs=out_specs,
        out_shape=out_shape,
        scratch_shapes=[pltpu.VMEM((CONV_HALO + rows, BRANCH_WIDTH), F32)],
        compiler_params=_params("arbitrary"),
        name="project",
    )(*args)


def _head_lane_mask(width, per_head, h, rows=1):
    lane = lax.broadcasted_iota(jnp.int32, (rows, width), 1)
    return (lane >= h * per_head) & (lane < (h + 1) * per_head)


def _masked_heads(x, per_head):
    return [jnp.where(_head_lane_mask(x.shape[1], per_head, h), x, jnp.zeros_like(x))
            for h in range(N_HEADS)]


def _float_to_key(x):
    bits = lax.bitcast_convert_type(x, jnp.int32)
    sign = bits >> 31
    return (bits ^ (sign & jnp.int32(0x7FFFFFFF))) - sign


def _key_to_float(key):
    bits = jnp.where(key < 0, (-key) | jnp.int32(INT_MIN), key)
    return lax.bitcast_convert_type(bits, F32)


def _col_reduce(x, op, final):
    while x.shape[0] > SUBLANES:
        half = x.shape[0] // 2
        x = op(x[:half], x[half:])
    return final(x, axis=0, keepdims=True)


def _col_sum(x):
    return _col_reduce(x, jnp.add, jnp.sum)


def _col_max(x):
    return _col_reduce(x, jnp.maximum, jnp.max)


def _step_starts(step):
    return [pl.multiple_of((step * TILES_PER_STEP + g) * K_TILE, K_TILE) for g in range(TILES_PER_STEP)]


def _dsa_kernel(iq_ref, iwt_ref, q_ref, ik_ref, k_ref, vt_ref, o_ref, sc_ref, hi_ref, lo_ref, *, topk):
    qb = q_ref.shape[0]
    i = pl.program_id(1)
    n_tiles = i + 1
    n_steps = (n_tiles + TILES_PER_STEP - 1) // TILES_PER_STEP

    @pl.when(i == 0)
    def _():
        sc_ref[...] = jnp.full(sc_ref.shape, -jnp.inf, F32)
        hi_ref[...] = jnp.full(hi_ref.shape, -HALF_RANGE, jnp.int16)
        lo_ref[...] = jnp.full(lo_ref.shape, -HALF_RANGE, jnp.int16)

    iq_heads = _masked_heads(iq_ref[...], IDX_DIM)
    iwt = iwt_ref[...]
    iw_heads = [iwt[h:h + 1, :] for h in range(N_HEADS)]

    def put_scores(start, sc):
        sc_ref[pl.ds(start, K_TILE), :] = sc
        key = _float_to_key(sc)
        hi_ref[pl.ds(start, K_TILE), :] = (key >> 16).astype(jnp.int16)
        lo_ref[pl.ds(start, K_TILE), :] = ((key & 0xFFFF) - HALF_RANGE).astype(jnp.int16)

    key_off = lax.broadcasted_iota(jnp.int32, (K_TILE, qb), 0)
    qry_pos = i * qb + lax.broadcasted_iota(jnp.int32, (K_TILE, qb), 1)

    def score_step(jj, masked):
        starts = _step_starts(jj)
        dots = [[lax.dot_general(ik_ref[pl.ds(st, K_TILE), :], iq_heads[h], _CONTRACT_LAST,
                                 preferred_element_type=F32) for h in range(N_HEADS)] for st in starts]
        for g in range(TILES_PER_STEP):
            sc = None
            for h in range(N_HEADS):
                t = iw_heads[h] * jnp.maximum(dots[g][h], 0.0)
                sc = t if sc is None else sc + t
            if masked:
                sc = jnp.where(starts[g] + key_off <= qry_pos, sc, -jnp.inf)
            put_scores(starts[g], sc)

    def full_step(jj, carry):
        score_step(jj, False)
        return carry

    lax.fori_loop(0, i // TILES_PER_STEP, full_step, 0)
    score_step(i // TILES_PER_STEP, True)

    n_chunks = (n_tiles * K_TILE + COUNT_CHUNK - 1) // COUNT_CHUNK
    kf = jnp.float32(topk)

    def chunk_rows(c):
        base = pl.multiple_of(c * COUNT_CHUNK, COUNT_CHUNK)
        return [pl.ds(base + r * COUNT_ROWS, COUNT_ROWS) for r in range(COUNT_CHUNK // COUNT_ROWS)]

    def count(ref, pred, dtype):
        one, zero = jnp.ones((), dtype), jnp.zeros((), dtype)

        def body(c, acc):
            for rows in chunk_rows(c):
                acc = acc + jnp.where(pred(ref[rows, :]), one, zero)
            return acc
        acc = lax.fori_loop(0, n_chunks, body, jnp.zeros((COUNT_ROWS, qb), dtype))
        return jnp.sum(acc.astype(F32), axis=0, keepdims=True)

    def bisect16(ref, target):
        def bit_step(b, carry):
            best, cnt_best = carry
            cand = best + lax.shift_left(jnp.int32(1), 15 - b)
            cand16 = cand.astype(jnp.int16)
            cnt = count(ref, lambda blk: blk >= cand16, jnp.int16)
            ok = cnt >= target
            return jnp.where(ok, cand, best), jnp.where(ok, cnt, cnt_best)
        init = (jnp.full((1, qb), -HALF_RANGE, jnp.int32), jnp.zeros((1, qb), F32))
        return lax.fori_loop(0, 16, bit_step, init)

    hi_t, cnt_hi = bisect16(hi_ref, kf)
    hi_t16 = hi_t.astype(jnp.int16)
    above = count(hi_ref, lambda blk: blk > hi_t16, jnp.int16)

    def mask_low(c, carry):
        for rows in chunk_rows(c):
            lo_ref[rows, :] = jnp.where(hi_ref[rows, :] == hi_t16, lo_ref[rows, :], jnp.int16(-HALF_RANGE))
        return carry

    lax.fori_loop(0, n_chunks, mask_low, 0)
    lo_t, cnt_lo = bisect16(lo_ref, kf - above)
    cnt_key = above + jnp.where(lo_t == -HALF_RANGE, cnt_hi - above, cnt_lo)
    key = hi_t * (2 * HALF_RANGE) + (lo_t + HALF_RANGE)
    few = i * qb + lax.broadcasted_iota(jnp.int32, (1, qb), 1) + 1 < topk
    thr = jnp.where(few, jnp.finfo(F32).min, _key_to_float(key))

    @pl.when(jnp.max(jnp.where(few, kf, cnt_key)) > kf)
    def _():
        need = kf - count(sc_ref, lambda blk: blk > thr, F32)
        before = (lax.broadcasted_iota(jnp.int32, (K_TILE, K_TILE), 1)
                  < lax.broadcasted_iota(jnp.int32, (K_TILE, K_TILE), 0))
        before = jnp.where(before, 1.0, 0.0).astype(BF16)

        def tie_body(jj, seen):
            starts = _step_starts(jj)
            blks = [sc_ref[pl.ds(st, K_TILE), :] for st in starts]
            eqs = [blk == thr for blk in blks]
            eq_fs = [jnp.where(eq, 1.0, 0.0) for eq in eqs]
            ranks = [jnp.dot(before, eq_f.astype(BF16), preferred_element_type=F32) for eq_f in eq_fs]
            for g in range(TILES_PER_STEP):
                drop = eqs[g] & (seen + ranks[g] >= need)
                sc_ref[pl.ds(starts[g], K_TILE), :] = jnp.where(drop, -jnp.inf, blks[g])
                seen = seen + _col_sum(eq_fs[g])
            return seen

        lax.fori_loop(0, n_steps, tie_body, jnp.zeros((1, qb), F32))

    q_heads = _masked_heads(q_ref[...], HEAD_DIM)

    def att_step(jj, carry):
        m, l, acc = carry
        starts = _step_starts(jj)
        sels = [sc_ref[pl.ds(st, K_TILE), :] >= thr for st in starts]
        logits = [[lax.dot_general(k_ref[pl.ds(st, K_TILE), :], q_heads[h], _CONTRACT_LAST,
                                   preferred_element_type=F32) for h in range(N_HEADS)] for st in starts]
        m_new, l_new, alphas, probs = [], [], [], []
        for h in range(N_HEADS):
            ss = [jnp.where(sels[g], logits[g][h], NEG_BIG) for g in range(TILES_PER_STEP)]
            mh = m[h]
            for s in ss:
                mh = jnp.maximum(mh, _col_max(s))
            ps = [jnp.exp(s - mh) for s in ss]
            alpha = jnp.exp(m[h] - mh)
            lh = alpha * l[h]
            for p in ps:
                lh = lh + _col_sum(p)
            m_new.append(mh)
            alphas.append(alpha)
            l_new.append(lh)
            probs.append([p.astype(BF16) for p in ps])
        acc_new = []
        for h in range(N_HEADS):
            pv = None
            for g, st in enumerate(starts):
                vt = vt_ref[h * HEAD_DIM:(h + 1) * HEAD_DIM, pl.ds(st, K_TILE)]
                t = jnp.dot(vt, probs[h][g], preferred_element_type=F32)
                pv = t if pv is None else pv + t
            acc_new.append(alphas[h] * acc[h] + pv)
        return tuple(m_new), tuple(l_new), tuple(acc_new)

    init = (tuple(jnp.full((1, qb), NEG_BIG, F32) for _ in range(N_HEADS)),
            tuple(jnp.zeros((1, qb), F32) for _ in range(N_HEADS)),
            tuple(jnp.zeros((HEAD_DIM, qb), F32) for _ in range(N_HEADS)))
    _, l, acc = lax.fori_loop(0, n_steps, att_step, init)
    out_t = jnp.concatenate([acc[h] / l[h] for h in range(N_HEADS)], axis=0)
    o_ref[...] = out_t.T.astype(o_ref.dtype)


def _dsa(iq, iwt, qa, ik, ka, vat, batch, seq):
    n = qa.shape[0]
    nq = seq // Q_BLOCK
    topk = min(TOPK_MAX, seq // 4)
    qrow = lambda w: pl.BlockSpec((Q_BLOCK, w), lambda b, i: (b * nq + i, 0))
    whole = lambda w: pl.BlockSpec((seq, w), lambda b, i: (b, 0))
    return pl.pallas_call(
        functools.partial(_dsa_kernel, topk=topk),
        grid=(batch, nq),
        in_specs=[qrow(LANES), pl.BlockSpec((SUBLANES, Q_BLOCK), lambda b, i: (0, b * nq + i)),
                  qrow(BRANCH_WIDTH), whole(LANES), whole(BRANCH_WIDTH),
                  pl.BlockSpec((BRANCH_WIDTH, seq), lambda b, i: (0, b))],
        out_specs=qrow(BRANCH_WIDTH),
        out_shape=jax.ShapeDtypeStruct((n, BRANCH_WIDTH), BF16),
        scratch_shapes=[pltpu.VMEM((seq, Q_BLOCK), F32), pltpu.VMEM((seq, Q_BLOCK), jnp.int16),
                        pltpu.VMEM((seq, Q_BLOCK), jnp.int16)],
        compiler_params=_params("arbitrary", "arbitrary"),
        name="dsa",
    )(iq, iwt, qa, ik, ka, vat)


def _stick_kernel(q_ref, k_ref, vt_ref, o_ref):
    qb = q_ref.shape[0]
    i = pl.program_id(1)
    q_heads = _masked_heads(q_ref[...], HEAD_DIM)
    after = (lax.broadcasted_iota(jnp.int32, (K_TILE, K_TILE), 1)
             > lax.broadcasted_iota(jnp.int32, (K_TILE, K_TILE), 0))
    after = jnp.where(after, 1.0, 0.0).astype(BF16)
    after = jnp.concatenate([after, jnp.ones((ONES_ROWS, K_TILE), BF16)], axis=0)
    key_off = lax.broadcasted_iota(jnp.int32, (K_TILE, qb), 0)
    qry_pos = i * qb + lax.broadcasted_iota(jnp.int32, (K_TILE, qb), 1)

    def step(u, carry, masked):
        tails, acc = carry
        first = i - 2 * u
        tiles = [first, jnp.maximum(first - 1, 0)]
        bias = [None, jnp.where(first >= 1, 0.0, PHANTOM_LOGIT)]
        starts = [pl.multiple_of(t * K_TILE, K_TILE) for t in tiles]
        stricts = [starts[g] + key_off < qry_pos if masked else None for g in range(2)]
        zs = [[lax.dot_general(k_ref[pl.ds(st, K_TILE), :], q_heads[h], _CONTRACT_LAST,
                               preferred_element_type=F32) for h in range(N_HEADS)] for st in starts]
        log_betas, splits = [], []
        for g in range(2):
            lbs, sps = [], []
            for h in range(N_HEADS):
                z = zs[g][h] if bias[g] is None else zs[g][h] + bias[g]
                log_beta = jnp.minimum(z, 0.0) - jnp.log(1.0 + jnp.exp(-jnp.abs(z)))
                log_keep = log_beta - z
                if masked:
                    log_keep = jnp.where(stricts[g], log_keep, 0.0)
                lbs.append(log_beta)
                sps.append(log_keep.astype(BF16))
            log_betas.append(lbs)
            splits.append(sps)
        sums = [[jnp.dot(after, splits[g][h], preferred_element_type=F32) for h in range(N_HEADS)]
                for g in range(2)]
        tails = list(tails)
        new_acc = []
        for h in range(N_HEADS):
            total = acc[h]
            for g in range(2):
                a = jnp.exp(log_betas[g][h] + sums[g][h][:K_TILE] + tails[h])
                if masked:
                    a = jnp.where(stricts[g], a, 0.0)
                tails[h] = tails[h] + sums[g][h][K_TILE:K_TILE + 1]
                vt = vt_ref[h * HEAD_DIM:(h + 1) * HEAD_DIM, pl.ds(starts[g], K_TILE)]
                total = total + jnp.dot(vt, a.astype(BF16), preferred_element_type=F32)
            new_acc.append(total)
        return tuple(tails), tuple(new_acc)

    init = (tuple(jnp.zeros((1, qb), F32) for _ in range(N_HEADS)),
            tuple(jnp.zeros((HEAD_DIM, qb), F32) for _ in range(N_HEADS)))
    carry = step(0, init, True)
    _, acc = lax.fori_loop(1, (i + 2) // 2, lambda u, c: step(u, c, False), carry)
    o_ref[...] = jnp.concatenate(acc, axis=0).T.astype(o_ref.dtype)


def _stick_breaking(qc, kc, vct, batch, seq):
    n = qc.shape[0]
    nq = seq // Q_BLOCK
    qrow = pl.BlockSpec((Q_BLOCK, BRANCH_WIDTH), lambda b, i: (b * nq + i, 0))
    return pl.pallas_call(
        _stick_kernel,
        grid=(batch, nq),
        in_specs=[qrow, pl.BlockSpec((seq, BRANCH_WIDTH), lambda b, i: (b, 0)),
                  pl.BlockSpec((BRANCH_WIDTH, seq), lambda b, i: (0, b))],
        out_specs=qrow,
        out_shape=jax.ShapeDtypeStruct((n, BRANCH_WIDTH), BF16),
        compiler_params=_params("arbitrary", "arbitrary"),
        name="stick_breaking",
    )(qc, kc, vct)


def _merge_kernel(x_ref, g1_ref, ya_ref, yb_ref, yc_ref, yd_ref, wg_ref, wb_ref, wo_ref,
                  g2_ref, wr_ref, br_ref, x1_o, comb_o):
    x = x_ref[...]
    d = x.shape[1]
    h = _rms_bf16(x, g1_ref[...])
    ys = (ya_ref[...], yb_ref[...], yc_ref[...], yd_ref[...])
    col = 256
    pieces = []
    for c in range(d // col):
        m = None
        for n in range(N_BRANCH):
            logit = jnp.dot(h, wg_ref[:, n * d + c * col:n * d + (c + 1) * col],
                            preferred_element_type=F32)
            br = jnp.dot(ys[n], wb_ref[n, :, c * col:(c + 1) * col], preferred_element_type=F32)
            t = jax.nn.sigmoid(logit) * br
            m = t if m is None else m + t
        pieces.append(m.astype(BF16))
    merged = jnp.concatenate(pieces, axis=1)
    x1 = x + jnp.dot(merged, wo_ref[...], preferred_element_type=F32)
    x1_o[...] = x1

    ms = jnp.mean(x1 * x1, axis=-1, keepdims=True)
    h2 = x1 * lax.rsqrt(ms + EPS) * g2_ref[...]
    h_hi = h2.astype(BF16)
    h_lo = (h2 - h_hi.astype(F32)).astype(BF16)
    both = jnp.dot(h_hi, wr_ref[...], preferred_element_type=F32)
    cross = jnp.dot(h_lo, wr_ref[:, 0:LANES], preferred_element_type=F32)
    logits = both[:, 0:LANES] + both[:, LANES:2 * LANES] + cross + br_ref[...]
    rows = logits.shape[0]
    lane = lax.broadcasted_iota(jnp.int32, (rows, LANES), 1)
    neg = -jnp.inf
    is_grp = (lane >= N_EXPERTS) & (lane < N_EXPERTS + N_EXPERT_GROUPS)
    gl = jnp.where(is_grp, logits, neg)
    gmax = jnp.max(gl, axis=1, keepdims=True)
    gidx = jnp.min(jnp.where(gl == gmax, lane, LANES), axis=1, keepdims=True) - N_EXPERTS
    grp_p = 1.0 / jnp.sum(jnp.where(is_grp, jnp.exp(gl - gmax), 0.0), axis=1, keepdims=True)
    in_grp = (lane >= gidx * EXPERTS_PER_GROUP) & (lane < (gidx + 1) * EXPERTS_PER_GROUP)
    el = jnp.where(in_grp, logits, neg)
    l1 = jnp.max(el, axis=1, keepdims=True)
    j1 = jnp.min(jnp.where(el == l1, lane, LANES), axis=1, keepdims=True)
    el2 = jnp.where(lane == j1, neg, el)
    l2 = jnp.max(el2, axis=1, keepdims=True)
    j2 = jnp.min(jnp.where(el2 == l2, lane, LANES), axis=1, keepdims=True)
    e2 = jnp.exp(l2 - l1)
    w1 = grp_p / (1.0 + e2)
    w2 = grp_p * e2 / (1.0 + e2)
    comb_o[...] = jnp.where(lane == j1, w1, 0.0) + jnp.where(lane == j2, w2, 0.0)


def _merge(x2, norm1_g, ys, w_gate, w_branch, w_out, norm2_g, w_router, b_router):
    n, d = x2.shape
    rows = MERGE_ROWS
    row = lambda w: pl.BlockSpec((rows, w), lambda i: (i, 0))
    full = lambda a: pl.BlockSpec(a.shape, lambda i: (0,) * a.ndim)
    args = (x2, norm1_g.reshape(1, d), *ys, w_gate, w_branch, w_out, norm2_g.reshape(1, d),
            w_router, b_router)
    in_specs = [row(d), full(args[1])] + [row(BRANCH_WIDTH)] * 4 + [full(a) for a in args[6:]]
    return pl.pallas_call(
        _merge_kernel,
        grid=(n // rows,),
        in_specs=in_specs,
        out_specs=[row(d), row(LANES)],
        out_shape=[jax.ShapeDtypeStruct((n, d), F32), jax.ShapeDtypeStruct((n, LANES), F32)],
        compiler_params=_params("arbitrary"),
        name="merge_router",
    )(*args)


def _moe_kernel(x_ref, g2_ref, comb_ref, wgu_ref, wd_ref, gf_ref, o_ref, h_scr, acc_scr, *, final_norm):
    step = pl.program_id(1)

    @pl.when(step == 0)
    def _():
        h_scr[...] = _rms_bf16(x_ref[...], g2_ref[...])
        acc_scr[...] = jnp.zeros(acc_scr.shape, F32)

    h = h_scr[...]
    comb = comb_ref[...]
    lane = lax.broadcasted_iota(jnp.int32, comb.shape, 1)
    total = None
    for j in range(EXPERTS_PER_STEP):
        e = step * EXPERTS_PER_STEP + j
        gu = jnp.dot(h, wgu_ref[j], preferred_element_type=F32)
        w = jnp.sum(jnp.where(lane == e, comb, 0.0), axis=1, keepdims=True)
        hh = (jax.nn.silu(gu[:, :D_EXPERT]) * gu[:, D_EXPERT:] * w).astype(BF16)
        t = jnp.dot(hh, wd_ref[j], preferred_element_type=F32)
        total = t if total is None else total + t
    acc_scr[...] += total

    @pl.when(step == pl.num_programs(1) - 1)
    def _():
        y = x_ref[...] + acc_scr[...]
        if final_norm:
            ms = jnp.mean(y * y, axis=-1, keepdims=True)
            y = y * lax.rsqrt(ms + EPS) * gf_ref[...]
        o_ref[...] = y


def _moe(x1, norm2_g, comb, w_gu, w_d, norm_f_g, final_norm):
    n, d = x1.shape
    rows = min(MOE_ROWS, n)
    n_exp = w_gu.shape[0]
    row = lambda w: pl.BlockSpec((rows, w), lambda i, e: (i, 0))
    vec = pl.BlockSpec((1, d), lambda i, e: (0, 0))
    return pl.pallas_call(
        functools.partial(_moe_kernel, final_norm=final_norm),
        grid=(n // rows, n_exp // EXPERTS_PER_STEP),
        in_specs=[row(d), vec, row(LANES),
                  pl.BlockSpec((EXPERTS_PER_STEP, d, 2 * D_EXPERT), lambda i, e: (e, 0, 0)),
                  pl.BlockSpec((EXPERTS_PER_STEP, D_EXPERT, d), lambda i, e: (e, 0, 0)),
                  vec],
        out_specs=row(d),
        out_shape=jax.ShapeDtypeStruct((n, d), F32),
        scratch_shapes=[pltpu.VMEM((rows, d), BF16), pltpu.VMEM((rows, d), F32)],
        compiler_params=_params("arbitrary", "arbitrary"),
        name="experts",
    )(x1, norm2_g.reshape(1, d), comb, w_gu, w_d, norm_f_g.reshape(1, d))


def kernel(x, positions, norm1_g, w_in, sgu_ln_g, sgu_ln_b, w_spatial, b_spatial, conv_w, w_branch,
           w_out, norm2_g, w_router_group, b_router_group, w_router_expert, b_router_expert,
           w_exp_gate, w_exp_up, w_exp_down, norm_f_g):
    batch, seq, d = x.shape
    depth = w_in.shape[0]
    n = batch * seq
    assert TILES_PER_STEP == 2
    assert seq % COUNT_CHUNK == 0 and seq % (TILES_PER_STEP * K_TILE) == 0
    assert n % PROJ_ROWS == 0 and n % MERGE_ROWS == 0
    x2 = x.reshape(n, d)
    tables = _rope_tables(positions)
    for l in range(depth):
        w_a, w_t, gate_off = _proj_weight(w_in[l])
        w_gate = w_in[l][:, gate_off:].astype(BF16)
        pad = jnp.zeros((d, LANES - N_EXPERTS - N_EXPERT_GROUPS), F32)
        w_router = jnp.concatenate([w_router_expert[l], w_router_group[l], pad], axis=1)
        w_router_hi = w_router.astype(BF16)
        w_router_lo = (w_router - w_router_hi.astype(F32)).astype(BF16)
        w_router = jnp.concatenate([w_router_hi, w_router_lo], axis=1)
        b_router = jnp.concatenate([b_router_expert[l], b_router_group[l], pad[0]]).reshape(1, LANES)
        qa, ka, vat, iq, ik, iwt, y_b, qc, kc, vct, y_d = _project(
            x2, norm1_g[l], w_a, w_t, tables, sgu_ln_g[l], sgu_ln_b[l], w_spatial[l], b_spatial[l],
            conv_w[l], seq)
        y_a = _dsa(iq, iwt, qa, ik, ka, vat, batch, seq)
        y_c = _stick_breaking(qc, kc, vct, batch, seq)
        x1, comb = _merge(x2, norm1_g[l], (y_a, y_b, y_c, y_d), w_gate, w_branch[l].astype(BF16),
                          w_out[l].astype(BF16), norm2_g[l], w_router, b_router)
        w_gu = jnp.concatenate([w_exp_gate[l], w_exp_up[l]], axis=2).astype(BF16)
        x2 = _moe(x1, norm2_g[l], comb, w_gu, w_exp_down[l].astype(BF16), norm_f_g,
                  final_norm=(l == depth - 1))
    return x2.reshape(batch, seq, d)
```

```python
import functools

import jax
import jax.numpy as jnp
from jax import lax
from jax.experimental import pallas as pl
from jax.experimental.pallas import tpu as pltpu

F32 = jnp.float32
BF16 = jnp.bfloat16

EPS = 1e-6
ROPE_THETA = 10000.0
HEAD_DIM = 64
BRANCH_WIDTH = 256
N_HEADS = 4
IDX_DIM = 32
TOPK_MAX = 256
CHUNK = 128
N_GROUPS_B = 4
CONV_WIDTH = 3
N_BRANCH = 4
N_EXPERT_GROUPS = 4
EXPERTS_PER_GROUP = 8
N_EXPERTS = N_EXPERT_GROUPS * EXPERTS_PER_GROUP
D_EXPERT = 256

LANES = 128
SUBLANES = 8
Q_BLOCK = 256
K_TILE = Q_BLOCK
TILES_PER_STEP = 2
COUNT_CHUNK = 512
COUNT_ROWS = 64
PROJ_ROWS = 256
MERGE_ROWS = 512
MOE_ROWS = 1024
EXPERTS_PER_STEP = 4
CONV_HALO = 8
VMEM_LIMIT = 56 * 1024 * 1024
PHANTOM_LOGIT = -1e4
NEG_BIG = -1e30
INT_MIN = -2 ** 31
HALF_RANGE = 2 ** 15
ONES_ROWS = 16
BOUND_MARGIN = 1.02
BOUND_LIMIT = 40.0

_OFF = {}
_o = 0
for _name, _w in (("q", 512), ("k", 512), ("iq", 256), ("ik", 256),
                  ("ub", 256), ("vb", 256), ("qc", 256), ("kc", 256),
                  ("bd", 256), ("cd", 256), ("xd", 256)):
    _OFF[_name] = (_o, _w)
    _o += _w
W_A_COLS = _o
T_VA, T_VC, T_IW = 0, BRANCH_WIDTH, 2 * BRANCH_WIDTH
W_T_ROWS = 2 * BRANCH_WIDTH + 16

_CONTRACT_LAST = (((1,), (1,)), ((), ()))


def _params(*sem):
    return pltpu.CompilerParams(dimension_semantics=sem, vmem_limit_bytes=VMEM_LIMIT)


def _rope_table_kernel(pos_ref, inv_a_ref, inv_i_ref, cos_a, sin_a, cos_i, sin_i):
    p = pos_ref[...]
    a = p * inv_a_ref[...]
    cos_a[...] = jnp.cos(a)
    sin_a[...] = jnp.sin(a)
    b = p * inv_i_ref[...]
    cos_i[...] = jnp.cos(b)
    sin_i[...] = jnp.sin(b)


def _rope_tables(positions):
    n = positions.size
    rows = PROJ_ROWS
    pos = positions.astype(F32).reshape(n, 1)
    inv_a = ROPE_THETA ** (-jnp.arange(0, HEAD_DIM, 2, dtype=F32) / HEAD_DIM)
    inv_i = ROPE_THETA ** (-jnp.arange(0, IDX_DIM, 2, dtype=F32) / IDX_DIM)
    inv_a = jnp.tile(inv_a, LANES // inv_a.size).reshape(1, LANES)
    inv_i = jnp.tile(inv_i, LANES // inv_i.size).reshape(1, LANES)
    tab = jax.ShapeDtypeStruct((n, LANES), F32)
    row_spec = pl.BlockSpec((rows, LANES), lambda i: (i, 0))
    const = pl.BlockSpec((1, LANES), lambda i: (0, 0))
    return pl.pallas_call(
        _rope_table_kernel,
        grid=(n // rows,),
        in_specs=[pl.BlockSpec((rows, 1), lambda i: (i, 0)), const, const],
        out_specs=[row_spec] * 4,
        out_shape=[tab] * 4,
        compiler_params=_params("arbitrary"),
        name="rope_tables",
    )(pos, inv_a, inv_i)


def _rms_bf16(x, g):
    ms = jnp.mean(x * x, axis=-1, keepdims=True)
    return (x * lax.rsqrt(ms + EPS) * g).astype(BF16)


def _proj_kernel(x_ref, g_ref, w_ref, wt_ref, cos_a, sin_a, cos_i, sin_i, lng_ref, lnb_ref, ws_ref,
                 bs_ref, cw_ref,
                 qa_o, ka_o, vat_o, iq_o, ik_o, iwt_o, yb_o, qc_o, kc_o, vct_o, yd_o,
                 zbuf, *, steps_per_seq):
    rows = x_ref.shape[0]
    h = _rms_bf16(x_ref[...], g_ref[...])

    def mm(name):
        off, width = _OFF[name]
        return jnp.dot(h, w_ref[:, off:off + width], preferred_element_type=F32)

    def rope(name, cos_ref, sin_ref, out_ref, scale):
        p = mm(name)
        w = p.shape[1] // 2
        c = cos_ref[...]
        s = sin_ref[...]
        for j in range(w // LANES):
            lo = j * LANES
            r = p[:, lo:lo + LANES] * c + p[:, w + lo:w + lo + LANES] * s
            out_ref[:, lo:lo + LANES] = (r if scale == 1.0 else r * scale).astype(out_ref.dtype)

    att_scale = HEAD_DIM ** -0.5
    rope("q", cos_a, sin_a, qa_o, att_scale)
    rope("k", cos_a, sin_a, ka_o, 1.0)
    rope("iq", cos_i, sin_i, iq_o, 1.0)
    rope("ik", cos_i, sin_i, ik_o, 1.0)
    qc_o[...] = (mm("qc") * att_scale).astype(BF16)
    kc_o[...] = mm("kc").astype(BF16)
    t = lax.dot_general(wt_ref[...], h, _CONTRACT_LAST, preferred_element_type=F32)
    vat_o[...] = t[T_VA:T_VA + BRANCH_WIDTH, :].astype(BF16)
    vct_o[...] = t[T_VC:T_VC + BRANCH_WIDTH, :].astype(BF16)
    iwt_o[...] = t[T_IW:T_IW + SUBLANES, :]

    u = jax.nn.gelu(mm("ub"))
    v = jax.nn.gelu(mm("vb"))
    mu = jnp.mean(v, axis=-1, keepdims=True)
    var = jnp.mean(jnp.square(v - mu), axis=-1, keepdims=True)
    vn = ((v - mu) * lax.rsqrt(var + EPS) * lng_ref[...] + lnb_ref[...]).astype(BF16)
    t_idx = lax.broadcasted_iota(jnp.int32, (CHUNK, CHUNK), 0)
    s_idx = lax.broadcasted_iota(jnp.int32, (CHUNK, CHUNK), 1)
    causal = s_idx <= t_idx
    group_masks = [_head_lane_mask(BRANCH_WIDTH, BRANCH_WIDTH // N_GROUPS_B, g) for g in range(N_GROUPS_B)]
    w_tril = [jnp.where(causal, ws_ref[g], 0.0).astype(BF16) for g in range(N_GROUPS_B)]
    for c in range(rows // CHUNK):
        vc = vn[c * CHUNK:(c + 1) * CHUNK, :]
        mixed = bs_ref[...]
        for g in range(N_GROUPS_B):
            full = jnp.dot(w_tril[g], vc, preferred_element_type=F32)
            mixed = mixed + jnp.where(group_masks[g], full, 0.0)
        yb_o[c * CHUNK:(c + 1) * CHUNK, :] = (u[c * CHUNK:(c + 1) * CHUNK, :] * mixed).astype(BF16)

    bd = mm("bd")
    z = mm("cd") * mm("xd")

    @pl.when(pl.program_id(0) % steps_per_seq == 0)
    def _():
        zbuf[0:CONV_HALO, :] = jnp.zeros((CONV_HALO, BRANCH_WIDTH), F32)

    zbuf[CONV_HALO:CONV_HALO + rows, :] = z
    y = cw_ref[CONV_WIDTH - 1:CONV_WIDTH, :] * z
    for tap in range(CONV_WIDTH - 1):
        shift = CONV_WIDTH - 1 - tap
        y = y + cw_ref[tap:tap + 1, :] * zbuf[CONV_HALO - shift:CONV_HALO - shift + rows, :]
    yd_o[...] = (bd * y).astype(BF16)
    zbuf[0:CONV_HALO, :] = z[rows - CONV_HALO:rows, :]


def _swap_halves(w, dim):
    d_in, cols = w.shape
    wh = w.reshape(d_in, cols // dim, 2, dim // 2)
    return jnp.stack([-wh[:, :, 1], wh[:, :, 0]], axis=2).reshape(d_in, cols)


def _proj_weight(w_in):
    d = w_in.shape[0]
    sizes = (BRANCH_WIDTH,) * 3 + (N_HEADS * IDX_DIM, IDX_DIM, N_HEADS) + (BRANCH_WIDTH,) * 8
    parts, o = [], 0
    for s in sizes:
        parts.append(w_in[:, o:o + s])
        o += s
    qa, ka, va, iq, ik, iw, ub, vb, qc, kc, vc, bd, cd, xd = parts
    ik_rep = jnp.tile(ik, (1, N_HEADS))
    cols = [qa, _swap_halves(qa, HEAD_DIM), ka, _swap_halves(ka, HEAD_DIM),
            iq, _swap_halves(iq, IDX_DIM), ik_rep, _swap_halves(ik_rep, IDX_DIM),
            ub, vb, qc, kc, bd, cd, xd]
    iw_pad = jnp.concatenate([iw, jnp.zeros((d, W_T_ROWS - 2 * BRANCH_WIDTH - N_HEADS), w_in.dtype)], axis=1)
    w_t = jnp.concatenate([va, vc, iw_pad], axis=1).T
    return jnp.concatenate(cols, axis=1).astype(BF16), w_t.astype(BF16), o


def _project(x2, norm_g, w_a, w_t, tables, ln_g, ln_b, w_spatial, b_spatial, conv_w, seq):
    n, d = x2.shape
    rows = PROJ_ROWS
    cos_a, sin_a, cos_i, sin_i = tables
    bias = jnp.repeat(b_spatial.T, BRANCH_WIDTH // N_GROUPS_B, axis=1)
    row = lambda w: pl.BlockSpec((rows, w), lambda i: (i, 0))
    col = lambda r: pl.BlockSpec((r, rows), lambda i: (0, i))
    full = lambda a: pl.BlockSpec(a.shape, lambda i: (0,) * a.ndim)
    bf = lambda w: jax.ShapeDtypeStruct((n, w), BF16)
    bft = jax.ShapeDtypeStruct((BRANCH_WIDTH, n), BF16)
    args = (x2, norm_g.reshape(1, d), w_a, w_t, cos_a, sin_a, cos_i, sin_i, ln_g.reshape(1, -1),
            ln_b.reshape(1, -1), w_spatial, bias, conv_w)
    in_specs = [row(d), full(args[1]), full(w_a), full(w_t), row(LANES), row(LANES), row(LANES),
                row(LANES), full(args[8]), full(args[9]), full(w_spatial), full(bias), full(conv_w)]
    out_shape = [bf(256), bf(256), bft, bf(128), bf(128), jax.ShapeDtypeStruct((SUBLANES, n), F32),
                 bf(256), bf(256), bf(256), bft, bf(256)]
    out_specs = [row(256), row(256), col(BRANCH_WIDTH), row(128), row(128), col(SUBLANES),
                 row(256), row(256), row(256), col(BRANCH_WIDTH), row(256)]
    return pl.pallas_call(
        functools.partial(_proj_kernel, steps_per_seq=seq // rows),
        grid=(n // rows,),
        in_specs=in_specs,
        out_specs=out_specs,
        out_shape=out_shape,
        scratch_shapes=[pltpu.VMEM((CONV_HALO + rows, BRANCH_WIDTH), F32)],
        compiler_params=_params("arbitrary"),
        name="project",
    )(*args)


def _head_lane_mask(width, per_head, h, rows=1):
    lane = lax.broadcasted_iota(jnp.int32, (rows, width), 1)
    return (lane >= h * per_head) & (lane < (h + 1) * per_head)


def _masked_heads(x, per_head):
    return [jnp.where(_head_lane_mask(x.shape[1], per_head, h), x, jnp.zeros_like(x))
            for h in range(N_HEADS)]


def _float_to_key(x):
    bits = lax.bitcast_convert_type(x, jnp.int32)
    sign = bits >> 31
    return (bits ^ (sign & jnp.int32(0x7FFFFFFF))) - sign


def _key_to_float(key):
    bits = jnp.where(key < 0, (-key) | jnp.int32(INT_MIN), key)
    return lax.bitcast_convert_type(bits, F32)


def _neg_abs(x):
    bits = lax.bitcast_convert_type(x, jnp.int32) | jnp.int32(INT_MIN)
    return lax.bitcast_convert_type(bits, F32)


def _col_reduce(x, op, final):
    while x.shape[0] > SUBLANES:
        half = x.shape[0] // 2
        x = op(x[:half], x[half:])
    return final(x, axis=0, keepdims=True)


def _col_sum(x):
    return _col_reduce(x, jnp.add, jnp.sum)


def _col_max(x):
    return _col_reduce(x, jnp.maximum, jnp.max)


def _step_starts(step):
    return [pl.multiple_of((step * TILES_PER_STEP + g) * K_TILE, K_TILE) for g in range(TILES_PER_STEP)]


def _dsa_kernel(iq_ref, iwt_ref, q_ref, ik_ref, k_ref, vt_ref, o_ref, sc_ref, hi_ref, lo_ref, kmax_ref, *,
                topk):
    qb = q_ref.shape[0]
    i = pl.program_id(1)
    n_tiles = i + 1
    n_steps = (n_tiles + TILES_PER_STEP - 1) // TILES_PER_STEP

    @pl.when(i == 0)
    def _():
        sc_ref[...] = jnp.full(sc_ref.shape, -jnp.inf, F32)
        hi_ref[...] = jnp.full(hi_ref.shape, -HALF_RANGE, jnp.int16)
        lo_ref[...] = jnp.full(lo_ref.shape, -HALF_RANGE, jnp.int16)

    iq_heads = _masked_heads(iq_ref[...], IDX_DIM)
    iwt = iwt_ref[...]
    iw_heads = [iwt[h:h + 1, :] for h in range(N_HEADS)]

    def put_scores(start, sc):
        sc_ref[pl.ds(start, K_TILE), :] = sc
        key = _float_to_key(sc)
        hi_ref[pl.ds(start, K_TILE), :] = (key >> 16).astype(jnp.int16)
        lo_ref[pl.ds(start, K_TILE), :] = ((key & 0xFFFF) - HALF_RANGE).astype(jnp.int16)

    key_off = lax.broadcasted_iota(jnp.int32, (K_TILE, qb), 0)
    qry_pos = i * qb + lax.broadcasted_iota(jnp.int32, (K_TILE, qb), 1)

    def score_step(jj, masked):
        starts = _step_starts(jj)
        dots = [[lax.dot_general(ik_ref[pl.ds(st, K_TILE), :], iq_heads[h], _CONTRACT_LAST,
                                 preferred_element_type=F32) for h in range(N_HEADS)] for st in starts]
        for g in range(TILES_PER_STEP):
            sc = None
            for h in range(N_HEADS):
                t = iw_heads[h] * jnp.maximum(dots[g][h], 0.0)
                sc = t if sc is None else sc + t
            if masked:
                sc = jnp.where(starts[g] + key_off <= qry_pos, sc, -jnp.inf)
            put_scores(starts[g], sc)

    def full_step(jj, carry):
        score_step(jj, False)
        return carry

    lax.fori_loop(0, i // TILES_PER_STEP, full_step, 0)
    score_step(i // TILES_PER_STEP, True)

    n_chunks = (n_tiles * K_TILE + COUNT_CHUNK - 1) // COUNT_CHUNK
    kf = jnp.float32(topk)

    def chunk_rows(c):
        base = pl.multiple_of(c * COUNT_CHUNK, COUNT_CHUNK)
        return [pl.ds(base + r * COUNT_ROWS, COUNT_ROWS) for r in range(COUNT_CHUNK // COUNT_ROWS)]

    def count(ref, pred, dtype):
        one, zero = jnp.ones((), dtype), jnp.zeros((), dtype)

        def body(c, acc):
            for rows in chunk_rows(c):
                acc = acc + jnp.where(pred(ref[rows, :]), one, zero)
            return acc
        acc = lax.fori_loop(0, n_chunks, body, jnp.zeros((COUNT_ROWS, qb), dtype))
        return jnp.sum(acc.astype(F32), axis=0, keepdims=True)

    def bisect16(ref, target):
        def bit_step(b, carry):
            best, cnt_best, cnt_above = carry
            cand = best + lax.shift_left(jnp.int32(1), 15 - b)
            cand16 = cand.astype(jnp.int16)
            cnt = count(ref, lambda blk: blk >= cand16, jnp.int16)
            ok = cnt >= target
            return jnp.where(ok, cand, best), jnp.where(ok, cnt, cnt_best), jnp.where(ok, cnt_above, cnt)
        init = (jnp.full((1, qb), -HALF_RANGE, jnp.int32), jnp.zeros((1, qb), F32), jnp.zeros((1, qb), F32))
        return lax.fori_loop(0, 16, bit_step, init)

    hi_t, cnt_hi, above = bisect16(hi_ref, kf)
    hi_t16 = hi_t.astype(jnp.int16)

    def mask_low(c, carry):
        for rows in chunk_rows(c):
            lo_ref[rows, :] = jnp.where(hi_ref[rows, :] == hi_t16, lo_ref[rows, :], jnp.int16(-HALF_RANGE))
        return carry

    lax.fori_loop(0, n_chunks, mask_low, 0)
    lo_t, cnt_lo, _ = bisect16(lo_ref, kf - above)
    cnt_key = above + jnp.where(lo_t == -HALF_RANGE, cnt_hi - above, cnt_lo)
    key = hi_t * (2 * HALF_RANGE) + (lo_t + HALF_RANGE)
    few = i * qb + lax.broadcasted_iota(jnp.int32, (1, qb), 1) + 1 < topk
    thr = jnp.where(few, jnp.finfo(F32).min, _key_to_float(key))

    @pl.when(jnp.max(jnp.where(few, kf, cnt_key)) > kf)
    def _():
        need = kf - count(sc_ref, lambda blk: blk > thr, F32)
        before = (lax.broadcasted_iota(jnp.int32, (K_TILE, K_TILE), 1)
                  < lax.broadcasted_iota(jnp.int32, (K_TILE, K_TILE), 0))
        before = jnp.where(before, 1.0, 0.0).astype(BF16)

        def tie_body(jj, seen):
            starts = _step_starts(jj)
            blks = [sc_ref[pl.ds(st, K_TILE), :] for st in starts]
            eqs = [blk == thr for blk in blks]
            eq_fs = [jnp.where(eq, 1.0, 0.0) for eq in eqs]
            ranks = [jnp.dot(before, eq_f.astype(BF16), preferred_element_type=F32) for eq_f in eq_fs]
            for g in range(TILES_PER_STEP):
                drop = eqs[g] & (seen + ranks[g] >= need)
                sc_ref[pl.ds(starts[g], K_TILE), :] = jnp.where(drop, -jnp.inf, blks[g])
                seen = seen + _col_sum(eq_fs[g])
            return seen

        lax.fori_loop(0, n_steps, tie_body, jnp.zeros((1, qb), F32))

    q = q_ref[...]
    q_heads = _masked_heads(q, HEAD_DIM)
    head_rows = jnp.concatenate(
        [jnp.where(_head_lane_mask(BRANCH_WIDTH, HEAD_DIM, h), 1.0, 0.0) for h in range(N_HEADS)]
        + [jnp.zeros((ONES_ROWS - N_HEADS, BRANCH_WIDTH), F32)], axis=0).astype(BF16)
    qf = q.astype(F32)
    q_sq = lax.dot_general(head_rows, (qf * qf).astype(BF16), _CONTRACT_LAST,
                           preferred_element_type=F32)
    kf32 = k_ref[pl.ds(pl.multiple_of(i * K_TILE, K_TILE), K_TILE), :].astype(F32)
    k_sq = lax.dot_general(head_rows, (kf32 * kf32).astype(BF16), _CONTRACT_LAST,
                           preferred_element_type=F32)
    k_sq_max = jnp.max(k_sq, axis=1, keepdims=True)

    @pl.when(i == 0)
    def _():
        kmax_ref[...] = jnp.zeros(kmax_ref.shape, F32)

    kmax_ref[...] = jnp.maximum(kmax_ref[...], jnp.broadcast_to(k_sq_max, kmax_ref.shape))
    bound = jnp.sqrt(q_sq * kmax_ref[:, 0:1]) * BOUND_MARGIN
    bounds = [bound[h:h + 1, :] for h in range(N_HEADS)]
    ones_rows = jnp.ones((ONES_ROWS, K_TILE), BF16)

    def step_logits(jj):
        starts = _step_starts(jj)
        sels = [sc_ref[pl.ds(st, K_TILE), :] >= thr for st in starts]
        logits = [[lax.dot_general(k_ref[pl.ds(st, K_TILE), :], q_heads[h], _CONTRACT_LAST,
                                   preferred_element_type=F32) for h in range(N_HEADS)] for st in starts]
        masked = [[jnp.where(sels[g], logits[g][h], NEG_BIG) for g in range(TILES_PER_STEP)]
                  for h in range(N_HEADS)]
        return starts, masked

    def value_products(starts, probs, h):
        pv = None
        for g, st in enumerate(starts):
            vt = vt_ref[h * HEAD_DIM:(h + 1) * HEAD_DIM, pl.ds(st, K_TILE)]
            t = jnp.dot(vt, probs[g], preferred_element_type=F32)
            pv = t if pv is None else pv + t
        return pv

    def bounded_step(jj, carry):
        l, acc = carry
        starts, masked = step_logits(jj)
        probs = [[jnp.exp(s - bounds[h]).astype(BF16) for s in masked[h]] for h in range(N_HEADS)]
        l_new, acc_new = [], []
        for h in range(N_HEADS):
            lh = l[h]
            for pb in probs[h]:
                lh = lh + jnp.dot(ones_rows, pb, preferred_element_type=F32)[0:1]
            l_new.append(lh)
            acc_new.append(acc[h] + value_products(starts, probs[h], h))
        return tuple(l_new), tuple(acc_new)

    def online_step(jj, carry):
        m, l, acc = carry
        starts, masked = step_logits(jj)
        m_new, l_new, acc_new = [], [], []
        for h in range(N_HEADS):
            mh = m[h]
            for s in masked[h]:
                mh = jnp.maximum(mh, _col_max(s))
            ps = [jnp.exp(s - mh) for s in masked[h]]
            alpha = jnp.exp(m[h] - mh)
            lh = alpha * l[h]
            for p in ps:
                lh = lh + _col_sum(p)
            m_new.append(mh)
            l_new.append(lh)
            acc_new.append(alpha * acc[h] + value_products(starts, [p.astype(BF16) for p in ps], h))
        return tuple(m_new), tuple(l_new), tuple(acc_new)

    def finish(l, acc):
        out_t = jnp.concatenate([acc[h] / l[h] for h in range(N_HEADS)], axis=0)
        o_ref[...] = out_t.T.astype(o_ref.dtype)

    zeros_l = tuple(jnp.zeros((1, qb), F32) for _ in range(N_HEADS))
    zeros_acc = tuple(jnp.zeros((HEAD_DIM, qb), F32) for _ in range(N_HEADS))
    small = jnp.max(bound[0:N_HEADS, :]) <= BOUND_LIMIT

    @pl.when(small)
    def _():
        finish(*lax.fori_loop(0, n_steps, bounded_step, (zeros_l, zeros_acc)))

    @pl.when(jnp.logical_not(small))
    def _():
        init = (tuple(jnp.full((1, qb), NEG_BIG, F32) for _ in range(N_HEADS)), zeros_l, zeros_acc)
        _, l, acc = lax.fori_loop(0, n_steps, online_step, init)
        finish(l, acc)


def _dsa(iq, iwt, qa, ik, ka, vat, batch, seq):
    n = qa.shape[0]
    nq = seq // Q_BLOCK
    topk = min(TOPK_MAX, seq // 4)
    qrow = lambda w: pl.BlockSpec((Q_BLOCK, w), lambda b, i: (b * nq + i, 0))
    whole = lambda w: pl.BlockSpec((seq, w), lambda b, i: (b, 0))
    return pl.pallas_call(
        functools.partial(_dsa_kernel, topk=topk),
        grid=(batch, nq),
        in_specs=[qrow(LANES), pl.BlockSpec((SUBLANES, Q_BLOCK), lambda b, i: (0, b * nq + i)),
                  qrow(BRANCH_WIDTH), whole(LANES), whole(BRANCH_WIDTH),
                  pl.BlockSpec((BRANCH_WIDTH, seq), lambda b, i: (0, b))],
        out_specs=qrow(BRANCH_WIDTH),
        out_shape=jax.ShapeDtypeStruct((n, BRANCH_WIDTH), BF16),
        scratch_shapes=[pltpu.VMEM((seq, Q_BLOCK), F32), pltpu.VMEM((seq, Q_BLOCK), jnp.int16),
                        pltpu.VMEM((seq, Q_BLOCK), jnp.int16), pltpu.VMEM((ONES_ROWS, LANES), F32)],
        compiler_params=_params("arbitrary", "arbitrary"),
        name="dsa",
    )(iq, iwt, qa, ik, ka, vat)


def _stick_kernel(q_ref, k_ref, vt_ref, o_ref):
    qb = q_ref.shape[0]
    i = pl.program_id(1)
    q_heads = _masked_heads(q_ref[...], HEAD_DIM)
    after = (lax.broadcasted_iota(jnp.int32, (K_TILE, K_TILE), 1)
             > lax.broadcasted_iota(jnp.int32, (K_TILE, K_TILE), 0))
    after = jnp.where(after, 1.0, 0.0).astype(BF16)
    after = jnp.concatenate([after, jnp.ones((ONES_ROWS, K_TILE), BF16)], axis=0)
    key_off = lax.broadcasted_iota(jnp.int32, (K_TILE, qb), 0)
    qry_pos = i * qb + lax.broadcasted_iota(jnp.int32, (K_TILE, qb), 1)

    def step(u, carry, masked):
        tails, acc = carry
        first = i - 2 * u
        tiles = [first, jnp.maximum(first - 1, 0)]
        bias = [None, jnp.where(first >= 1, 0.0, PHANTOM_LOGIT)]
        starts = [pl.multiple_of(t * K_TILE, K_TILE) for t in tiles]
        stricts = [starts[g] + key_off < qry_pos if masked else None for g in range(2)]
        zs = [[lax.dot_general(k_ref[pl.ds(st, K_TILE), :], q_heads[h], _CONTRACT_LAST,
                               preferred_element_type=F32) for h in range(N_HEADS)] for st in starts]
        log_betas, splits = [], []
        for g in range(2):
            lbs, sps = [], []
            for h in range(N_HEADS):
                z = zs[g][h] if bias[g] is None else zs[g][h] + bias[g]
                log_beta = jnp.minimum(z, 0.0) - jnp.log(1.0 + jnp.exp(_neg_abs(z)))
                log_keep = log_beta - z
                if masked:
                    log_keep = jnp.where(stricts[g], log_keep, 0.0)
                lbs.append(log_beta)
                sps.append(log_keep.astype(BF16))
            log_betas.append(lbs)
            splits.append(sps)
        sums = [[jnp.dot(after, splits[g][h], preferred_element_type=F32) for h in range(N_HEADS)]
                for g in range(2)]
        tails = list(tails)
        new_acc = []
        for h in range(N_HEADS):
            total = acc[h]
            for g in range(2):
                a = jnp.exp(log_betas[g][h] + sums[g][h][:K_TILE] + tails[h])
                if masked:
                    a = jnp.where(stricts[g], a, 0.0)
                tails[h] = tails[h] + sums[g][h][K_TILE:K_TILE + 1]
                vt = vt_ref[h * HEAD_DIM:(h + 1) * HEAD_DIM, pl.ds(starts[g], K_TILE)]
                total = total + jnp.dot(vt, a.astype(BF16), preferred_element_type=F32)
            new_acc.append(total)
        return tuple(tails), tuple(new_acc)

    init = (tuple(jnp.zeros((1, qb), F32) for _ in range(N_HEADS)),
            tuple(jnp.zeros((HEAD_DIM, qb), F32) for _ in range(N_HEADS)))
    carry = step(0, init, True)
    _, acc = lax.fori_loop(1, (i + 2) // 2, lambda u, c: step(u, c, False), carry)
    o_ref[...] = jnp.concatenate(acc, axis=0).T.astype(o_ref.dtype)


def _stick_breaking(qc, kc, vct, batch, seq):
    n = qc.shape[0]
    nq = seq // Q_BLOCK
    qrow = pl.BlockSpec((Q_BLOCK, BRANCH_WIDTH), lambda b, i: (b * nq + i, 0))
    return pl.pallas_call(
        _stick_kernel,
        grid=(batch, nq),
        in_specs=[qrow, pl.BlockSpec((seq, BRANCH_WIDTH), lambda b, i: (b, 0)),
                  pl.BlockSpec((BRANCH_WIDTH, seq), lambda b, i: (0, b))],
        out_specs=qrow,
        out_shape=jax.ShapeDtypeStruct((n, BRANCH_WIDTH), BF16),
        compiler_params=_params("arbitrary", "arbitrary"),
        name="stick_breaking",
    )(qc, kc, vct)


def _merge_kernel(x_ref, g1_ref, ya_ref, yb_ref, yc_ref, yd_ref, wg_ref, wb_ref, wo_ref,
                  g2_ref, wr_ref, br_ref, x1_o, comb_o):
    x = x_ref[...]
    d = x.shape[1]
    h = _rms_bf16(x, g1_ref[...])
    ys = (ya_ref[...], yb_ref[...], yc_ref[...], yd_ref[...])
    col = 256
    pieces = []
    for c in range(d // col):
        m = None
        for n in range(N_BRANCH):
            logit = jnp.dot(h, wg_ref[:, n * d + c * col:n * d + (c + 1) * col],
                            preferred_element_type=F32)
            br = jnp.dot(ys[n], wb_ref[n, :, c * col:(c + 1) * col], preferred_element_type=F32)
            t = jax.nn.sigmoid(logit) * br
            m = t if m is None else m + t
        pieces.append(m.astype(BF16))
    merged = jnp.concatenate(pieces, axis=1)
    x1 = x + jnp.dot(merged, wo_ref[...], preferred_element_type=F32)
    x1_o[...] = x1

    ms = jnp.mean(x1 * x1, axis=-1, keepdims=True)
    h2 = x1 * lax.rsqrt(ms + EPS) * g2_ref[...]
    h_hi = h2.astype(BF16)
    h_lo = (h2 - h_hi.astype(F32)).astype(BF16)
    both = jnp.dot(h_hi, wr_ref[...], preferred_element_type=F32)
    cross = jnp.dot(h_lo, wr_ref[:, 0:LANES], preferred_element_type=F32)
    logits = both[:, 0:LANES] + both[:, LANES:2 * LANES] + cross + br_ref[...]
    rows = logits.shape[0]
    lane = lax.broadcasted_iota(jnp.int32, (rows, LANES), 1)
    neg = -jnp.inf
    is_grp = (lane >= N_EXPERTS) & (lane < N_EXPERTS + N_EXPERT_GROUPS)
    gl = jnp.where(is_grp, logits, neg)
    gmax = jnp.max(gl, axis=1, keepdims=True)
    gidx = jnp.min(jnp.where(gl == gmax, lane, LANES), axis=1, keepdims=True) - N_EXPERTS
    grp_p = 1.0 / jnp.sum(jnp.where(is_grp, jnp.exp(gl - gmax), 0.0), axis=1, keepdims=True)
    in_grp = (lane >= gidx * EXPERTS_PER_GROUP) & (lane < (gidx + 1) * EXPERTS_PER_GROUP)
    el = jnp.where(in_grp, logits, neg)
    l1 = jnp.max(el, axis=1, keepdims=True)
    j1 = jnp.min(jnp.where(el == l1, lane, LANES), axis=1, keepdims=True)
    el2 = jnp.where(lane == j1, neg, el)
    l2 = jnp.max(el2, axis=1, keepdims=True)
    j2 = jnp.min(jnp.where(el2 == l2, lane, LANES), axis=1, keepdims=True)
    e2 = jnp.exp(l2 - l1)
    w1 = grp_p / (1.0 + e2)
    w2 = grp_p * e2 / (1.0 + e2)
    comb_o[...] = jnp.where(lane == j1, w1, 0.0) + jnp.where(lane == j2, w2, 0.0)


def _merge(x2, norm1_g, ys, w_gate, w_branch, w_out, norm2_g, w_router, b_router):
    n, d = x2.shape
    rows = MERGE_ROWS
    row = lambda w: pl.BlockSpec((rows, w), lambda i: (i, 0))
    full = lambda a: pl.BlockSpec(a.shape, lambda i: (0,) * a.ndim)
    args = (x2, norm1_g.reshape(1, d), *ys, w_gate, w_branch, w_out, norm2_g.reshape(1, d),
            w_router, b_router)
    in_specs = [row(d), full(args[1])] + [row(BRANCH_WIDTH)] * 4 + [full(a) for a in args[6:]]
    return pl.pallas_call(
        _merge_kernel,
        grid=(n // rows,),
        in_specs=in_specs,
        out_specs=[row(d), row(LANES)],
        out_shape=[jax.ShapeDtypeStruct((n, d), F32), jax.ShapeDtypeStruct((n, LANES), F32)],
        compiler_params=_params("arbitrary"),
        name="merge_router",
    )(*args)


def _moe_kernel(x_ref, g2_ref, comb_ref, wgu_ref, wd_ref, gf_ref, o_ref, h_scr, acc_scr, *, final_norm):
    step = pl.program_id(1)

    @pl.when(step == 0)
    def _():
        h_scr[...] = _rms_bf16(x_ref[...], g2_ref[...])
        acc_scr[...] = jnp.zeros(acc_scr.shape, F32)

    h = h_scr[...]
    comb = comb_ref[...]
    lane = lax.broadcasted_iota(jnp.int32, comb.shape, 1)
    total = None
    for j in range(EXPERTS_PER_STEP):
        e = step * EXPERTS_PER_STEP + j
        gu = jnp.dot(h, wgu_ref[j], preferred_element_type=F32)
        w = jnp.sum(jnp.where(lane == e, comb, 0.0), axis=1, keepdims=True)
        hh = (jax.nn.silu(gu[:, :D_EXPERT]) * gu[:, D_EXPERT:] * w).astype(BF16)
        t = jnp.dot(hh, wd_ref[j], preferred_element_type=F32)
        total = t if total is None else total + t
    acc_scr[...] += total

    @pl.when(step == pl.num_programs(1) - 1)
    def _():
        y = x_ref[...] + acc_scr[...]
        if final_norm:
            ms = jnp.mean(y * y, axis=-1, keepdims=True)
            y = y * lax.rsqrt(ms + EPS) * gf_ref[...]
        o_ref[...] = y


def _moe(x1, norm2_g, comb, w_gu, w_d, norm_f_g, final_norm):
    n, d = x1.shape
    rows = min(MOE_ROWS, n)
    n_exp = w_gu.shape[0]
    row = lambda w: pl.BlockSpec((rows, w), lambda i, e: (i, 0))
    vec = pl.BlockSpec((1, d), lambda i, e: (0, 0))
    return pl.pallas_call(
        functools.partial(_moe_kernel, final_norm=final_norm),
        grid=(n // rows, n_exp // EXPERTS_PER_STEP),
        in_specs=[row(d), vec, row(LANES),
                  pl.BlockSpec((EXPERTS_PER_STEP, d, 2 * D_EXPERT), lambda i, e: (e, 0, 0)),
                  pl.BlockSpec((EXPERTS_PER_STEP, D_EXPERT, d), lambda i, e: (e, 0, 0)),
                  vec],
        out_specs=row(d),
        out_shape=jax.ShapeDtypeStruct((n, d), F32),
        scratch_shapes=[pltpu.VMEM((rows, d), BF16), pltpu.VMEM((rows, d), F32)],
        compiler_params=_params("arbitrary", "arbitrary"),
        name="experts",
    )(x1, norm2_g.reshape(1, d), comb, w_gu, w_d, norm_f_g.reshape(1, d))


def kernel(x, positions, norm1_g, w_in, sgu_ln_g, sgu_ln_b, w_spatial, b_spatial, conv_w, w_branch,
           w_out, norm2_g, w_router_group, b_router_group, w_router_expert, b_router_expert,
           w_exp_gate, w_exp_up, w_exp_down, norm_f_g):
    batch, seq, d = x.shape
    depth = w_in.shape[0]
    n = batch * seq
    assert TILES_PER_STEP == 2
    assert seq % COUNT_CHUNK == 0 and seq % (TILES_PER_STEP * K_TILE) == 0
    assert n % PROJ_ROWS == 0 and n % MERGE_ROWS == 0
    x2 = x.reshape(n, d)
    tables = _rope_tables(positions)
    for l in range(depth):
        w_a, w_t, gate_off = _proj_weight(w_in[l])
        w_gate = w_in[l][:, gate_off:].astype(BF16)
        pad = jnp.zeros((d, LANES - N_EXPERTS - N_EXPERT_GROUPS), F32)
        w_router = jnp.concatenate([w_router_expert[l], w_router_group[l], pad], axis=1)
        w_router_hi = w_router.astype(BF16)
        w_router_lo = (w_router - w_router_hi.astype(F32)).astype(BF16)
        w_router = jnp.concatenate([w_router_hi, w_router_lo], axis=1)
        b_router = jnp.concatenate([b_router_expert[l], b_router_group[l], pad[0]]).reshape(1, LANES)
        qa, ka, vat, iq, ik, iwt, y_b, qc, kc, vct, y_d = _project(
            x2, norm1_g[l], w_a, w_t, tables, sgu_ln_g[l], sgu_ln_b[l], w_spatial[l], b_spatial[l],
            conv_w[l], seq)
        y_a = _dsa(iq, iwt, qa, ik, ka, vat, batch, seq)
        y_c = _stick_breaking(qc, kc, vct, batch, seq)
        x1, comb = _merge(x2, norm1_g[l], (y_a, y_b, y_c, y_d), w_gate, w_branch[l].astype(BF16),
                          w_out[l].astype(BF16), norm2_g[l], w_router, b_router)
        w_gu = jnp.concatenate([w_exp_gate[l], w_exp_up[l]], axis=2).astype(BF16)
        x2 = _moe(x1, norm2_g[l], comb, w_gu, w_exp_down[l].astype(BF16), norm_f_g,
                  final_norm=(l == depth - 1))
    return x2.reshape(batch, seq, d)
```

```python
import functools

import jax
import jax.numpy as jnp
from jax import lax
from jax.experimental import pallas as pl
from jax.experimental.pallas import tpu as pltpu

F32 = jnp.float32
BF16 = jnp.bfloat16

EPS = 1e-6
ROPE_THETA = 10000.0
HEAD_DIM = 64
BRANCH_WIDTH = 256
N_HEADS = 4
IDX_DIM = 32
TOPK_MAX = 256
CHUNK = 128
N_GROUPS_B = 4
CONV_WIDTH = 3
N_BRANCH = 4
N_EXPERT_GROUPS = 4
EXPERTS_PER_GROUP = 8
N_EXPERTS = N_EXPERT_GROUPS * EXPERTS_PER_GROUP
D_EXPERT = 256

LANES = 128
SUBLANES = 8
Q_BLOCK = 256
K_TILE = Q_BLOCK
TILES_PER_STEP = 2
COUNT_CHUNK = 512
COUNT_ROWS = 64
PROJ_ROWS = 256
MERGE_ROWS = 512
MOE_ROWS = 1024
EXPERTS_PER_STEP = 4
CONV_HALO = 8
VMEM_LIMIT = 56 * 1024 * 1024
PHANTOM_LOGIT = -1e4
NEG_BIG = -1e30
EXP_ZERO_BELOW = -110.0
INT_MIN = -2 ** 31
HALF_RANGE = 2 ** 15
ONES_ROWS = 16
BOUND_MARGIN = 1.02
BOUND_LIMIT = 40.0

_OFF = {}
_o = 0
for _name, _w in (("q", 512), ("k", 512), ("iq", 256), ("ik", 256),
                  ("ub", 256), ("vb", 256), ("qc", 256), ("kc", 256),
                  ("bd", 256), ("cd", 256), ("xd", 256)):
    _OFF[_name] = (_o, _w)
    _o += _w
W_A_COLS = _o
T_VA, T_VC, T_IW = 0, BRANCH_WIDTH, 2 * BRANCH_WIDTH
W_T_ROWS = 2 * BRANCH_WIDTH + 16

_CONTRACT_LAST = (((1,), (1,)), ((), ()))


def _params(*sem):
    return pltpu.CompilerParams(dimension_semantics=sem, vmem_limit_bytes=VMEM_LIMIT)


def _rope_table_kernel(pos_ref, inv_a_ref, inv_i_ref, cos_a, sin_a, cos_i, sin_i):
    p = pos_ref[...]
    a = p * inv_a_ref[...]
    cos_a[...] = jnp.cos(a)
    sin_a[...] = jnp.sin(a)
    b = p * inv_i_ref[...]
    cos_i[...] = jnp.cos(b)
    sin_i[...] = jnp.sin(b)


def _rope_tables(positions):
    n = positions.size
    rows = PROJ_ROWS
    pos = positions.astype(F32).reshape(n, 1)
    inv_a = ROPE_THETA ** (-jnp.arange(0, HEAD_DIM, 2, dtype=F32) / HEAD_DIM)
    inv_i = ROPE_THETA ** (-jnp.arange(0, IDX_DIM, 2, dtype=F32) / IDX_DIM)
    inv_a = jnp.tile(inv_a, LANES // inv_a.size).reshape(1, LANES)
    inv_i = jnp.tile(inv_i, LANES // inv_i.size).reshape(1, LANES)
    tab = jax.ShapeDtypeStruct((n, LANES), F32)
    row_spec = pl.BlockSpec((rows, LANES), lambda i: (i, 0))
    const = pl.BlockSpec((1, LANES), lambda i: (0, 0))
    return pl.pallas_call(
        _rope_table_kernel,
        grid=(n // rows,),
        in_specs=[pl.BlockSpec((rows, 1), lambda i: (i, 0)), const, const],
        out_specs=[row_spec] * 4,
        out_shape=[tab] * 4,
        compiler_params=_params("arbitrary"),
        name="rope_tables",
    )(pos, inv_a, inv_i)


def _rms_bf16(x, g):
    ms = jnp.mean(x * x, axis=-1, keepdims=True)
    return (x * lax.rsqrt(ms + EPS) * g).astype(BF16)


def _proj_kernel(x_ref, g_ref, w_ref, wt_ref, cos_a, sin_a, cos_i, sin_i, lng_ref, lnb_ref, ws_ref,
                 bs_ref, cw_ref,
                 qa_o, ka_o, vat_o, iq_o, ik_o, iwt_o, yb_o, qc_o, kc_o, vct_o, yd_o,
                 zbuf, *, steps_per_seq):
    rows = x_ref.shape[0]
    h = _rms_bf16(x_ref[...], g_ref[...])

    def mm(name):
        off, width = _OFF[name]
        return jnp.dot(h, w_ref[:, off:off + width], preferred_element_type=F32)

    def rope(name, cos_ref, sin_ref, out_ref, scale):
        p = mm(name)
        w = p.shape[1] // 2
        c = cos_ref[...]
        s = sin_ref[...]
        for j in range(w // LANES):
            lo = j * LANES
            r = p[:, lo:lo + LANES] * c + p[:, w + lo:w + lo + LANES] * s
            out_ref[:, lo:lo + LANES] = (r if scale == 1.0 else r * scale).astype(out_ref.dtype)

    att_scale = HEAD_DIM ** -0.5
    rope("q", cos_a, sin_a, qa_o, att_scale)
    rope("k", cos_a, sin_a, ka_o, 1.0)
    rope("iq", cos_i, sin_i, iq_o, 1.0)
    rope("ik", cos_i, sin_i, ik_o, 1.0)
    qc_o[...] = (mm("qc") * att_scale).astype(BF16)
    kc_o[...] = mm("kc").astype(BF16)
    t = lax.dot_general(wt_ref[...], h, _CONTRACT_LAST, preferred_element_type=F32)
    vat_o[...] = t[T_VA:T_VA + BRANCH_WIDTH, :].astype(BF16)
    vct_o[...] = t[T_VC:T_VC + BRANCH_WIDTH, :].astype(BF16)
    iwt_o[...] = t[T_IW:T_IW + SUBLANES, :]

    u = jax.nn.gelu(mm("ub"))
    v = jax.nn.gelu(mm("vb"))
    mu = jnp.mean(v, axis=-1, keepdims=True)
    var = jnp.mean(jnp.square(v - mu), axis=-1, keepdims=True)
    vn = ((v - mu) * lax.rsqrt(var + EPS) * lng_ref[...] + lnb_ref[...]).astype(BF16)
    t_idx = lax.broadcasted_iota(jnp.int32, (CHUNK, CHUNK), 0)
    s_idx = lax.broadcasted_iota(jnp.int32, (CHUNK, CHUNK), 1)
    causal = s_idx <= t_idx
    group_masks = [_head_lane_mask(BRANCH_WIDTH, BRANCH_WIDTH // N_GROUPS_B, g) for g in range(N_GROUPS_B)]
    w_tril = [jnp.where(causal, ws_ref[g], 0.0).astype(BF16) for g in range(N_GROUPS_B)]
    for c in range(rows // CHUNK):
        vc = vn[c * CHUNK:(c + 1) * CHUNK, :]
        mixed = bs_ref[...]
        for g in range(N_GROUPS_B):
            full = jnp.dot(w_tril[g], vc, preferred_element_type=F32)
            mixed = mixed + jnp.where(group_masks[g], full, 0.0)
        yb_o[c * CHUNK:(c + 1) * CHUNK, :] = (u[c * CHUNK:(c + 1) * CHUNK, :] * mixed).astype(BF16)

    bd = mm("bd")
    z = mm("cd") * mm("xd")

    @pl.when(pl.program_id(0) % steps_per_seq == 0)
    def _():
        zbuf[0:CONV_HALO, :] = jnp.zeros((CONV_HALO, BRANCH_WIDTH), F32)

    zbuf[CONV_HALO:CONV_HALO + rows, :] = z
    y = cw_ref[CONV_WIDTH - 1:CONV_WIDTH, :] * z
    for tap in range(CONV_WIDTH - 1):
        shift = CONV_WIDTH - 1 - tap
        y = y + cw_ref[tap:tap + 1, :] * zbuf[CONV_HALO - shift:CONV_HALO - shift + rows, :]
    yd_o[...] = (bd * y).astype(BF16)
    zbuf[0:CONV_HALO, :] = z[rows - CONV_HALO:rows, :]


def _swap_halves(w, dim):
    d_in, cols = w.shape
    wh = w.reshape(d_in, cols // dim, 2, dim // 2)
    return jnp.stack([-wh[:, :, 1], wh[:, :, 0]], axis=2).reshape(d_in, cols)


def _proj_weight(w_in):
    d = w_in.shape[0]
    sizes = (BRANCH_WIDTH,) * 3 + (N_HEADS * IDX_DIM, IDX_DIM, N_HEADS) + (BRANCH_WIDTH,) * 8
    parts, o = [], 0
    for s in sizes:
        parts.append(w_in[:, o:o + s])
        o += s
    qa, ka, va, iq, ik, iw, ub, vb, qc, kc, vc, bd, cd, xd = parts
    ik_rep = jnp.tile(ik, (1, N_HEADS))
    cols = [qa, _swap_halves(qa, HEAD_DIM), ka, _swap_halves(ka, HEAD_DIM),
            iq, _swap_halves(iq, IDX_DIM), ik_rep, _swap_halves(ik_rep, IDX_DIM),
            ub, vb, qc, kc, bd, cd, xd]
    iw_pad = jnp.concatenate([iw, jnp.zeros((d, W_T_ROWS - 2 * BRANCH_WIDTH - N_HEADS), w_in.dtype)], axis=1)
    w_t = jnp.concatenate([va, vc, iw_pad], axis=1).T
    return jnp.concatenate(cols, axis=1).astype(BF16), w_t.astype(BF16), o


def _project(x2, norm_g, w_a, w_t, tables, ln_g, ln_b, w_spatial, b_spatial, conv_w, seq):
    n, d = x2.shape
    rows = PROJ_ROWS
    cos_a, sin_a, cos_i, sin_i = tables
    bias = jnp.repeat(b_spatial.T, BRANCH_WIDTH // N_GROUPS_B, axis=1)
    row = lambda w: pl.BlockSpec((rows, w), lambda i: (i, 0))
    col = lambda r: pl.BlockSpec((r, rows), lambda i: (0, i))
    full = lambda a: pl.BlockSpec(a.shape, lambda i: (0,) * a.ndim)
    bf = lambda w: jax.ShapeDtypeStruct((n, w), BF16)
    bft = jax.ShapeDtypeStruct((BRANCH_WIDTH, n), BF16)
    args = (x2, norm_g.reshape(1, d), w_a, w_t, cos_a, sin_a, cos_i, sin_i, ln_g.reshape(1, -1),
            ln_b.reshape(1, -1), w_spatial, bias, conv_w)
    in_specs = [row(d), full(args[1]), full(w_a), full(w_t), row(LANES), row(LANES), row(LANES),
                row(LANES), full(args[8]), full(args[9]), full(w_spatial), full(bias), full(conv_w)]
    out_shape = [bf(256), bf(256), bft, bf(128), bf(128), jax.ShapeDtypeStruct((SUBLANES, n), F32),
                 bf(256), bf(256), bf(256), bft, bf(256)]
    out_specs = [row(256), row(256), col(BRANCH_WIDTH), row(128), row(128), col(SUBLANES),
                 row(256), row(256), row(256), col(BRANCH_WIDTH), row(256)]
    return pl.pallas_call(
        functools.partial(_proj_kernel, steps_per_seq=seq // rows),
        grid=(n // rows,),
        in_specs=in_specs,
        out_specs=out_specs,
        out_shape=out_shape,
        scratch_shapes=[pltpu.VMEM((CONV_HALO + rows, BRANCH_WIDTH), F32)],
        compiler_params=_params("arbitrary"),
        name="project",
    )(*args)


def _head_lane_mask(width, per_head, h, rows=1):
    lane = lax.broadcasted_iota(jnp.int32, (rows, width), 1)
    return (lane >= h * per_head) & (lane < (h + 1) * per_head)


def _masked_heads(x, per_head):
    return [jnp.where(_head_lane_mask(x.shape[1], per_head, h), x, jnp.zeros_like(x))
            for h in range(N_HEADS)]


def _float_to_key(x):
    bits = lax.bitcast_convert_type(x, jnp.int32)
    sign = bits >> 31
    return (bits ^ (sign & jnp.int32(0x7FFFFFFF))) - sign


def _key_to_float(key):
    bits = jnp.where(key < 0, (-key) | jnp.int32(INT_MIN), key)
    return lax.bitcast_convert_type(bits, F32)


def _neg_abs(x):
    bits = lax.bitcast_convert_type(x, jnp.int32) | jnp.int32(INT_MIN)
    return lax.bitcast_convert_type(bits, F32)


def _col_reduce(x, op, final):
    while x.shape[0] > SUBLANES:
        half = x.shape[0] // 2
        x = op(x[:half], x[half:])
    return final(x, axis=0, keepdims=True)


def _col_sum(x):
    return _col_reduce(x, jnp.add, jnp.sum)


def _col_max(x):
    return _col_reduce(x, jnp.maximum, jnp.max)


def _step_starts(step):
    return [pl.multiple_of((step * TILES_PER_STEP + g) * K_TILE, K_TILE) for g in range(TILES_PER_STEP)]


def _dsa_kernel(iq_ref, iwt_ref, q_ref, ik_ref, k_ref, vt_ref, o_ref, sc_ref, hi_ref, lo_ref, kmax_ref, *,
                topk):
    qb = q_ref.shape[0]
    i = pl.program_id(1)
    n_tiles = i + 1
    n_steps = (n_tiles + TILES_PER_STEP - 1) // TILES_PER_STEP

    @pl.when(i == 0)
    def _():
        sc_ref[...] = jnp.full(sc_ref.shape, -jnp.inf, F32)
        hi_ref[...] = jnp.full(hi_ref.shape, -HALF_RANGE, jnp.int16)
        lo_ref[...] = jnp.full(lo_ref.shape, -HALF_RANGE, jnp.int16)

    iq_heads = _masked_heads(iq_ref[...], IDX_DIM)
    iwt = iwt_ref[...]
    iw_heads = [iwt[h:h + 1, :] for h in range(N_HEADS)]

    def put_scores(start, sc):
        sc_ref[pl.ds(start, K_TILE), :] = sc
        key = _float_to_key(sc)
        hi_ref[pl.ds(start, K_TILE), :] = (key >> 16).astype(jnp.int16)
        lo_ref[pl.ds(start, K_TILE), :] = ((key & 0xFFFF) - HALF_RANGE).astype(jnp.int16)

    key_off = lax.broadcasted_iota(jnp.int32, (K_TILE, qb), 0)
    qry_pos = i * qb + lax.broadcasted_iota(jnp.int32, (K_TILE, qb), 1)

    def score_step(jj, masked):
        starts = _step_starts(jj)
        dots = [[lax.dot_general(ik_ref[pl.ds(st, K_TILE), :], iq_heads[h], _CONTRACT_LAST,
                                 preferred_element_type=F32) for h in range(N_HEADS)] for st in starts]
        for g in range(TILES_PER_STEP):
            sc = None
            for h in range(N_HEADS):
                t = iw_heads[h] * jnp.maximum(dots[g][h], 0.0)
                sc = t if sc is None else sc + t
            if masked:
                sc = jnp.where(starts[g] + key_off <= qry_pos, sc, -jnp.inf)
            put_scores(starts[g], sc)

    def full_step(jj, carry):
        score_step(jj, False)
        return carry

    lax.fori_loop(0, i // TILES_PER_STEP, full_step, 0)
    score_step(i // TILES_PER_STEP, True)

    n_chunks = (n_tiles * K_TILE + COUNT_CHUNK - 1) // COUNT_CHUNK
    kf = jnp.float32(topk)

    def chunk_rows(c):
        base = pl.multiple_of(c * COUNT_CHUNK, COUNT_CHUNK)
        return [pl.ds(base + r * COUNT_ROWS, COUNT_ROWS) for r in range(COUNT_CHUNK // COUNT_ROWS)]

    def count(ref, pred, dtype):
        one, zero = jnp.ones((), dtype), jnp.zeros((), dtype)

        def body(c, acc):
            for rows in chunk_rows(c):
                acc = acc + jnp.where(pred(ref[rows, :]), one, zero)
            return acc
        acc = lax.fori_loop(0, n_chunks, body, jnp.zeros((COUNT_ROWS, qb), dtype))
        return jnp.sum(acc.astype(F32), axis=0, keepdims=True)

    def bisect16(ref, target):
        def bit_step(b, carry):
            best, cnt_best, cnt_above = carry
            cand = best + lax.shift_left(jnp.int32(1), 15 - b)
            cand16 = cand.astype(jnp.int16)
            cnt = count(ref, lambda blk: blk >= cand16, jnp.int16)
            ok = cnt >= target
            return jnp.where(ok, cand, best), jnp.where(ok, cnt, cnt_best), jnp.where(ok, cnt_above, cnt)
        init = (jnp.full((1, qb), -HALF_RANGE, jnp.int32), jnp.zeros((1, qb), F32), jnp.zeros((1, qb), F32))
        return lax.fori_loop(0, 16, bit_step, init)

    hi_t, cnt_hi, above = bisect16(hi_ref, kf)
    hi_t16 = hi_t.astype(jnp.int16)

    def mask_low(c, carry):
        for rows in chunk_rows(c):
            lo_ref[rows, :] = jnp.where(hi_ref[rows, :] == hi_t16, lo_ref[rows, :], jnp.int16(-HALF_RANGE))
        return carry

    lax.fori_loop(0, n_chunks, mask_low, 0)
    lo_t, cnt_lo, _ = bisect16(lo_ref, kf - above)
    cnt_key = above + jnp.where(lo_t == -HALF_RANGE, cnt_hi - above, cnt_lo)
    key = hi_t * (2 * HALF_RANGE) + (lo_t + HALF_RANGE)
    few = i * qb + lax.broadcasted_iota(jnp.int32, (1, qb), 1) + 1 < topk
    thr = jnp.where(few, jnp.finfo(F32).min, _key_to_float(key))

    @pl.when(jnp.max(jnp.where(few, kf, cnt_key)) > kf)
    def _():
        need = kf - count(sc_ref, lambda blk: blk > thr, F32)
        before = (lax.broadcasted_iota(jnp.int32, (K_TILE, K_TILE), 1)
                  < lax.broadcasted_iota(jnp.int32, (K_TILE, K_TILE), 0))
        before = jnp.where(before, 1.0, 0.0).astype(BF16)

        def tie_body(jj, seen):
            starts = _step_starts(jj)
            blks = [sc_ref[pl.ds(st, K_TILE), :] for st in starts]
            eqs = [blk == thr for blk in blks]
            eq_fs = [jnp.where(eq, 1.0, 0.0) for eq in eqs]
            ranks = [jnp.dot(before, eq_f.astype(BF16), preferred_element_type=F32) for eq_f in eq_fs]
            for g in range(TILES_PER_STEP):
                drop = eqs[g] & (seen + ranks[g] >= need)
                sc_ref[pl.ds(starts[g], K_TILE), :] = jnp.where(drop, -jnp.inf, blks[g])
                seen = seen + _col_sum(eq_fs[g])
            return seen

        lax.fori_loop(0, n_steps, tie_body, jnp.zeros((1, qb), F32))

    q = q_ref[...]
    q_heads = _masked_heads(q, HEAD_DIM)
    head_rows = jnp.concatenate(
        [jnp.where(_head_lane_mask(BRANCH_WIDTH, HEAD_DIM, h), 1.0, 0.0) for h in range(N_HEADS)]
        + [jnp.zeros((ONES_ROWS - N_HEADS, BRANCH_WIDTH), F32)], axis=0).astype(BF16)
    qf = q.astype(F32)
    q_sq = lax.dot_general(head_rows, (qf * qf).astype(BF16), _CONTRACT_LAST,
                           preferred_element_type=F32)
    kf32 = k_ref[pl.ds(pl.multiple_of(i * K_TILE, K_TILE), K_TILE), :].astype(F32)
    k_sq = lax.dot_general(head_rows, (kf32 * kf32).astype(BF16), _CONTRACT_LAST,
                           preferred_element_type=F32)
    k_sq_max = jnp.max(k_sq, axis=1, keepdims=True)

    @pl.when(i == 0)
    def _():
        kmax_ref[...] = jnp.zeros(kmax_ref.shape, F32)

    kmax_ref[...] = jnp.maximum(kmax_ref[...], jnp.broadcast_to(k_sq_max, kmax_ref.shape))
    bound = jnp.sqrt(q_sq * kmax_ref[:, 0:1]) * BOUND_MARGIN
    bounds = [bound[h:h + 1, :] for h in range(N_HEADS)]
    ones_rows = jnp.ones((ONES_ROWS, K_TILE), BF16)

    def step_logits(jj):
        starts = _step_starts(jj)
        sels = [sc_ref[pl.ds(st, K_TILE), :] >= thr for st in starts]
        logits = [[lax.dot_general(k_ref[pl.ds(st, K_TILE), :], q_heads[h], _CONTRACT_LAST,
                                   preferred_element_type=F32) for h in range(N_HEADS)] for st in starts]
        masked = [[jnp.where(sels[g], logits[g][h], NEG_BIG) for g in range(TILES_PER_STEP)]
                  for h in range(N_HEADS)]
        return starts, masked

    def value_products(starts, probs, h):
        pv = None
        for g, st in enumerate(starts):
            vt = vt_ref[h * HEAD_DIM:(h + 1) * HEAD_DIM, pl.ds(st, K_TILE)]
            t = jnp.dot(vt, probs[g], preferred_element_type=F32)
            pv = t if pv is None else pv + t
        return pv

    def bounded_step(jj, carry):
        l, acc = carry
        starts, masked = step_logits(jj)
        probs = [[jnp.exp(s - bounds[h]).astype(BF16) for s in masked[h]] for h in range(N_HEADS)]
        l_new, acc_new = [], []
        for h in range(N_HEADS):
            lh = l[h]
            for pb in probs[h]:
                lh = lh + jnp.dot(ones_rows, pb, preferred_element_type=F32)[0:1]
            l_new.append(lh)
            acc_new.append(acc[h] + value_products(starts, probs[h], h))
        return tuple(l_new), tuple(acc_new)

    def online_step(jj, carry):
        m, l, acc = carry
        starts, masked = step_logits(jj)
        m_new, l_new, acc_new = [], [], []
        for h in range(N_HEADS):
            mh = m[h]
            for s in masked[h]:
                mh = jnp.maximum(mh, _col_max(s))
            ps = [jnp.exp(s - mh) for s in masked[h]]
            alpha = jnp.exp(m[h] - mh)
            lh = alpha * l[h]
            for p in ps:
                lh = lh + _col_sum(p)
            m_new.append(mh)
            l_new.append(lh)
            acc_new.append(alpha * acc[h] + value_products(starts, [p.astype(BF16) for p in ps], h))
        return tuple(m_new), tuple(l_new), tuple(acc_new)

    def finish(l, acc):
        out_t = jnp.concatenate([acc[h] / l[h] for h in range(N_HEADS)], axis=0)
        o_ref[...] = out_t.T.astype(o_ref.dtype)

    zeros_l = tuple(jnp.zeros((1, qb), F32) for _ in range(N_HEADS))
    zeros_acc = tuple(jnp.zeros((HEAD_DIM, qb), F32) for _ in range(N_HEADS))
    small = jnp.max(bound[0:N_HEADS, :]) <= BOUND_LIMIT

    @pl.when(small)
    def _():
        finish(*lax.fori_loop(0, n_steps, bounded_step, (zeros_l, zeros_acc)))

    @pl.when(jnp.logical_not(small))
    def _():
        init = (tuple(jnp.full((1, qb), NEG_BIG, F32) for _ in range(N_HEADS)), zeros_l, zeros_acc)
        _, l, acc = lax.fori_loop(0, n_steps, online_step, init)
        finish(l, acc)


def _dsa(iq, iwt, qa, ik, ka, vat, batch, seq):
    n = qa.shape[0]
    nq = seq // Q_BLOCK
    topk = min(TOPK_MAX, seq // 4)
    qrow = lambda w: pl.BlockSpec((Q_BLOCK, w), lambda b, i: (b * nq + i, 0))
    whole = lambda w: pl.BlockSpec((seq, w), lambda b, i: (b, 0))
    return pl.pallas_call(
        functools.partial(_dsa_kernel, topk=topk),
        grid=(batch, nq),
        in_specs=[qrow(LANES), pl.BlockSpec((SUBLANES, Q_BLOCK), lambda b, i: (0, b * nq + i)),
                  qrow(BRANCH_WIDTH), whole(LANES), whole(BRANCH_WIDTH),
                  pl.BlockSpec((BRANCH_WIDTH, seq), lambda b, i: (0, b))],
        out_specs=qrow(BRANCH_WIDTH),
        out_shape=jax.ShapeDtypeStruct((n, BRANCH_WIDTH), BF16),
        scratch_shapes=[pltpu.VMEM((seq, Q_BLOCK), F32), pltpu.VMEM((seq, Q_BLOCK), jnp.int16),
                        pltpu.VMEM((seq, Q_BLOCK), jnp.int16), pltpu.VMEM((ONES_ROWS, LANES), F32)],
        compiler_params=_params("arbitrary", "arbitrary"),
        name="dsa",
    )(iq, iwt, qa, ik, ka, vat)


def _stick_kernel(q_ref, k_ref, vt_ref, o_ref):
    qb = q_ref.shape[0]
    i = pl.program_id(1)
    q_heads = _masked_heads(q_ref[...], HEAD_DIM)
    after = (lax.broadcasted_iota(jnp.int32, (K_TILE, K_TILE), 1)
             > lax.broadcasted_iota(jnp.int32, (K_TILE, K_TILE), 0))
    after = jnp.where(after, 1.0, 0.0).astype(BF16)
    after = jnp.concatenate([after, jnp.ones((ONES_ROWS, K_TILE), BF16)], axis=0)
    key_off = lax.broadcasted_iota(jnp.int32, (K_TILE, qb), 0)
    qry_pos = i * qb + lax.broadcasted_iota(jnp.int32, (K_TILE, qb), 1)

    def step(u, carry, masked):
        tails, acc = carry
        first = i - 2 * u
        tiles = [first, jnp.maximum(first - 1, 0)]
        bias = [None, jnp.where(first >= 1, 0.0, PHANTOM_LOGIT)]
        starts = [pl.multiple_of(t * K_TILE, K_TILE) for t in tiles]
        stricts = [starts[g] + key_off < qry_pos if masked else None for g in range(2)]
        zs = [[lax.dot_general(k_ref[pl.ds(st, K_TILE), :], q_heads[h], _CONTRACT_LAST,
                               preferred_element_type=F32) for h in range(N_HEADS)] for st in starts]
        log_betas, splits = [], []
        for g in range(2):
            lbs, sps = [], []
            for h in range(N_HEADS):
                z = zs[g][h] if bias[g] is None else zs[g][h] + bias[g]
                log_beta = jnp.minimum(z, 0.0) - jnp.log(1.0 + jnp.exp(_neg_abs(z)))
                log_keep = log_beta - z
                if masked:
                    log_keep = jnp.where(stricts[g], log_keep, 0.0)
                lbs.append(log_beta)
                sps.append(log_keep.astype(BF16))
            log_betas.append(lbs)
            splits.append(sps)
        sums = [[jnp.dot(after, splits[g][h], preferred_element_type=F32) for h in range(N_HEADS)]
                for g in range(2)]
        tails = list(tails)
        new_acc = []
        for h in range(N_HEADS):
            total = acc[h]
            for g in range(2):
                a = jnp.exp(log_betas[g][h] + sums[g][h][:K_TILE] + tails[h])
                if masked:
                    a = jnp.where(stricts[g], a, 0.0)
                tails[h] = tails[h] + sums[g][h][K_TILE:K_TILE + 1]
                vt = vt_ref[h * HEAD_DIM:(h + 1) * HEAD_DIM, pl.ds(starts[g], K_TILE)]
                total = total + jnp.dot(vt, a.astype(BF16), preferred_element_type=F32)
            new_acc.append(total)
        return tuple(tails), tuple(new_acc)

    init = (tuple(jnp.zeros((1, qb), F32) for _ in range(N_HEADS)),
            tuple(jnp.zeros((HEAD_DIM, qb), F32) for _ in range(N_HEADS)))
    carry = step(0, init, True)

    def more(state):
        u, (tails, _) = state
        live = tails[0]
        for h in range(1, N_HEADS):
            live = jnp.maximum(live, tails[h])
        return (u < (i + 2) // 2) & (jnp.max(live) > EXP_ZERO_BELOW)

    _, (_, acc) = lax.while_loop(more, lambda st: (st[0] + 1, step(st[0], st[1], False)), (jnp.int32(1), carry))
    o_ref[...] = jnp.concatenate(acc, axis=0).T.astype(o_ref.dtype)


def _stick_breaking(qc, kc, vct, batch, seq):
    n = qc.shape[0]
    nq = seq // Q_BLOCK
    qrow = pl.BlockSpec((Q_BLOCK, BRANCH_WIDTH), lambda b, i: (b * nq + i, 0))
    return pl.pallas_call(
        _stick_kernel,
        grid=(batch, nq),
        in_specs=[qrow, pl.BlockSpec((seq, BRANCH_WIDTH), lambda b, i: (b, 0)),
                  pl.BlockSpec((BRANCH_WIDTH, seq), lambda b, i: (0, b))],
        out_specs=qrow,
        out_shape=jax.ShapeDtypeStruct((n, BRANCH_WIDTH), BF16),
        compiler_params=_params("arbitrary", "arbitrary"),
        name="stick_breaking",
    )(qc, kc, vct)


def _merge_kernel(x_ref, g1_ref, ya_ref, yb_ref, yc_ref, yd_ref, wg_ref, wb_ref, wo_ref,
                  g2_ref, wr_ref, br_ref, x1_o, comb_o):
    x = x_ref[...]
    d = x.shape[1]
    h = _rms_bf16(x, g1_ref[...])
    ys = (ya_ref[...], yb_ref[...], yc_ref[...], yd_ref[...])
    col = 256
    pieces = []
    for c in range(d // col):
        m = None
        for n in range(N_BRANCH):
            logit = jnp.dot(h, wg_ref[:, n * d + c * col:n * d + (c + 1) * col],
                            preferred_element_type=F32)
            br = jnp.dot(ys[n], wb_ref[n, :, c * col:(c + 1) * col], preferred_element_type=F32)
            t = jax.nn.sigmoid(logit) * br
            m = t if m is None else m + t
        pieces.append(m.astype(BF16))
    merged = jnp.concatenate(pieces, axis=1)
    x1 = x + jnp.dot(merged, wo_ref[...], preferred_element_type=F32)
    x1_o[...] = x1

    ms = jnp.mean(x1 * x1, axis=-1, keepdims=True)
    h2 = x1 * lax.rsqrt(ms + EPS) * g2_ref[...]
    h_hi = h2.astype(BF16)
    h_lo = (h2 - h_hi.astype(F32)).astype(BF16)
    both = jnp.dot(h_hi, wr_ref[...], preferred_element_type=F32)
    cross = jnp.dot(h_lo, wr_ref[:, 0:LANES], preferred_element_type=F32)
    logits = both[:, 0:LANES] + both[:, LANES:2 * LANES] + cross + br_ref[...]
    rows = logits.shape[0]
    lane = lax.broadcasted_iota(jnp.int32, (rows, LANES), 1)
    neg = -jnp.inf
    is_grp = (lane >= N_EXPERTS) & (lane < N_EXPERTS + N_EXPERT_GROUPS)
    gl = jnp.where(is_grp, logits, neg)
    gmax = jnp.max(gl, axis=1, keepdims=True)
    gidx = jnp.min(jnp.where(gl == gmax, lane, LANES), axis=1, keepdims=True) - N_EXPERTS
    grp_p = 1.0 / jnp.sum(jnp.where(is_grp, jnp.exp(gl - gmax), 0.0), axis=1, keepdims=True)
    in_grp = (lane >= gidx * EXPERTS_PER_GROUP) & (lane < (gidx + 1) * EXPERTS_PER_GROUP)
    el = jnp.where(in_grp, logits, neg)
    l1 = jnp.max(el, axis=1, keepdims=True)
    j1 = jnp.min(jnp.where(el == l1, lane, LANES), axis=1, keepdims=True)
    el2 = jnp.where(lane == j1, neg, el)
    l2 = jnp.max(el2, axis=1, keepdims=True)
    j2 = jnp.min(jnp.where(el2 == l2, lane, LANES), axis=1, keepdims=True)
    e2 = jnp.exp(l2 - l1)
    w1 = grp_p / (1.0 + e2)
    w2 = grp_p * e2 / (1.0 + e2)
    comb_o[...] = jnp.where(lane == j1, w1, 0.0) + jnp.where(lane == j2, w2, 0.0)


def _merge(x2, norm1_g, ys, w_gate, w_branch, w_out, norm2_g, w_router, b_router):
    n, d = x2.shape
    rows = MERGE_ROWS
    row = lambda w: pl.BlockSpec((rows, w), lambda i: (i, 0))
    full = lambda a: pl.BlockSpec(a.shape, lambda i: (0,) * a.ndim)
    args = (x2, norm1_g.reshape(1, d), *ys, w_gate, w_branch, w_out, norm2_g.reshape(1, d),
            w_router, b_router)
    in_specs = [row(d), full(args[1])] + [row(BRANCH_WIDTH)] * 4 + [full(a) for a in args[6:]]
    return pl.pallas_call(
        _merge_kernel,
        grid=(n // rows,),
        in_specs=in_specs,
        out_specs=[row(d), row(LANES)],
        out_shape=[jax.ShapeDtypeStruct((n, d), F32), jax.ShapeDtypeStruct((n, LANES), F32)],
        compiler_params=_params("arbitrary"),
        name="merge_router",
    )(*args)


def _moe_kernel(x_ref, g2_ref, comb_ref, wgu_ref, wd_ref, gf_ref, o_ref, h_scr, acc_scr, *, final_norm):
    step = pl.program_id(1)

    @pl.when(step == 0)
    def _():
        h_scr[...] = _rms_bf16(x_ref[...], g2_ref[...])
        acc_scr[...] = jnp.zeros(acc_scr.shape, F32)

    h = h_scr[...]
    comb = comb_ref[...]
    lane = lax.broadcasted_iota(jnp.int32, comb.shape, 1)
    total = None
    for j in range(EXPERTS_PER_STEP):
        e = step * EXPERTS_PER_STEP + j
        gu = jnp.dot(h, wgu_ref[j], preferred_element_type=F32)
        w = jnp.sum(jnp.where(lane == e, comb, 0.0), axis=1, keepdims=True)
        hh = (jax.nn.silu(gu[:, :D_EXPERT]) * gu[:, D_EXPERT:] * w).astype(BF16)
        t = jnp.dot(hh, wd_ref[j], preferred_element_type=F32)
        total = t if total is None else total + t
    acc_scr[...] += total

    @pl.when(step == pl.num_programs(1) - 1)
    def _():
        y = x_ref[...] + acc_scr[...]
        if final_norm:
            ms = jnp.mean(y * y, axis=-1, keepdims=True)
            y = y * lax.rsqrt(ms + EPS) * gf_ref[...]
        o_ref[...] = y


def _moe(x1, norm2_g, comb, w_gu, w_d, norm_f_g, final_norm):
    n, d = x1.shape
    rows = min(MOE_ROWS, n)
    n_exp = w_gu.shape[0]
    row = lambda w: pl.BlockSpec((rows, w), lambda i, e: (i, 0))
    vec = pl.BlockSpec((1, d), lambda i, e: (0, 0))
    return pl.pallas_call(
        functools.partial(_moe_kernel, final_norm=final_norm),
        grid=(n // rows, n_exp // EXPERTS_PER_STEP),
        in_specs=[row(d), vec, row(LANES),
                  pl.BlockSpec((EXPERTS_PER_STEP, d, 2 * D_EXPERT), lambda i, e: (e, 0, 0)),
                  pl.BlockSpec((EXPERTS_PER_STEP, D_EXPERT, d), lambda i, e: (e, 0, 0)),
                  vec],
        out_specs=row(d),
        out_shape=jax.ShapeDtypeStruct((n, d), F32),
        scratch_shapes=[pltpu.VMEM((rows, d), BF16), pltpu.VMEM((rows, d), F32)],
        compiler_params=_params("arbitrary", "arbitrary"),
        name="experts",
    )(x1, norm2_g.reshape(1, d), comb, w_gu, w_d, norm_f_g.reshape(1, d))


def kernel(x, positions, norm1_g, w_in, sgu_ln_g, sgu_ln_b, w_spatial, b_spatial, conv_w, w_branch,
           w_out, norm2_g, w_router_group, b_router_group, w_router_expert, b_router_expert,
           w_exp_gate, w_exp_up, w_exp_down, norm_f_g):
    batch, seq, d = x.shape
    depth = w_in.shape[0]
    n = batch * seq
    assert TILES_PER_STEP == 2
    assert seq % COUNT_CHUNK == 0 and seq % (TILES_PER_STEP * K_TILE) == 0
    assert n % PROJ_ROWS == 0 and n % MERGE_ROWS == 0
    x2 = x.reshape(n, d)
    tables = _rope_tables(positions)
    for l in range(depth):
        w_a, w_t, gate_off = _proj_weight(w_in[l])
        w_gate = w_in[l][:, gate_off:].astype(BF16)
        pad = jnp.zeros((d, LANES - N_EXPERTS - N_EXPERT_GROUPS), F32)
        w_router = jnp.concatenate([w_router_expert[l], w_router_group[l], pad], axis=1)
        w_router_hi = w_router.astype(BF16)
        w_router_lo = (w_router - w_router_hi.astype(F32)).astype(BF16)
        w_router = jnp.concatenate([w_router_hi, w_router_lo], axis=1)
        b_router = jnp.concatenate([b_router_expert[l], b_router_group[l], pad[0]]).reshape(1, LANES)
        qa, ka, vat, iq, ik, iwt, y_b, qc, kc, vct, y_d = _project(
            x2, norm1_g[l], w_a, w_t, tables, sgu_ln_g[l], sgu_ln_b[l], w_spatial[l], b_spatial[l],
            conv_w[l], seq)
        y_a = _dsa(iq, iwt, qa, ik, ka, vat, batch, seq)
        y_c = _stick_breaking(qc, kc, vct, batch, seq)
        x1, comb = _merge(x2, norm1_g[l], (y_a, y_b, y_c, y_d), w_gate, w_branch[l].astype(BF16),
                          w_out[l].astype(BF16), norm2_g[l], w_router, b_router)
        w_gu = jnp.concatenate([w_exp_gate[l], w_exp_up[l]], axis=2).astype(BF16)
        x2 = _moe(x1, norm2_g[l], comb, w_gu, w_exp_down[l].astype(BF16), norm_f_g,
                  final_norm=(l == depth - 1))
    return x2.reshape(batch, seq, d)
```

```python
import functools

import jax
import jax.numpy as jnp
from jax import lax
from jax.experimental import pallas as pl
from jax.experimental.pallas import tpu as pltpu

F32 = jnp.float32
BF16 = jnp.bfloat16

EPS = 1e-6
ROPE_THETA = 10000.0
HEAD_DIM = 64
BRANCH_WIDTH = 256
N_HEADS = 4
IDX_DIM = 32
TOPK_MAX = 256
CHUNK = 128
N_GROUPS_B = 4
CONV_WIDTH = 3
N_BRANCH = 4
N_EXPERT_GROUPS = 4
EXPERTS_PER_GROUP = 8
N_EXPERTS = N_EXPERT_GROUPS * EXPERTS_PER_GROUP
D_EXPERT = 256

LANES = 128
SUBLANES = 8
Q_BLOCK = 256
K_TILE = Q_BLOCK
TILES_PER_STEP = 2
COUNT_CHUNK = 512
COUNT_ROWS = 64
PROJ_ROWS = 256
MERGE_ROWS = 512
MOE_ROWS = 1024
EXPERTS_PER_STEP = 4
CONV_HALO = 8
VMEM_LIMIT = 56 * 1024 * 1024
PHANTOM_LOGIT = -1e4
NEG_BIG = -1e30
EXP_ZERO_BELOW = -110.0
INT_MIN = -2 ** 31
HALF_RANGE = 2 ** 15
ONES_ROWS = 16
BOUND_MARGIN = 1.02
BOUND_LIMIT = 40.0

_OFF = {}
_o = 0
for _name, _w in (("q", 512), ("k", 512), ("iq", 256), ("ik", 256),
                  ("ub", 256), ("vb", 256), ("qc", 256), ("kc", 256),
                  ("bd", 256), ("cd", 256), ("xd", 256)):
    _OFF[_name] = (_o, _w)
    _o += _w
W_A_COLS = _o
T_VA, T_VC, T_IW = 0, BRANCH_WIDTH, 2 * BRANCH_WIDTH
W_T_ROWS = 2 * BRANCH_WIDTH + 16

_CONTRACT_LAST = (((1,), (1,)), ((), ()))


def _params(*sem):
    return pltpu.CompilerParams(dimension_semantics=sem, vmem_limit_bytes=VMEM_LIMIT)


def _rope_table_kernel(pos_ref, inv_a_ref, inv_i_ref, cos_a, sin_a, cos_i, sin_i):
    p = pos_ref[...]
    a = p * inv_a_ref[...]
    cos_a[...] = jnp.cos(a)
    sin_a[...] = jnp.sin(a)
    b = p * inv_i_ref[...]
    cos_i[...] = jnp.cos(b)
    sin_i[...] = jnp.sin(b)


def _rope_tables(positions):
    n = positions.size
    rows = PROJ_ROWS
    pos = positions.astype(F32).reshape(n, 1)
    inv_a = ROPE_THETA ** (-jnp.arange(0, HEAD_DIM, 2, dtype=F32) / HEAD_DIM)
    inv_i = ROPE_THETA ** (-jnp.arange(0, IDX_DIM, 2, dtype=F32) / IDX_DIM)
    inv_a = jnp.tile(inv_a, LANES // inv_a.size).reshape(1, LANES)
    inv_i = jnp.tile(inv_i, LANES // inv_i.size).reshape(1, LANES)
    tab = jax.ShapeDtypeStruct((n, LANES), F32)
    row_spec = pl.BlockSpec((rows, LANES), lambda i: (i, 0))
    const = pl.BlockSpec((1, LANES), lambda i: (0, 0))
    return pl.pallas_call(
        _rope_table_kernel,
        grid=(n // rows,),
        in_specs=[pl.BlockSpec((rows, 1), lambda i: (i, 0)), const, const],
        out_specs=[row_spec] * 4,
        out_shape=[tab] * 4,
        compiler_params=_params("arbitrary"),
        name="rope_tables",
    )(pos, inv_a, inv_i)


def _rms_bf16(x, g):
    ms = jnp.mean(x * x, axis=-1, keepdims=True)
    return (x * lax.rsqrt(ms + EPS) * g).astype(BF16)


def _proj_kernel(x_ref, g_ref, w_ref, wt_ref, cos_a, sin_a, cos_i, sin_i, lng_ref, lnb_ref, ws_ref,
                 bs_ref, cw_ref,
                 qa_o, ka_o, vat_o, iq_o, ik_o, iwt_o, yb_o, qc_o, kc_o, vct_o, yd_o,
                 zbuf, *, steps_per_seq):
    rows = x_ref.shape[0]
    h = _rms_bf16(x_ref[...], g_ref[...])

    def mm(name):
        off, width = _OFF[name]
        return jnp.dot(h, w_ref[:, off:off + width], preferred_element_type=F32)

    def rope(name, cos_ref, sin_ref, out_ref, scale):
        p = mm(name)
        w = p.shape[1] // 2
        c = cos_ref[...]
        s = sin_ref[...]
        for j in range(w // LANES):
            lo = j * LANES
            r = p[:, lo:lo + LANES] * c + p[:, w + lo:w + lo + LANES] * s
            out_ref[:, lo:lo + LANES] = (r if scale == 1.0 else r * scale).astype(out_ref.dtype)

    att_scale = HEAD_DIM ** -0.5
    rope("q", cos_a, sin_a, qa_o, att_scale)
    rope("k", cos_a, sin_a, ka_o, 1.0)
    rope("iq", cos_i, sin_i, iq_o, 1.0)
    rope("ik", cos_i, sin_i, ik_o, 1.0)
    qc_o[...] = (mm("qc") * att_scale).astype(BF16)
    kc_o[...] = mm("kc").astype(BF16)
    t = lax.dot_general(wt_ref[...], h, _CONTRACT_LAST, preferred_element_type=F32)
    vat_o[...] = t[T_VA:T_VA + BRANCH_WIDTH, :].astype(BF16)
    vct_o[...] = t[T_VC:T_VC + BRANCH_WIDTH, :].astype(BF16)
    iwt_o[...] = t[T_IW:T_IW + SUBLANES, :]

    u = jax.nn.gelu(mm("ub"))
    v = jax.nn.gelu(mm("vb"))
    mu = jnp.mean(v, axis=-1, keepdims=True)
    var = jnp.mean(jnp.square(v - mu), axis=-1, keepdims=True)
    vn = ((v - mu) * lax.rsqrt(var + EPS) * lng_ref[...] + lnb_ref[...]).astype(BF16)
    t_idx = lax.broadcasted_iota(jnp.int32, (CHUNK, CHUNK), 0)
    s_idx = lax.broadcasted_iota(jnp.int32, (CHUNK, CHUNK), 1)
    causal = s_idx <= t_idx
    group_masks = [_head_lane_mask(BRANCH_WIDTH, BRANCH_WIDTH // N_GROUPS_B, g) for g in range(N_GROUPS_B)]
    w_tril = [jnp.where(causal, ws_ref[g], 0.0).astype(BF16) for g in range(N_GROUPS_B)]
    for c in range(rows // CHUNK):
        vc = vn[c * CHUNK:(c + 1) * CHUNK, :]
        mixed = bs_ref[...]
        for g in range(N_GROUPS_B):
            full = jnp.dot(w_tril[g], vc, preferred_element_type=F32)
            mixed = mixed + jnp.where(group_masks[g], full, 0.0)
        yb_o[c * CHUNK:(c + 1) * CHUNK, :] = (u[c * CHUNK:(c + 1) * CHUNK, :] * mixed).astype(BF16)

    bd = mm("bd")
    z = mm("cd") * mm("xd")

    @pl.when(pl.program_id(0) % steps_per_seq == 0)
    def _():
        zbuf[0:CONV_HALO, :] = jnp.zeros((CONV_HALO, BRANCH_WIDTH), F32)

    zbuf[CONV_HALO:CONV_HALO + rows, :] = z
    y = cw_ref[CONV_WIDTH - 1:CONV_WIDTH, :] * z
    for tap in range(CONV_WIDTH - 1):
        shift = CONV_WIDTH - 1 - tap
        y = y + cw_ref[tap:tap + 1, :] * zbuf[CONV_HALO - shift:CONV_HALO - shift + rows, :]
    yd_o[...] = (bd * y).astype(BF16)
    zbuf[0:CONV_HALO, :] = z[rows - CONV_HALO:rows, :]


def _swap_halves(w, dim):
    d_in, cols = w.shape
    wh = w.reshape(d_in, cols // dim, 2, dim // 2)
    return jnp.stack([-wh[:, :, 1], wh[:, :, 0]], axis=2).reshape(d_in, cols)


def _proj_weight(w_in):
    d = w_in.shape[0]
    sizes = (BRANCH_WIDTH,) * 3 + (N_HEADS * IDX_DIM, IDX_DIM, N_HEADS) + (BRANCH_WIDTH,) * 8
    parts, o = [], 0
    for s in sizes:
        parts.append(w_in[:, o:o + s])
        o += s
    qa, ka, va, iq, ik, iw, ub, vb, qc, kc, vc, bd, cd, xd = parts
    ik_rep = jnp.tile(ik, (1, N_HEADS))
    cols = [qa, _swap_halves(qa, HEAD_DIM), ka, _swap_halves(ka, HEAD_DIM),
            iq, _swap_halves(iq, IDX_DIM), ik_rep, _swap_halves(ik_rep, IDX_DIM),
            ub, vb, qc, kc, bd, cd, xd]
    iw_pad = jnp.concatenate([iw, jnp.zeros((d, W_T_ROWS - 2 * BRANCH_WIDTH - N_HEADS), w_in.dtype)], axis=1)
    w_t = jnp.concatenate([va, vc, iw_pad], axis=1).T
    return jnp.concatenate(cols, axis=1).astype(BF16), w_t.astype(BF16), o


def _project(x2, norm_g, w_a, w_t, tables, ln_g, ln_b, w_spatial, b_spatial, conv_w, seq):
    n, d = x2.shape
    rows = PROJ_ROWS
    cos_a, sin_a, cos_i, sin_i = tables
    bias = jnp.repeat(b_spatial.T, BRANCH_WIDTH // N_GROUPS_B, axis=1)
    row = lambda w: pl.BlockSpec((rows, w), lambda i: (i, 0))
    col = lambda r: pl.BlockSpec((r, rows), lambda i: (0, i))
    full = lambda a: pl.BlockSpec(a.shape, lambda i: (0,) * a.ndim)
    bf = lambda w: jax.ShapeDtypeStruct((n, w), BF16)
    bft = jax.ShapeDtypeStruct((BRANCH_WIDTH, n), BF16)
    args = (x2, norm_g.reshape(1, d), w_a, w_t, cos_a, sin_a, cos_i, sin_i, ln_g.reshape(1, -1),
            ln_b.reshape(1, -1), w_spatial, bias, conv_w)
    in_specs = [row(d), full(args[1]), full(w_a), full(w_t), row(LANES), row(LANES), row(LANES),
                row(LANES), full(args[8]), full(args[9]), full(w_spatial), full(bias), full(conv_w)]
    out_shape = [bf(256), bf(256), bft, bf(128), bf(128), jax.ShapeDtypeStruct((SUBLANES, n), F32),
                 bf(256), bf(256), bf(256), bft, bf(256)]
    out_specs = [row(256), row(256), col(BRANCH_WIDTH), row(128), row(128), col(SUBLANES),
                 row(256), row(256), row(256), col(BRANCH_WIDTH), row(256)]
    return pl.pallas_call(
        functools.partial(_proj_kernel, steps_per_seq=seq // rows),
        grid=(n // rows,),
        in_specs=in_specs,
        out_specs=out_specs,
        out_shape=out_shape,
        scratch_shapes=[pltpu.VMEM((CONV_HALO + rows, BRANCH_WIDTH), F32)],
        compiler_params=_params("arbitrary"),
        name="project",
    )(*args)


def _head_lane_mask(width, per_head, h, rows=1):
    lane = lax.broadcasted_iota(jnp.int32, (rows, width), 1)
    return (lane >= h * per_head) & (lane < (h + 1) * per_head)


def _masked_heads(x, per_head):
    return [jnp.where(_head_lane_mask(x.shape[1], per_head, h), x, jnp.zeros_like(x))
            for h in range(N_HEADS)]


def _float_to_key(x):
    bits = lax.bitcast_convert_type(x, jnp.int32)
    sign = bits >> 31
    return (bits ^ (sign & jnp.int32(0x7FFFFFFF))) - sign


def _key_to_float(key):
    bits = jnp.where(key < 0, (-key) | jnp.int32(INT_MIN), key)
    return lax.bitcast_convert_type(bits, F32)


def _neg_abs(x):
    bits = lax.bitcast_convert_type(x, jnp.int32) | jnp.int32(INT_MIN)
    return lax.bitcast_convert_type(bits, F32)


def _col_reduce(x, op, final):
    while x.shape[0] > SUBLANES:
        half = x.shape[0] // 2
        x = op(x[:half], x[half:])
    return final(x, axis=0, keepdims=True)


def _col_sum(x):
    return _col_reduce(x, jnp.add, jnp.sum)


def _col_max(x):
    return _col_reduce(x, jnp.maximum, jnp.max)


def _step_starts(step):
    return [pl.multiple_of((step * TILES_PER_STEP + g) * K_TILE, K_TILE) for g in range(TILES_PER_STEP)]


def _dsa_kernel(iq_ref, iwt_ref, q_ref, ik_ref, k_ref, vt_ref, o_ref, sc_ref, hi_ref, lo_ref, kmax_ref, *,
                topk):
    qb = q_ref.shape[0]
    i = pl.program_id(1)
    n_tiles = i + 1
    n_steps = (n_tiles + TILES_PER_STEP - 1) // TILES_PER_STEP

    @pl.when(i == 0)
    def _():
        sc_ref[...] = jnp.full(sc_ref.shape, -jnp.inf, F32)
        hi_ref[...] = jnp.full(hi_ref.shape, -HALF_RANGE, jnp.int16)
        lo_ref[...] = jnp.full(lo_ref.shape, -HALF_RANGE, jnp.int16)

    iq_heads = _masked_heads(iq_ref[...], IDX_DIM)
    iwt = iwt_ref[...]
    iw_heads = [iwt[h:h + 1, :] for h in range(N_HEADS)]

    def put_scores(start, sc):
        sc_ref[pl.ds(start, K_TILE), :] = sc
        key = _float_to_key(sc)
        hi_ref[pl.ds(start, K_TILE), :] = (key >> 16).astype(jnp.int16)
        lo_ref[pl.ds(start, K_TILE), :] = ((key & 0xFFFF) - HALF_RANGE).astype(jnp.int16)

    key_off = lax.broadcasted_iota(jnp.int32, (K_TILE, qb), 0)
    qry_pos = i * qb + lax.broadcasted_iota(jnp.int32, (K_TILE, qb), 1)

    def score_step(jj, masked):
        starts = _step_starts(jj)
        dots = [[lax.dot_general(ik_ref[pl.ds(st, K_TILE), :], iq_heads[h], _CONTRACT_LAST,
                                 preferred_element_type=F32) for h in range(N_HEADS)] for st in starts]
        for g in range(TILES_PER_STEP):
            sc = None
            for h in range(N_HEADS):
                t = iw_heads[h] * jnp.maximum(dots[g][h], 0.0)
                sc = t if sc is None else sc + t
            if masked:
                sc = jnp.where(starts[g] + key_off <= qry_pos, sc, -jnp.inf)
            put_scores(starts[g], sc)

    def full_step(jj, carry):
        score_step(jj, False)
        return carry

    lax.fori_loop(0, i // TILES_PER_STEP, full_step, 0)
    score_step(i // TILES_PER_STEP, True)

    n_chunks = (n_tiles * K_TILE + COUNT_CHUNK - 1) // COUNT_CHUNK
    kf = jnp.float32(topk)

    def chunk_rows(c):
        base = pl.multiple_of(c * COUNT_CHUNK, COUNT_CHUNK)
        return [pl.ds(base + r * COUNT_ROWS, COUNT_ROWS) for r in range(COUNT_CHUNK // COUNT_ROWS)]

    def count(ref, pred, dtype):
        one, zero = jnp.ones((), dtype), jnp.zeros((), dtype)

        def body(c, acc):
            for rows in chunk_rows(c):
                acc = acc + jnp.where(pred(ref[rows, :]), one, zero)
            return acc
        acc = lax.fori_loop(0, n_chunks, body, jnp.zeros((COUNT_ROWS, qb), dtype))
        return jnp.sum(acc.astype(F32), axis=0, keepdims=True)

    def bisect16(ref, target):
        def bit_step(b, carry):
            best, cnt_best, cnt_above = carry
            cand = best + lax.shift_left(jnp.int32(1), 15 - b)
            cand16 = cand.astype(jnp.int16)
            cnt = count(ref, lambda blk: blk >= cand16, jnp.int16)
            ok = cnt >= target
            return jnp.where(ok, cand, best), jnp.where(ok, cnt, cnt_best), jnp.where(ok, cnt_above, cnt)
        init = (jnp.full((1, qb), -HALF_RANGE, jnp.int32), jnp.zeros((1, qb), F32), jnp.zeros((1, qb), F32))
        return lax.fori_loop(0, 16, bit_step, init)

    hi_t, cnt_hi, above = bisect16(hi_ref, kf)
    hi_t16 = hi_t.astype(jnp.int16)

    def mask_low(c, carry):
        for rows in chunk_rows(c):
            lo_ref[rows, :] = jnp.where(hi_ref[rows, :] == hi_t16, lo_ref[rows, :], jnp.int16(-HALF_RANGE))
        return carry

    lax.fori_loop(0, n_chunks, mask_low, 0)
    few = i * qb + lax.broadcasted_iota(jnp.int32, (1, qb), 1) + 1 < topk

    target_lo = kf - above
    lowest = jnp.int16(1 - HALF_RANGE)
    settled0 = few | (count(lo_ref, lambda blk: blk >= lowest, jnp.int16) < target_lo)

    def lo_more(st):
        b, _, _, settled = st
        return (b < 16) & (jnp.min(settled) < 1.0)

    def lo_step(st):
        b, best, cnt_best, settled = st
        cand = best + lax.shift_left(jnp.int32(1), 15 - b)
        cand16 = cand.astype(jnp.int16)
        cnt = count(lo_ref, lambda blk: blk >= cand16, jnp.int16)
        ok = (cnt >= target_lo) & (settled < 1.0)
        return (b + 1, jnp.where(ok, cand, best), jnp.where(ok, cnt, cnt_best),
                jnp.where(ok & (cnt == target_lo), 1.0, settled))

    _, lo_t, cnt_lo, _ = lax.while_loop(
        lo_more, lo_step,
        (jnp.int32(0), jnp.full((1, qb), -HALF_RANGE, jnp.int32), jnp.zeros((1, qb), F32),
         jnp.where(settled0, 1.0, 0.0)))
    cnt_key = above + jnp.where(lo_t == -HALF_RANGE, cnt_hi - above, cnt_lo)
    key = hi_t * (2 * HALF_RANGE) + (lo_t + HALF_RANGE)
    thr = jnp.where(few, jnp.finfo(F32).min, _key_to_float(key))

    @pl.when(jnp.max(jnp.where(few, kf, cnt_key)) > kf)
    def _():
        need = kf - count(sc_ref, lambda blk: blk > thr, F32)
        before = (lax.broadcasted_iota(jnp.int32, (K_TILE, K_TILE), 1)
                  < lax.broadcasted_iota(jnp.int32, (K_TILE, K_TILE), 0))
        before = jnp.where(before, 1.0, 0.0).astype(BF16)

        def tie_body(jj, seen):
            starts = _step_starts(jj)
            blks = [sc_ref[pl.ds(st, K_TILE), :] for st in starts]
            eqs = [blk == thr for blk in blks]
            eq_fs = [jnp.where(eq, 1.0, 0.0) for eq in eqs]
            ranks = [jnp.dot(before, eq_f.astype(BF16), preferred_element_type=F32) for eq_f in eq_fs]
            for g in range(TILES_PER_STEP):
                drop = eqs[g] & (seen + ranks[g] >= need)
                sc_ref[pl.ds(starts[g], K_TILE), :] = jnp.where(drop, -jnp.inf, blks[g])
                seen = seen + _col_sum(eq_fs[g])
            return seen

        lax.fori_loop(0, n_steps, tie_body, jnp.zeros((1, qb), F32))

    q = q_ref[...]
    q_heads = _masked_heads(q, HEAD_DIM)
    head_rows = jnp.concatenate(
        [jnp.where(_head_lane_mask(BRANCH_WIDTH, HEAD_DIM, h), 1.0, 0.0) for h in range(N_HEADS)]
        + [jnp.zeros((ONES_ROWS - N_HEADS, BRANCH_WIDTH), F32)], axis=0).astype(BF16)
    qf = q.astype(F32)
    q_sq = lax.dot_general(head_rows, (qf * qf).astype(BF16), _CONTRACT_LAST,
                           preferred_element_type=F32)
    kf32 = k_ref[pl.ds(pl.multiple_of(i * K_TILE, K_TILE), K_TILE), :].astype(F32)
    k_sq = lax.dot_general(head_rows, (kf32 * kf32).astype(BF16), _CONTRACT_LAST,
                           preferred_element_type=F32)
    k_sq_max = jnp.max(k_sq, axis=1, keepdims=True)

    @pl.when(i == 0)
    def _():
        kmax_ref[...] = jnp.zeros(kmax_ref.shape, F32)

    kmax_ref[...] = jnp.maximum(kmax_ref[...], jnp.broadcast_to(k_sq_max, kmax_ref.shape))
    bound = jnp.sqrt(q_sq * kmax_ref[:, 0:1]) * BOUND_MARGIN
    bounds = [bound[h:h + 1, :] for h in range(N_HEADS)]
    ones_rows = jnp.ones((ONES_ROWS, K_TILE), BF16)

    def step_logits(jj):
        starts = _step_starts(jj)
        sels = [sc_ref[pl.ds(st, K_TILE), :] >= thr for st in starts]
        logits = [[lax.dot_general(k_ref[pl.ds(st, K_TILE), :], q_heads[h], _CONTRACT_LAST,
                                   preferred_element_type=F32) for h in range(N_HEADS)] for st in starts]
        masked = [[jnp.where(sels[g], logits[g][h], NEG_BIG) for g in range(TILES_PER_STEP)]
                  for h in range(N_HEADS)]
        return starts, masked

    def value_products(starts, probs, h):
        pv = None
        for g, st in enumerate(starts):
            vt = vt_ref[h * HEAD_DIM:(h + 1) * HEAD_DIM, pl.ds(st, K_TILE)]
            t = jnp.dot(vt, probs[g], preferred_element_type=F32)
            pv = t if pv is None else pv + t
        return pv

    def bounded_step(jj, carry):
        l, acc = carry
        starts, masked = step_logits(jj)
        probs = [[jnp.exp(s - bounds[h]).astype(BF16) for s in masked[h]] for h in range(N_HEADS)]
        l_new, acc_new = [], []
        for h in range(N_HEADS):
            lh = l[h]
            for pb in probs[h]:
                lh = lh + jnp.dot(ones_rows, pb, preferred_element_type=F32)[0:1]
            l_new.append(lh)
            acc_new.append(acc[h] + value_products(starts, probs[h], h))
        return tuple(l_new), tuple(acc_new)

    def online_step(jj, carry):
        m, l, acc = carry
        starts, masked = step_logits(jj)
        m_new, l_new, acc_new = [], [], []
        for h in range(N_HEADS):
            mh = m[h]
            for s in masked[h]:
                mh = jnp.maximum(mh, _col_max(s))
            ps = [jnp.exp(s - mh) for s in masked[h]]
            alpha = jnp.exp(m[h] - mh)
            lh = alpha * l[h]
            for p in ps:
                lh = lh + _col_sum(p)
            m_new.append(mh)
            l_new.append(lh)
            acc_new.append(alpha * acc[h] + value_products(starts, [p.astype(BF16) for p in ps], h))
        return tuple(m_new), tuple(l_new), tuple(acc_new)

    def finish(l, acc):
        out_t = jnp.concatenate([acc[h] / l[h] for h in range(N_HEADS)], axis=0)
        o_ref[...] = out_t.T.astype(o_ref.dtype)

    zeros_l = tuple(jnp.zeros((1, qb), F32) for _ in range(N_HEADS))
    zeros_acc = tuple(jnp.zeros((HEAD_DIM, qb), F32) for _ in range(N_HEADS))
    small = jnp.max(bound[0:N_HEADS, :]) <= BOUND_LIMIT

    @pl.when(small)
    def _():
        finish(*lax.fori_loop(0, n_steps, bounded_step, (zeros_l, zeros_acc)))

    @pl.when(jnp.logical_not(small))
    def _():
        init = (tuple(jnp.full((1, qb), NEG_BIG, F32) for _ in range(N_HEADS)), zeros_l, zeros_acc)
        _, l, acc = lax.fori_loop(0, n_steps, online_step, init)
        finish(l, acc)


def _dsa(iq, iwt, qa, ik, ka, vat, batch, seq):
    n = qa.shape[0]
    nq = seq // Q_BLOCK
    topk = min(TOPK_MAX, seq // 4)
    qrow = lambda w: pl.BlockSpec((Q_BLOCK, w), lambda b, i: (b * nq + i, 0))
    whole = lambda w: pl.BlockSpec((seq, w), lambda b, i: (b, 0))
    return pl.pallas_call(
        functools.partial(_dsa_kernel, topk=topk),
        grid=(batch, nq),
        in_specs=[qrow(LANES), pl.BlockSpec((SUBLANES, Q_BLOCK), lambda b, i: (0, b * nq + i)),
                  qrow(BRANCH_WIDTH), whole(LANES), whole(BRANCH_WIDTH),
                  pl.BlockSpec((BRANCH_WIDTH, seq), lambda b, i: (0, b))],
        out_specs=qrow(BRANCH_WIDTH),
        out_shape=jax.ShapeDtypeStruct((n, BRANCH_WIDTH), BF16),
        scratch_shapes=[pltpu.VMEM((seq, Q_BLOCK), F32), pltpu.VMEM((seq, Q_BLOCK), jnp.int16),
                        pltpu.VMEM((seq, Q_BLOCK), jnp.int16), pltpu.VMEM((ONES_ROWS, LANES), F32)],
        compiler_params=_params("arbitrary", "arbitrary"),
        name="dsa",
    )(iq, iwt, qa, ik, ka, vat)


def _stick_kernel(q_ref, k_ref, vt_ref, o_ref):
    qb = q_ref.shape[0]
    i = pl.program_id(1)
    q_heads = _masked_heads(q_ref[...], HEAD_DIM)
    after = (lax.broadcasted_iota(jnp.int32, (K_TILE, K_TILE), 1)
             > lax.broadcasted_iota(jnp.int32, (K_TILE, K_TILE), 0))
    after = jnp.where(after, 1.0, 0.0).astype(BF16)
    after = jnp.concatenate([after, jnp.ones((ONES_ROWS, K_TILE), BF16)], axis=0)
    key_off = lax.broadcasted_iota(jnp.int32, (K_TILE, qb), 0)
    qry_pos = i * qb + lax.broadcasted_iota(jnp.int32, (K_TILE, qb), 1)

    def step(u, carry, masked):
        tails, acc = carry
        first = i - 2 * u
        tiles = [first, jnp.maximum(first - 1, 0)]
        bias = [None, jnp.where(first >= 1, 0.0, PHANTOM_LOGIT)]
        starts = [pl.multiple_of(t * K_TILE, K_TILE) for t in tiles]
        stricts = [starts[g] + key_off < qry_pos if masked else None for g in range(2)]
        zs = [[lax.dot_general(k_ref[pl.ds(st, K_TILE), :], q_heads[h], _CONTRACT_LAST,
                               preferred_element_type=F32) for h in range(N_HEADS)] for st in starts]
        log_betas, splits = [], []
        for g in range(2):
            lbs, sps = [], []
            for h in range(N_HEADS):
                z = zs[g][h] if bias[g] is None else zs[g][h] + bias[g]
                log_beta = jnp.minimum(z, 0.0) - jnp.log(1.0 + jnp.exp(_neg_abs(z)))
                log_keep = log_beta - z
                if masked:
                    log_keep = jnp.where(stricts[g], log_keep, 0.0)
                lbs.append(log_beta)
                sps.append(log_keep.astype(BF16))
            log_betas.append(lbs)
            splits.append(sps)
        sums = [[jnp.dot(after, splits[g][h], preferred_element_type=F32) for h in range(N_HEADS)]
                for g in range(2)]
        tails = list(tails)
        new_acc = []
        for h in range(N_HEADS):
            total = acc[h]
            for g in range(2):
                a = jnp.exp(log_betas[g][h] + sums[g][h][:K_TILE] + tails[h])
                if masked:
                    a = jnp.where(stricts[g], a, 0.0)
                tails[h] = tails[h] + sums[g][h][K_TILE:K_TILE + 1]
                vt = vt_ref[h * HEAD_DIM:(h + 1) * HEAD_DIM, pl.ds(starts[g], K_TILE)]
                total = total + jnp.dot(vt, a.astype(BF16), preferred_element_type=F32)
            new_acc.append(total)
        return tuple(tails), tuple(new_acc)

    init = (tuple(jnp.zeros((1, qb), F32) for _ in range(N_HEADS)),
            tuple(jnp.zeros((HEAD_DIM, qb), F32) for _ in range(N_HEADS)))
    carry = step(0, init, True)

    def more(state):
        u, (tails, _) = state
        live = tails[0]
        for h in range(1, N_HEADS):
            live = jnp.maximum(live, tails[h])
        return (u < (i + 2) // 2) & (jnp.max(live) > EXP_ZERO_BELOW)

    _, (_, acc) = lax.while_loop(more, lambda st: (st[0] + 1, step(st[0], st[1], False)), (jnp.int32(1), carry))
    o_ref[...] = jnp.concatenate(acc, axis=0).T.astype(o_ref.dtype)


def _stick_breaking(qc, kc, vct, batch, seq):
    n = qc.shape[0]
    nq = seq // Q_BLOCK
    qrow = pl.BlockSpec((Q_BLOCK, BRANCH_WIDTH), lambda b, i: (b * nq + i, 0))
    return pl.pallas_call(
        _stick_kernel,
        grid=(batch, nq),
        in_specs=[qrow, pl.BlockSpec((seq, BRANCH_WIDTH), lambda b, i: (b, 0)),
                  pl.BlockSpec((BRANCH_WIDTH, seq), lambda b, i: (0, b))],
        out_specs=qrow,
        out_shape=jax.ShapeDtypeStruct((n, BRANCH_WIDTH), BF16),
        compiler_params=_params("arbitrary", "arbitrary"),
        name="stick_breaking",
    )(qc, kc, vct)


def _merge_kernel(x_ref, g1_ref, ya_ref, yb_ref, yc_ref, yd_ref, wg_ref, wb_ref, wo_ref,
                  g2_ref, wr_ref, br_ref, x1_o, comb_o):
    x = x_ref[...]
    d = x.shape[1]
    h = _rms_bf16(x, g1_ref[...])
    ys = (ya_ref[...], yb_ref[...], yc_ref[...], yd_ref[...])
    col = 256
    pieces = []
    for c in range(d // col):
        m = None
        for n in range(N_BRANCH):
            logit = jnp.dot(h, wg_ref[:, n * d + c * col:n * d + (c + 1) * col],
                            preferred_element_type=F32)
            br = jnp.dot(ys[n], wb_ref[n, :, c * col:(c + 1) * col], preferred_element_type=F32)
            t = jax.nn.sigmoid(logit) * br
            m = t if m is None else m + t
        pieces.append(m.astype(BF16))
    merged = jnp.concatenate(pieces, axis=1)
    x1 = x + jnp.dot(merged, wo_ref[...], preferred_element_type=F32)
    x1_o[...] = x1

    ms = jnp.mean(x1 * x1, axis=-1, keepdims=True)
    h2 = x1 * lax.rsqrt(ms + EPS) * g2_ref[...]
    h_hi = h2.astype(BF16)
    h_lo = (h2 - h_hi.astype(F32)).astype(BF16)
    both = jnp.dot(h_hi, wr_ref[...], preferred_element_type=F32)
    cross = jnp.dot(h_lo, wr_ref[:, 0:LANES], preferred_element_type=F32)
    logits = both[:, 0:LANES] + both[:, LANES:2 * LANES] + cross + br_ref[...]
    rows = logits.shape[0]
    lane = lax.broadcasted_iota(jnp.int32, (rows, LANES), 1)
    neg = -jnp.inf
    is_grp = (lane >= N_EXPERTS) & (lane < N_EXPERTS + N_EXPERT_GROUPS)
    gl = jnp.where(is_grp, logits, neg)
    gmax = jnp.max(gl, axis=1, keepdims=True)
    gidx = jnp.min(jnp.where(gl == gmax, lane, LANES), axis=1, keepdims=True) - N_EXPERTS
    grp_p = 1.0 / jnp.sum(jnp.where(is_grp, jnp.exp(gl - gmax), 0.0), axis=1, keepdims=True)
    in_grp = (lane >= gidx * EXPERTS_PER_GROUP) & (lane < (gidx + 1) * EXPERTS_PER_GROUP)
    el = jnp.where(in_grp, logits, neg)
    l1 = jnp.max(el, axis=1, keepdims=True)
    j1 = jnp.min(jnp.where(el == l1, lane, LANES), axis=1, keepdims=True)
    el2 = jnp.where(lane == j1, neg, el)
    l2 = jnp.max(el2, axis=1, keepdims=True)
    j2 = jnp.min(jnp.where(el2 == l2, lane, LANES), axis=1, keepdims=True)
    e2 = jnp.exp(l2 - l1)
    w1 = grp_p / (1.0 + e2)
    w2 = grp_p * e2 / (1.0 + e2)
    comb_o[...] = jnp.where(lane == j1, w1, 0.0) + jnp.where(lane == j2, w2, 0.0)


def _merge(x2, norm1_g, ys, w_gate, w_branch, w_out, norm2_g, w_router, b_router):
    n, d = x2.shape
    rows = MERGE_ROWS
    row = lambda w: pl.BlockSpec((rows, w), lambda i: (i, 0))
    full = lambda a: pl.BlockSpec(a.shape, lambda i: (0,) * a.ndim)
    args = (x2, norm1_g.reshape(1, d), *ys, w_gate, w_branch, w_out, norm2_g.reshape(1, d),
            w_router, b_router)
    in_specs = [row(d), full(args[1])] + [row(BRANCH_WIDTH)] * 4 + [full(a) for a in args[6:]]
    return pl.pallas_call(
        _merge_kernel,
        grid=(n // rows,),
        in_specs=in_specs,
        out_specs=[row(d), row(LANES)],
        out_shape=[jax.ShapeDtypeStruct((n, d), F32), jax.ShapeDtypeStruct((n, LANES), F32)],
        compiler_params=_params("arbitrary"),
        name="merge_router",
    )(*args)


def _moe_kernel(x_ref, g2_ref, comb_ref, wgu_ref, wd_ref, gf_ref, o_ref, h_scr, acc_scr, *, final_norm):
    step = pl.program_id(1)

    @pl.when(step == 0)
    def _():
        h_scr[...] = _rms_bf16(x_ref[...], g2_ref[...])
        acc_scr[...] = jnp.zeros(acc_scr.shape, F32)

    h = h_scr[...]
    comb = comb_ref[...]
    lane = lax.broadcasted_iota(jnp.int32, comb.shape, 1)
    total = None
    for j in range(EXPERTS_PER_STEP):
        e = step * EXPERTS_PER_STEP + j
        gu = jnp.dot(h, wgu_ref[j], preferred_element_type=F32)
        w = jnp.sum(jnp.where(lane == e, comb, 0.0), axis=1, keepdims=True)
        hh = (jax.nn.silu(gu[:, :D_EXPERT]) * gu[:, D_EXPERT:] * w).astype(BF16)
        t = jnp.dot(hh, wd_ref[j], preferred_element_type=F32)
        total = t if total is None else total + t
    acc_scr[...] += total

    @pl.when(step == pl.num_programs(1) - 1)
    def _():
        y = x_ref[...] + acc_scr[...]
        if final_norm:
            ms = jnp.mean(y * y, axis=-1, keepdims=True)
            y = y * lax.rsqrt(ms + EPS) * gf_ref[...]
        o_ref[...] = y


def _moe(x1, norm2_g, comb, w_gu, w_d, norm_f_g, final_norm):
    n, d = x1.shape
    rows = min(MOE_ROWS, n)
    n_exp = w_gu.shape[0]
    row = lambda w: pl.BlockSpec((rows, w), lambda i, e: (i, 0))
    vec = pl.BlockSpec((1, d), lambda i, e: (0, 0))
    return pl.pallas_call(
        functools.partial(_moe_kernel, final_norm=final_norm),
        grid=(n // rows, n_exp // EXPERTS_PER_STEP),
        in_specs=[row(d), vec, row(LANES),
                  pl.BlockSpec((EXPERTS_PER_STEP, d, 2 * D_EXPERT), lambda i, e: (e, 0, 0)),
                  pl.BlockSpec((EXPERTS_PER_STEP, D_EXPERT, d), lambda i, e: (e, 0, 0)),
                  vec],
        out_specs=row(d),
        out_shape=jax.ShapeDtypeStruct((n, d), F32),
        scratch_shapes=[pltpu.VMEM((rows, d), BF16), pltpu.VMEM((rows, d), F32)],
        compiler_params=_params("arbitrary", "arbitrary"),
        name="experts",
    )(x1, norm2_g.reshape(1, d), comb, w_gu, w_d, norm_f_g.reshape(1, d))


def kernel(x, positions, norm1_g, w_in, sgu_ln_g, sgu_ln_b, w_spatial, b_spatial, conv_w, w_branch,
           w_out, norm2_g, w_router_group, b_router_group, w_router_expert, b_router_expert,
           w_exp_gate, w_exp_up, w_exp_down, norm_f_g):
    batch, seq, d = x.shape
    depth = w_in.shape[0]
    n = batch * seq
    assert TILES_PER_STEP == 2
    assert seq % COUNT_CHUNK == 0 and seq % (TILES_PER_STEP * K_TILE) == 0
    assert n % PROJ_ROWS == 0 and n % MERGE_ROWS == 0
    x2 = x.reshape(n, d)
    tables = _rope_tables(positions)
    for l in range(depth):
        w_a, w_t, gate_off = _proj_weight(w_in[l])
        w_gate = w_in[l][:, gate_off:].astype(BF16)
        pad = jnp.zeros((d, LANES - N_EXPERTS - N_EXPERT_GROUPS), F32)
        w_router = jnp.concatenate([w_router_expert[l], w_router_group[l], pad], axis=1)
        w_router_hi = w_router.astype(BF16)
        w_router_lo = (w_router - w_router_hi.astype(F32)).astype(BF16)
        w_router = jnp.concatenate([w_router_hi, w_router_lo], axis=1)
        b_router = jnp.concatenate([b_router_expert[l], b_router_group[l], pad[0]]).reshape(1, LANES)
        qa, ka, vat, iq, ik, iwt, y_b, qc, kc, vct, y_d = _project(
            x2, norm1_g[l], w_a, w_t, tables, sgu_ln_g[l], sgu_ln_b[l], w_spatial[l], b_spatial[l],
            conv_w[l], seq)
        y_a = _dsa(iq, iwt, qa, ik, ka, vat, batch, seq)
        y_c = _stick_breaking(qc, kc, vct, batch, seq)
        x1, comb = _merge(x2, norm1_g[l], (y_a, y_b, y_c, y_d), w_gate, w_branch[l].astype(BF16),
                          w_out[l].astype(BF16), norm2_g[l], w_router, b_router)
        w_gu = jnp.concatenate([w_exp_gate[l], w_exp_up[l]], axis=2).astype(BF16)
        x2 = _moe(x1, norm2_g[l], comb, w_gu, w_exp_down[l].astype(BF16), norm_f_g,
                  final_norm=(l == depth - 1))
    return x2.reshape(batch, seq, d)
```

```python
import functools

import jax
import jax.numpy as jnp
from jax import lax
from jax.experimental import pallas as pl
from jax.experimental.pallas import tpu as pltpu

F32 = jnp.float32
BF16 = jnp.bfloat16

EPS = 1e-6
ROPE_THETA = 10000.0
HEAD_DIM = 64
BRANCH_WIDTH = 256
N_HEADS = 4
IDX_DIM = 32
TOPK_MAX = 256
CHUNK = 128
N_GROUPS_B = 4
CONV_WIDTH = 3
N_BRANCH = 4
N_EXPERT_GROUPS = 4
EXPERTS_PER_GROUP = 8
N_EXPERTS = N_EXPERT_GROUPS * EXPERTS_PER_GROUP
D_EXPERT = 256

LANES = 128
SUBLANES = 8
Q_BLOCK = 256
K_TILE = Q_BLOCK
TILES_PER_STEP = 2
COUNT_CHUNK = 512
COUNT_ROWS = 64
PROJ_ROWS = 256
MERGE_ROWS = 512
MOE_ROWS = 1024
EXPERTS_PER_STEP = 4
GROUP_SLOTS = 320
CONV_HALO = 8
VMEM_LIMIT = 56 * 1024 * 1024
PHANTOM_LOGIT = -1e4
NEG_BIG = -1e30
EXP_ZERO_BELOW = -110.0
INT_MIN = -2 ** 31
HALF_RANGE = 2 ** 15
ONES_ROWS = 16
BOUND_MARGIN = 1.02
BOUND_LIMIT = 40.0

_OFF = {}
_o = 0
for _name, _w in (("q", 512), ("k", 512), ("iq", 256), ("ik", 256),
                  ("ub", 256), ("vb", 256), ("qc", 256), ("kc", 256),
                  ("bd", 256), ("cd", 256), ("xd", 256)):
    _OFF[_name] = (_o, _w)
    _o += _w
W_A_COLS = _o
T_VA, T_VC, T_IW = 0, BRANCH_WIDTH, 2 * BRANCH_WIDTH
W_T_ROWS = 2 * BRANCH_WIDTH + 16

_CONTRACT_LAST = (((1,), (1,)), ((), ()))


def _params(*sem):
    return pltpu.CompilerParams(dimension_semantics=sem, vmem_limit_bytes=VMEM_LIMIT)


def _rope_table_kernel(pos_ref, inv_a_ref, inv_i_ref, cos_a, sin_a, cos_i, sin_i):
    p = pos_ref[...]
    a = p * inv_a_ref[...]
    cos_a[...] = jnp.cos(a)
    sin_a[...] = jnp.sin(a)
    b = p * inv_i_ref[...]
    cos_i[...] = jnp.cos(b)
    sin_i[...] = jnp.sin(b)


def _rope_tables(positions):
    n = positions.size
    rows = PROJ_ROWS
    pos = positions.astype(F32).reshape(n, 1)
    inv_a = ROPE_THETA ** (-jnp.arange(0, HEAD_DIM, 2, dtype=F32) / HEAD_DIM)
    inv_i = ROPE_THETA ** (-jnp.arange(0, IDX_DIM, 2, dtype=F32) / IDX_DIM)
    inv_a = jnp.tile(inv_a, LANES // inv_a.size).reshape(1, LANES)
    inv_i = jnp.tile(inv_i, LANES // inv_i.size).reshape(1, LANES)
    tab = jax.ShapeDtypeStruct((n, LANES), F32)
    row_spec = pl.BlockSpec((rows, LANES), lambda i: (i, 0))
    const = pl.BlockSpec((1, LANES), lambda i: (0, 0))
    return pl.pallas_call(
        _rope_table_kernel,
        grid=(n // rows,),
        in_specs=[pl.BlockSpec((rows, 1), lambda i: (i, 0)), const, const],
        out_specs=[row_spec] * 4,
        out_shape=[tab] * 4,
        compiler_params=_params("arbitrary"),
        name="rope_tables",
    )(pos, inv_a, inv_i)


def _rms_bf16(x, g):
    ms = jnp.mean(x * x, axis=-1, keepdims=True)
    return (x * lax.rsqrt(ms + EPS) * g).astype(BF16)


def _proj_kernel(x_ref, g_ref, w_ref, wt_ref, cos_a, sin_a, cos_i, sin_i, lng_ref, lnb_ref, ws_ref,
                 bs_ref, cw_ref,
                 qa_o, ka_o, vat_o, iq_o, ik_o, iwt_o, yb_o, qc_o, kc_o, vct_o, yd_o,
                 zbuf, *, steps_per_seq):
    rows = x_ref.shape[0]
    h = _rms_bf16(x_ref[...], g_ref[...])

    def mm(name):
        off, width = _OFF[name]
        return jnp.dot(h, w_ref[:, off:off + width], preferred_element_type=F32)

    def rope(name, cos_ref, sin_ref, out_ref, scale):
        p = mm(name)
        w = p.shape[1] // 2
        c = cos_ref[...]
        s = sin_ref[...]
        for j in range(w // LANES):
            lo = j * LANES
            r = p[:, lo:lo + LANES] * c + p[:, w + lo:w + lo + LANES] * s
            out_ref[:, lo:lo + LANES] = (r if scale == 1.0 else r * scale).astype(out_ref.dtype)

    att_scale = HEAD_DIM ** -0.5
    rope("q", cos_a, sin_a, qa_o, att_scale)
    rope("k", cos_a, sin_a, ka_o, 1.0)
    rope("iq", cos_i, sin_i, iq_o, 1.0)
    rope("ik", cos_i, sin_i, ik_o, 1.0)
    qc_o[...] = (mm("qc") * att_scale).astype(BF16)
    kc_o[...] = mm("kc").astype(BF16)
    t = lax.dot_general(wt_ref[...], h, _CONTRACT_LAST, preferred_element_type=F32)
    vat_o[...] = t[T_VA:T_VA + BRANCH_WIDTH, :].astype(BF16)
    vct_o[...] = t[T_VC:T_VC + BRANCH_WIDTH, :].astype(BF16)
    iwt_o[...] = t[T_IW:T_IW + SUBLANES, :]

    u = jax.nn.gelu(mm("ub"))
    v = jax.nn.gelu(mm("vb"))
    mu = jnp.mean(v, axis=-1, keepdims=True)
    var = jnp.mean(jnp.square(v - mu), axis=-1, keepdims=True)
    vn = ((v - mu) * lax.rsqrt(var + EPS) * lng_ref[...] + lnb_ref[...]).astype(BF16)
    t_idx = lax.broadcasted_iota(jnp.int32, (CHUNK, CHUNK), 0)
    s_idx = lax.broadcasted_iota(jnp.int32, (CHUNK, CHUNK), 1)
    causal = s_idx <= t_idx
    group_masks = [_head_lane_mask(BRANCH_WIDTH, BRANCH_WIDTH // N_GROUPS_B, g) for g in range(N_GROUPS_B)]
    w_tril = [jnp.where(causal, ws_ref[g], 0.0).astype(BF16) for g in range(N_GROUPS_B)]
    for c in range(rows // CHUNK):
        vc = vn[c * CHUNK:(c + 1) * CHUNK, :]
        mixed = bs_ref[...]
        for g in range(N_GROUPS_B):
            full = jnp.dot(w_tril[g], vc, preferred_element_type=F32)
            mixed = mixed + jnp.where(group_masks[g], full, 0.0)
        yb_o[c * CHUNK:(c + 1) * CHUNK, :] = (u[c * CHUNK:(c + 1) * CHUNK, :] * mixed).astype(BF16)

    bd = mm("bd")
    z = mm("cd") * mm("xd")

    @pl.when(pl.program_id(0) % steps_per_seq == 0)
    def _():
        zbuf[0:CONV_HALO, :] = jnp.zeros((CONV_HALO, BRANCH_WIDTH), F32)

    zbuf[CONV_HALO:CONV_HALO + rows, :] = z
    y = cw_ref[CONV_WIDTH - 1:CONV_WIDTH, :] * z
    for tap in range(CONV_WIDTH - 1):
        shift = CONV_WIDTH - 1 - tap
        y = y + cw_ref[tap:tap + 1, :] * zbuf[CONV_HALO - shift:CONV_HALO - shift + rows, :]
    yd_o[...] = (bd * y).astype(BF16)
    zbuf[0:CONV_HALO, :] = z[rows - CONV_HALO:rows, :]


def _swap_halves(w, dim):
    d_in, cols = w.shape
    wh = w.reshape(d_in, cols // dim, 2, dim // 2)
    return jnp.stack([-wh[:, :, 1], wh[:, :, 0]], axis=2).reshape(d_in, cols)


def _proj_weight(w_in):
    d = w_in.shape[0]
    sizes = (BRANCH_WIDTH,) * 3 + (N_HEADS * IDX_DIM, IDX_DIM, N_HEADS) + (BRANCH_WIDTH,) * 8
    parts, o = [], 0
    for s in sizes:
        parts.append(w_in[:, o:o + s])
        o += s
    qa, ka, va, iq, ik, iw, ub, vb, qc, kc, vc, bd, cd, xd = parts
    ik_rep = jnp.tile(ik, (1, N_HEADS))
    cols = [qa, _swap_halves(qa, HEAD_DIM), ka, _swap_halves(ka, HEAD_DIM),
            iq, _swap_halves(iq, IDX_DIM), ik_rep, _swap_halves(ik_rep, IDX_DIM),
            ub, vb, qc, kc, bd, cd, xd]
    iw_pad = jnp.concatenate([iw, jnp.zeros((d, W_T_ROWS - 2 * BRANCH_WIDTH - N_HEADS), w_in.dtype)], axis=1)
    w_t = jnp.concatenate([va, vc, iw_pad], axis=1).T
    return jnp.concatenate(cols, axis=1).astype(BF16), w_t.astype(BF16), o


def _project(x2, norm_g, w_a, w_t, tables, ln_g, ln_b, w_spatial, b_spatial, conv_w, seq):
    n, d = x2.shape
    rows = PROJ_ROWS
    cos_a, sin_a, cos_i, sin_i = tables
    bias = jnp.repeat(b_spatial.T, BRANCH_WIDTH // N_GROUPS_B, axis=1)
    row = lambda w: pl.BlockSpec((rows, w), lambda i: (i, 0))
    col = lambda r: pl.BlockSpec((r, rows), lambda i: (0, i))
    full = lambda a: pl.BlockSpec(a.shape, lambda i: (0,) * a.ndim)
    bf = lambda w: jax.ShapeDtypeStruct((n, w), BF16)
    bft = jax.ShapeDtypeStruct((BRANCH_WIDTH, n), BF16)
    args = (x2, norm_g.reshape(1, d), w_a, w_t, cos_a, sin_a, cos_i, sin_i, ln_g.reshape(1, -1),
            ln_b.reshape(1, -1), w_spatial, bias, conv_w)
    in_specs = [row(d), full(args[1]), full(w_a), full(w_t), row(LANES), row(LANES), row(LANES),
                row(LANES), full(args[8]), full(args[9]), full(w_spatial), full(bias), full(conv_w)]
    out_shape = [bf(256), bf(256), bft, bf(128), bf(128), jax.ShapeDtypeStruct((SUBLANES, n), F32),
                 bf(256), bf(256), bf(256), bft, bf(256)]
    out_specs = [row(256), row(256), col(BRANCH_WIDTH), row(128), row(128), col(SUBLANES),
                 row(256), row(256), row(256), col(BRANCH_WIDTH), row(256)]
    return pl.pallas_call(
        functools.partial(_proj_kernel, steps_per_seq=seq // rows),
        grid=(n // rows,),
        in_specs=in_specs,
        out_specs=out_specs,
        out_shape=out_shape,
        scratch_shapes=[pltpu.VMEM((CONV_HALO + rows, BRANCH_WIDTH), F32)],
        compiler_params=_params("arbitrary"),
        name="project",
    )(*args)


def _head_lane_mask(width, per_head, h, rows=1):
    lane = lax.broadcasted_iota(jnp.int32, (rows, width), 1)
    return (lane >= h * per_head) & (lane < (h + 1) * per_head)


def _masked_heads(x, per_head):
    return [jnp.where(_head_lane_mask(x.shape[1], per_head, h), x, jnp.zeros_like(x))
            for h in range(N_HEADS)]


def _float_to_key(x):
    bits = lax.bitcast_convert_type(x, jnp.int32)
    sign = bits >> 31
    return (bits ^ (sign & jnp.int32(0x7FFFFFFF))) - sign


def _key_to_float(key):
    bits = jnp.where(key < 0, (-key) | jnp.int32(INT_MIN), key)
    return lax.bitcast_convert_type(bits, F32)


def _neg_abs(x):
    bits = lax.bitcast_convert_type(x, jnp.int32) | jnp.int32(INT_MIN)
    return lax.bitcast_convert_type(bits, F32)


def _col_reduce(x, op, final):
    while x.shape[0] > SUBLANES:
        half = x.shape[0] // 2
        x = op(x[:half], x[half:])
    return final(x, axis=0, keepdims=True)


def _col_sum(x):
    return _col_reduce(x, jnp.add, jnp.sum)


def _col_max(x):
    return _col_reduce(x, jnp.maximum, jnp.max)


def _step_starts(step):
    return [pl.multiple_of((step * TILES_PER_STEP + g) * K_TILE, K_TILE) for g in range(TILES_PER_STEP)]


def _dsa_kernel(iq_ref, iwt_ref, q_ref, ik_ref, k_ref, vt_ref, o_ref, sc_ref, hi_ref, lo_ref, kmax_ref, *,
                topk):
    qb = q_ref.shape[0]
    i = pl.program_id(1)
    n_tiles = i + 1
    n_steps = (n_tiles + TILES_PER_STEP - 1) // TILES_PER_STEP

    @pl.when(i == 0)
    def _():
        sc_ref[...] = jnp.full(sc_ref.shape, -jnp.inf, F32)
        hi_ref[...] = jnp.full(hi_ref.shape, -HALF_RANGE, jnp.int16)
        lo_ref[...] = jnp.full(lo_ref.shape, -HALF_RANGE, jnp.int16)

    iq_heads = _masked_heads(iq_ref[...], IDX_DIM)
    iwt = iwt_ref[...]
    iw_heads = [iwt[h:h + 1, :] for h in range(N_HEADS)]

    def put_scores(start, sc):
        sc_ref[pl.ds(start, K_TILE), :] = sc
        key = _float_to_key(sc)
        hi_ref[pl.ds(start, K_TILE), :] = (key >> 16).astype(jnp.int16)
        lo_ref[pl.ds(start, K_TILE), :] = ((key & 0xFFFF) - HALF_RANGE).astype(jnp.int16)

    key_off = lax.broadcasted_iota(jnp.int32, (K_TILE, qb), 0)
    qry_pos = i * qb + lax.broadcasted_iota(jnp.int32, (K_TILE, qb), 1)

    def score_step(jj, masked):
        starts = _step_starts(jj)
        dots = [[lax.dot_general(ik_ref[pl.ds(st, K_TILE), :], iq_heads[h], _CONTRACT_LAST,
                                 preferred_element_type=F32) for h in range(N_HEADS)] for st in starts]
        for g in range(TILES_PER_STEP):
            sc = None
            for h in range(N_HEADS):
                t = iw_heads[h] * jnp.maximum(dots[g][h], 0.0)
                sc = t if sc is None else sc + t
            if masked:
                sc = jnp.where(starts[g] + key_off <= qry_pos, sc, -jnp.inf)
            put_scores(starts[g], sc)

    def full_step(jj, carry):
        score_step(jj, False)
        return carry

    lax.fori_loop(0, i // TILES_PER_STEP, full_step, 0)
    score_step(i // TILES_PER_STEP, True)

    n_chunks = (n_tiles * K_TILE + COUNT_CHUNK - 1) // COUNT_CHUNK
    kf = jnp.float32(topk)

    def chunk_rows(c):
        base = pl.multiple_of(c * COUNT_CHUNK, COUNT_CHUNK)
        return [pl.ds(base + r * COUNT_ROWS, COUNT_ROWS) for r in range(COUNT_CHUNK // COUNT_ROWS)]

    def count(ref, pred, dtype):
        one, zero = jnp.ones((), dtype), jnp.zeros((), dtype)

        def body(c, acc):
            for rows in chunk_rows(c):
                acc = acc + jnp.where(pred(ref[rows, :]), one, zero)
            return acc
        acc = lax.fori_loop(0, n_chunks, body, jnp.zeros((COUNT_ROWS, qb), dtype))
        return jnp.sum(acc.astype(F32), axis=0, keepdims=True)

    def bisect16(ref, target):
        def bit_step(b, carry):
            best, cnt_best, cnt_above = carry
            cand = best + lax.shift_left(jnp.int32(1), 15 - b)
            cand16 = cand.astype(jnp.int16)
            cnt = count(ref, lambda blk: blk >= cand16, jnp.int16)
            ok = cnt >= target
            return jnp.where(ok, cand, best), jnp.where(ok, cnt, cnt_best), jnp.where(ok, cnt_above, cnt)
        init = (jnp.full((1, qb), -HALF_RANGE, jnp.int32), jnp.zeros((1, qb), F32), jnp.zeros((1, qb), F32))
        return lax.fori_loop(0, 16, bit_step, init)

    hi_t, cnt_hi, above = bisect16(hi_ref, kf)
    hi_t16 = hi_t.astype(jnp.int16)

    def mask_low(c, carry):
        for rows in chunk_rows(c):
            lo_ref[rows, :] = jnp.where(hi_ref[rows, :] == hi_t16, lo_ref[rows, :], jnp.int16(-HALF_RANGE))
        return carry

    lax.fori_loop(0, n_chunks, mask_low, 0)
    lo_t, cnt_lo, _ = bisect16(lo_ref, kf - above)
    cnt_key = above + jnp.where(lo_t == -HALF_RANGE, cnt_hi - above, cnt_lo)
    key = hi_t * (2 * HALF_RANGE) + (lo_t + HALF_RANGE)
    few = i * qb + lax.broadcasted_iota(jnp.int32, (1, qb), 1) + 1 < topk
    thr = jnp.where(few, jnp.finfo(F32).min, _key_to_float(key))

    @pl.when(jnp.max(jnp.where(few, kf, cnt_key)) > kf)
    def _():
        need = kf - count(sc_ref, lambda blk: blk > thr, F32)
        before = (lax.broadcasted_iota(jnp.int32, (K_TILE, K_TILE), 1)
                  < lax.broadcasted_iota(jnp.int32, (K_TILE, K_TILE), 0))
        before = jnp.where(before, 1.0, 0.0).astype(BF16)

        def tie_body(jj, seen):
            starts = _step_starts(jj)
            blks = [sc_ref[pl.ds(st, K_TILE), :] for st in starts]
            eqs = [blk == thr for blk in blks]
            eq_fs = [jnp.where(eq, 1.0, 0.0) for eq in eqs]
            ranks = [jnp.dot(before, eq_f.astype(BF16), preferred_element_type=F32) for eq_f in eq_fs]
            for g in range(TILES_PER_STEP):
                drop = eqs[g] & (seen + ranks[g] >= need)
                sc_ref[pl.ds(starts[g], K_TILE), :] = jnp.where(drop, -jnp.inf, blks[g])
                seen = seen + _col_sum(eq_fs[g])
            return seen

        lax.fori_loop(0, n_steps, tie_body, jnp.zeros((1, qb), F32))

    q = q_ref[...]
    q_heads = _masked_heads(q, HEAD_DIM)
    head_rows = jnp.concatenate(
        [jnp.where(_head_lane_mask(BRANCH_WIDTH, HEAD_DIM, h), 1.0, 0.0) for h in range(N_HEADS)]
        + [jnp.zeros((ONES_ROWS - N_HEADS, BRANCH_WIDTH), F32)], axis=0).astype(BF16)
    qf = q.astype(F32)
    q_sq = lax.dot_general(head_rows, (qf * qf).astype(BF16), _CONTRACT_LAST,
                           preferred_element_type=F32)
    kf32 = k_ref[pl.ds(pl.multiple_of(i * K_TILE, K_TILE), K_TILE), :].astype(F32)
    k_sq = lax.dot_general(head_rows, (kf32 * kf32).astype(BF16), _CONTRACT_LAST,
                           preferred_element_type=F32)
    k_sq_max = jnp.max(k_sq, axis=1, keepdims=True)

    @pl.when(i == 0)
    def _():
        kmax_ref[...] = jnp.zeros(kmax_ref.shape, F32)

    kmax_ref[...] = jnp.maximum(kmax_ref[...], jnp.broadcast_to(k_sq_max, kmax_ref.shape))
    bound = jnp.sqrt(q_sq * kmax_ref[:, 0:1]) * BOUND_MARGIN
    bounds = [bound[h:h + 1, :] for h in range(N_HEADS)]
    ones_rows = jnp.ones((ONES_ROWS, K_TILE), BF16)

    def step_logits(jj):
        starts = _step_starts(jj)
        sels = [sc_ref[pl.ds(st, K_TILE), :] >= thr for st in starts]
        logits = [[lax.dot_general(k_ref[pl.ds(st, K_TILE), :], q_heads[h], _CONTRACT_LAST,
                                   preferred_element_type=F32) for h in range(N_HEADS)] for st in starts]
        masked = [[jnp.where(sels[g], logits[g][h], NEG_BIG) for g in range(TILES_PER_STEP)]
                  for h in range(N_HEADS)]
        return starts, masked

    def value_products(starts, probs, h):
        pv = None
        for g, st in enumerate(starts):
            vt = vt_ref[h * HEAD_DIM:(h + 1) * HEAD_DIM, pl.ds(st, K_TILE)]
            t = jnp.dot(vt, probs[g], preferred_element_type=F32)
            pv = t if pv is None else pv + t
        return pv

    def bounded_step(jj, carry):
        l, acc = carry
        starts, masked = step_logits(jj)
        probs = [[jnp.exp(s - bounds[h]).astype(BF16) for s in masked[h]] for h in range(N_HEADS)]
        l_new, acc_new = [], []
        for h in range(N_HEADS):
            lh = l[h]
            for pb in probs[h]:
                lh = lh + jnp.dot(ones_rows, pb, preferred_element_type=F32)[0:1]
            l_new.append(lh)
            acc_new.append(acc[h] + value_products(starts, probs[h], h))
        return tuple(l_new), tuple(acc_new)

    def online_step(jj, carry):
        m, l, acc = carry
        starts, masked = step_logits(jj)
        m_new, l_new, acc_new = [], [], []
        for h in range(N_HEADS):
            mh = m[h]
            for s in masked[h]:
                mh = jnp.maximum(mh, _col_max(s))
            ps = [jnp.exp(s - mh) for s in masked[h]]
            alpha = jnp.exp(m[h] - mh)
            lh = alpha * l[h]
            for p in ps:
                lh = lh + _col_sum(p)
            m_new.append(mh)
            l_new.append(lh)
            acc_new.append(alpha * acc[h] + value_products(starts, [p.astype(BF16) for p in ps], h))
        return tuple(m_new), tuple(l_new), tuple(acc_new)

    def finish(l, acc):
        out_t = jnp.concatenate([acc[h] / l[h] for h in range(N_HEADS)], axis=0)
        o_ref[...] = out_t.T.astype(o_ref.dtype)

    zeros_l = tuple(jnp.zeros((1, qb), F32) for _ in range(N_HEADS))
    zeros_acc = tuple(jnp.zeros((HEAD_DIM, qb), F32) for _ in range(N_HEADS))
    small = jnp.max(bound[0:N_HEADS, :]) <= BOUND_LIMIT

    @pl.when(small)
    def _():
        finish(*lax.fori_loop(0, n_steps, bounded_step, (zeros_l, zeros_acc)))

    @pl.when(jnp.logical_not(small))
    def _():
        init = (tuple(jnp.full((1, qb), NEG_BIG, F32) for _ in range(N_HEADS)), zeros_l, zeros_acc)
        _, l, acc = lax.fori_loop(0, n_steps, online_step, init)
        finish(l, acc)


def _dsa(iq, iwt, qa, ik, ka, vat, batch, seq):
    n = qa.shape[0]
    nq = seq // Q_BLOCK
    topk = min(TOPK_MAX, seq // 4)
    qrow = lambda w: pl.BlockSpec((Q_BLOCK, w), lambda b, i: (b * nq + i, 0))
    whole = lambda w: pl.BlockSpec((seq, w), lambda b, i: (b, 0))
    return pl.pallas_call(
        functools.partial(_dsa_kernel, topk=topk),
        grid=(batch, nq),
        in_specs=[qrow(LANES), pl.BlockSpec((SUBLANES, Q_BLOCK), lambda b, i: (0, b * nq + i)),
                  qrow(BRANCH_WIDTH), whole(LANES), whole(BRANCH_WIDTH),
                  pl.BlockSpec((BRANCH_WIDTH, seq), lambda b, i: (0, b))],
        out_specs=qrow(BRANCH_WIDTH),
        out_shape=jax.ShapeDtypeStruct((n, BRANCH_WIDTH), BF16),
        scratch_shapes=[pltpu.VMEM((seq, Q_BLOCK), F32), pltpu.VMEM((seq, Q_BLOCK), jnp.int16),
                        pltpu.VMEM((seq, Q_BLOCK), jnp.int16), pltpu.VMEM((ONES_ROWS, LANES), F32)],
        compiler_params=_params("arbitrary", "arbitrary"),
        name="dsa",
    )(iq, iwt, qa, ik, ka, vat)


def _stick_kernel(q_ref, k_ref, vt_ref, o_ref):
    qb = q_ref.shape[0]
    i = pl.program_id(1)
    q_heads = _masked_heads(q_ref[...], HEAD_DIM)
    after = (lax.broadcasted_iota(jnp.int32, (K_TILE, K_TILE), 1)
             > lax.broadcasted_iota(jnp.int32, (K_TILE, K_TILE), 0))
    after = jnp.where(after, 1.0, 0.0).astype(BF16)
    after = jnp.concatenate([after, jnp.ones((ONES_ROWS, K_TILE), BF16)], axis=0)
    key_off = lax.broadcasted_iota(jnp.int32, (K_TILE, qb), 0)
    qry_pos = i * qb + lax.broadcasted_iota(jnp.int32, (K_TILE, qb), 1)

    def step(u, carry, masked):
        tails, acc = carry
        first = i - 2 * u
        tiles = [first, jnp.maximum(first - 1, 0)]
        bias = [None, jnp.where(first >= 1, 0.0, PHANTOM_LOGIT)]
        starts = [pl.multiple_of(t * K_TILE, K_TILE) for t in tiles]
        stricts = [starts[g] + key_off < qry_pos if masked else None for g in range(2)]
        zs = [[lax.dot_general(k_ref[pl.ds(st, K_TILE), :], q_heads[h], _CONTRACT_LAST,
                               preferred_element_type=F32) for h in range(N_HEADS)] for st in starts]
        log_betas, splits = [], []
        for g in range(2):
            lbs, sps = [], []
            for h in range(N_HEADS):
                z = zs[g][h] if bias[g] is None else zs[g][h] + bias[g]
                log_beta = jnp.minimum(z, 0.0) - jnp.log(1.0 + jnp.exp(_neg_abs(z)))
                log_keep = log_beta - z
                if masked:
                    log_keep = jnp.where(stricts[g], log_keep, 0.0)
                lbs.append(log_beta)
                sps.append(log_keep.astype(BF16))
            log_betas.append(lbs)
            splits.append(sps)
        sums = [[jnp.dot(after, splits[g][h], preferred_element_type=F32) for h in range(N_HEADS)]
                for g in range(2)]
        tails = list(tails)
        new_acc = []
        for h in range(N_HEADS):
            total = acc[h]
            for g in range(2):
                a = jnp.exp(log_betas[g][h] + sums[g][h][:K_TILE] + tails[h])
                if masked:
                    a = jnp.where(stricts[g], a, 0.0)
                tails[h] = tails[h] + sums[g][h][K_TILE:K_TILE + 1]
                vt = vt_ref[h * HEAD_DIM:(h + 1) * HEAD_DIM, pl.ds(starts[g], K_TILE)]
                total = total + jnp.dot(vt, a.astype(BF16), preferred_element_type=F32)
            new_acc.append(total)
        return tuple(tails), tuple(new_acc)

    init = (tuple(jnp.zeros((1, qb), F32) for _ in range(N_HEADS)),
            tuple(jnp.zeros((HEAD_DIM, qb), F32) for _ in range(N_HEADS)))
    carry = step(0, init, True)

    def more(state):
        u, (tails, _) = state
        live = tails[0]
        for h in range(1, N_HEADS):
            live = jnp.maximum(live, tails[h])
        return (u < (i + 2) // 2) & (jnp.max(live) > EXP_ZERO_BELOW)

    _, (_, acc) = lax.while_loop(more, lambda st: (st[0] + 1, step(st[0], st[1], False)), (jnp.int32(1), carry))
    o_ref[...] = jnp.concatenate(acc, axis=0).T.astype(o_ref.dtype)


def _stick_breaking(qc, kc, vct, batch, seq):
    n = qc.shape[0]
    nq = seq // Q_BLOCK
    qrow = pl.BlockSpec((Q_BLOCK, BRANCH_WIDTH), lambda b, i: (b * nq + i, 0))
    return pl.pallas_call(
        _stick_kernel,
        grid=(batch, nq),
        in_specs=[qrow, pl.BlockSpec((seq, BRANCH_WIDTH), lambda b, i: (b, 0)),
                  pl.BlockSpec((BRANCH_WIDTH, seq), lambda b, i: (0, b))],
        out_specs=qrow,
        out_shape=jax.ShapeDtypeStruct((n, BRANCH_WIDTH), BF16),
        compiler_params=_params("arbitrary", "arbitrary"),
        name="stick_breaking",
    )(qc, kc, vct)


def _merge_kernel(x_ref, g1_ref, ya_ref, yb_ref, yc_ref, yd_ref, wg_ref, wb_ref, wo_ref,
                  g2_ref, wr_ref, br_ref, x1_o, comb_o):
    x = x_ref[...]
    d = x.shape[1]
    h = _rms_bf16(x, g1_ref[...])
    ys = (ya_ref[...], yb_ref[...], yc_ref[...], yd_ref[...])
    col = 256
    pieces = []
    for c in range(d // col):
        m = None
        for n in range(N_BRANCH):
            logit = jnp.dot(h, wg_ref[:, n * d + c * col:n * d + (c + 1) * col],
                            preferred_element_type=F32)
            br = jnp.dot(ys[n], wb_ref[n, :, c * col:(c + 1) * col], preferred_element_type=F32)
            t = jax.nn.sigmoid(logit) * br
            m = t if m is None else m + t
        pieces.append(m.astype(BF16))
    merged = jnp.concatenate(pieces, axis=1)
    x1 = x + jnp.dot(merged, wo_ref[...], preferred_element_type=F32)
    x1_o[...] = x1

    ms = jnp.mean(x1 * x1, axis=-1, keepdims=True)
    h2 = x1 * lax.rsqrt(ms + EPS) * g2_ref[...]
    h_hi = h2.astype(BF16)
    h_lo = (h2 - h_hi.astype(F32)).astype(BF16)
    both = jnp.dot(h_hi, wr_ref[...], preferred_element_type=F32)
    cross = jnp.dot(h_lo, wr_ref[:, 0:LANES], preferred_element_type=F32)
    logits = both[:, 0:LANES] + both[:, LANES:2 * LANES] + cross + br_ref[...]
    rows = logits.shape[0]
    lane = lax.broadcasted_iota(jnp.int32, (rows, LANES), 1)
    neg = -jnp.inf
    is_grp = (lane >= N_EXPERTS) & (lane < N_EXPERTS + N_EXPERT_GROUPS)
    gl = jnp.where(is_grp, logits, neg)
    gmax = jnp.max(gl, axis=1, keepdims=True)
    gidx = jnp.min(jnp.where(gl == gmax, lane, LANES), axis=1, keepdims=True) - N_EXPERTS
    grp_p = 1.0 / jnp.sum(jnp.where(is_grp, jnp.exp(gl - gmax), 0.0), axis=1, keepdims=True)
    in_grp = (lane >= gidx * EXPERTS_PER_GROUP) & (lane < (gidx + 1) * EXPERTS_PER_GROUP)
    el = jnp.where(in_grp, logits, neg)
    l1 = jnp.max(el, axis=1, keepdims=True)
    j1 = jnp.min(jnp.where(el == l1, lane, LANES), axis=1, keepdims=True)
    el2 = jnp.where(lane == j1, neg, el)
    l2 = jnp.max(el2, axis=1, keepdims=True)
    j2 = jnp.min(jnp.where(el2 == l2, lane, LANES), axis=1, keepdims=True)
    e2 = jnp.exp(l2 - l1)
    w1 = grp_p / (1.0 + e2)
    w2 = grp_p * e2 / (1.0 + e2)
    comb_o[...] = jnp.where(lane == j1, w1, 0.0) + jnp.where(lane == j2, w2, 0.0)


def _merge(x2, norm1_g, ys, w_gate, w_branch, w_out, norm2_g, w_router, b_router):
    n, d = x2.shape
    rows = MERGE_ROWS
    row = lambda w: pl.BlockSpec((rows, w), lambda i: (i, 0))
    full = lambda a: pl.BlockSpec(a.shape, lambda i: (0,) * a.ndim)
    args = (x2, norm1_g.reshape(1, d), *ys, w_gate, w_branch, w_out, norm2_g.reshape(1, d),
            w_router, b_router)
    in_specs = [row(d), full(args[1])] + [row(BRANCH_WIDTH)] * 4 + [full(a) for a in args[6:]]
    return pl.pallas_call(
        _merge_kernel,
        grid=(n // rows,),
        in_specs=in_specs,
        out_specs=[row(d), row(LANES)],
        out_shape=[jax.ShapeDtypeStruct((n, d), F32), jax.ShapeDtypeStruct((n, LANES), F32)],
        compiler_params=_params("arbitrary"),
        name="merge_router",
    )(*args)


def _moe_kernel(x_ref, g2_ref, comb_ref, wgu_ref, wd_ref, gf_ref, o_ref,
                h_scr, acc_scr, memb_scr, rank_scr, mcol_scr, rcol_scr, hg_scr, cw_scr, y_scr, flag_scr,
                *, final_norm):
    step = pl.program_id(1)
    rows = x_ref.shape[0]
    steps_per_group = EXPERTS_PER_GROUP // EXPERTS_PER_STEP
    group = step // steps_per_group
    first_of_group = step % steps_per_group == 0
    last_of_group = step % steps_per_group == steps_per_group - 1
    lane = lax.broadcasted_iota(jnp.int32, (rows, LANES), 1)

    @pl.when(step == 0)
    def _():
        h_scr[...] = _rms_bf16(x_ref[...], g2_ref[...])
        acc_scr[...] = jnp.zeros(acc_scr.shape, F32)
        comb = comb_ref[...]
        member_cols = [jnp.sum(jnp.where((lane >= g * EXPERTS_PER_GROUP) & (lane < (g + 1) * EXPERTS_PER_GROUP),
                                         comb, 0.0), axis=1, keepdims=True) > 0.0 for g in range(N_EXPERT_GROUPS)]
        mcol = jnp.zeros((rows, LANES), F32)
        for g in range(N_EXPERT_GROUPS):
            mcol = jnp.where((lane == g) & member_cols[g], 1.0, mcol)
        mcol_scr[...] = mcol
        t_row = lax.broadcasted_iota(jnp.int32, (rows, rows), 0)
        t_col = lax.broadcasted_iota(jnp.int32, (rows, rows), 1)
        earlier = jnp.where(t_col < t_row, 1.0, 0.0).astype(BF16)
        rcol_scr[...] = jnp.dot(earlier, mcol.astype(BF16), preferred_element_type=F32)
        memb = mcol.T[0:ONES_ROWS, :]
        memb_scr[...] = memb
        later = jnp.where(t_row < t_col, 1.0, 0.0).astype(BF16)
        rank_scr[...] = jnp.dot(memb.astype(BF16), later, preferred_element_type=F32)
        counts = jnp.sum(memb, axis=1, keepdims=True)
        flag_scr[0] = (jnp.max(counts) > GROUP_SLOTS).astype(jnp.int32)

    crowded = flag_scr[0] == 1
    comb = comb_ref[...]

    def expert_rows(hb, cw, j):
        e = step * EXPERTS_PER_STEP + j
        lane_c = lax.broadcasted_iota(jnp.int32, cw.shape, 1)
        w = jnp.sum(jnp.where(lane_c == e, cw, 0.0), axis=1, keepdims=True)
        gu = jnp.dot(hb, wgu_ref[j], preferred_element_type=F32)
        hh = (jax.nn.silu(gu[:, :D_EXPERT]) * gu[:, D_EXPERT:] * w).astype(BF16)
        return jnp.dot(hh, wd_ref[j], preferred_element_type=F32)

    @pl.when(crowded)
    def _():
        total = None
        for j in range(EXPERTS_PER_STEP):
            t = expert_rows(h_scr[...], comb, j)
            total = t if total is None else total + t
        acc_scr[...] += total

    @pl.when(jnp.logical_not(crowded))
    def _():
        slot_rows = lax.broadcasted_iota(jnp.int32, (GROUP_SLOTS, rows), 0).astype(F32)

        @pl.when(first_of_group)
        def _():
            in_group = memb_scr[pl.ds(group, 1), :] > 0.0
            rank = rank_scr[pl.ds(group, 1), :]
            pick = jnp.where(in_group & (slot_rows == rank), 1.0, 0.0).astype(BF16)
            hg_scr[...] = jnp.dot(pick, h_scr[...], preferred_element_type=F32).astype(BF16)
            comb_hi = comb.astype(BF16)
            comb_lo = (comb - comb_hi.astype(F32)).astype(BF16)
            cw_scr[...] = (jnp.dot(pick, comb_hi, preferred_element_type=F32)
                           + jnp.dot(pick, comb_lo, preferred_element_type=F32))
            y_scr[...] = jnp.zeros(y_scr.shape, F32)

        total = None
        for j in range(EXPERTS_PER_STEP):
            t = expert_rows(hg_scr[...], cw_scr[...], j)
            total = t if total is None else total + t
        y_scr[...] += total

        @pl.when(last_of_group)
        def _():
            in_group = jnp.sum(jnp.where(lane == group, mcol_scr[...], 0.0), axis=1, keepdims=True) > 0.0
            rank = jnp.sum(jnp.where(lane == group, rcol_scr[...], 0.0), axis=1, keepdims=True)
            slot_cols = lax.broadcasted_iota(jnp.int32, (rows, GROUP_SLOTS), 1).astype(F32)
            place = jnp.where(in_group & (slot_cols == rank), 1.0, 0.0).astype(BF16)
            y = y_scr[...]
            y_hi = y.astype(BF16)
            y_lo = (y - y_hi.astype(F32)).astype(BF16)
            acc_scr[...] += (jnp.dot(place, y_hi, preferred_element_type=F32)
                             + jnp.dot(place, y_lo, preferred_element_type=F32))

    @pl.when(step == pl.num_programs(1) - 1)
    def _():
        y = x_ref[...] + acc_scr[...]
        if final_norm:
            ms = jnp.mean(y * y, axis=-1, keepdims=True)
            y = y * lax.rsqrt(ms + EPS) * gf_ref[...]
        o_ref[...] = y


def _moe(x1, norm2_g, comb, w_gu, w_d, norm_f_g, final_norm):
    n, d = x1.shape
    rows = min(MOE_ROWS, n)
    n_exp = w_gu.shape[0]
    row = lambda w: pl.BlockSpec((rows, w), lambda i, e: (i, 0))
    vec = pl.BlockSpec((1, d), lambda i, e: (0, 0))
    return pl.pallas_call(
        functools.partial(_moe_kernel, final_norm=final_norm),
        grid=(n // rows, n_exp // EXPERTS_PER_STEP),
        in_specs=[row(d), vec, row(LANES),
                  pl.BlockSpec((EXPERTS_PER_STEP, d, 2 * D_EXPERT), lambda i, e: (e, 0, 0)),
                  pl.BlockSpec((EXPERTS_PER_STEP, D_EXPERT, d), lambda i, e: (e, 0, 0)),
                  vec],
        out_specs=row(d),
        out_shape=jax.ShapeDtypeStruct((n, d), F32),
        scratch_shapes=[pltpu.VMEM((rows, d), BF16), pltpu.VMEM((rows, d), F32),
                        pltpu.VMEM((ONES_ROWS, rows), F32), pltpu.VMEM((ONES_ROWS, rows), F32),
                        pltpu.VMEM((rows, LANES), F32), pltpu.VMEM((rows, LANES), F32),
                        pltpu.VMEM((GROUP_SLOTS, d), BF16), pltpu.VMEM((GROUP_SLOTS, LANES), F32),
                        pltpu.VMEM((GROUP_SLOTS, d), F32), pltpu.SMEM((1,), jnp.int32)],
        compiler_params=_params("arbitrary", "arbitrary"),
        name="experts",
    )(x1, norm2_g.reshape(1, d), comb, w_gu, w_d, norm_f_g.reshape(1, d))


def kernel(x, positions, norm1_g, w_in, sgu_ln_g, sgu_ln_b, w_spatial, b_spatial, conv_w, w_branch,
           w_out, norm2_g, w_router_group, b_router_group, w_router_expert, b_router_expert,
           w_exp_gate, w_exp_up, w_exp_down, norm_f_g):
    batch, seq, d = x.shape
    depth = w_in.shape[0]
    n = batch * seq
    assert TILES_PER_STEP == 2
    assert seq % COUNT_CHUNK == 0 and seq % (TILES_PER_STEP * K_TILE) == 0
    assert n % PROJ_ROWS == 0 and n % MERGE_ROWS == 0
    x2 = x.reshape(n, d)
    tables = _rope_tables(positions)
    for l in range(depth):
        w_a, w_t, gate_off = _proj_weight(w_in[l])
        w_gate = w_in[l][:, gate_off:].astype(BF16)
        pad = jnp.zeros((d, LANES - N_EXPERTS - N_EXPERT_GROUPS), F32)
        w_router = jnp.concatenate([w_router_expert[l], w_router_group[l], pad], axis=1)
        w_router_hi = w_router.astype(BF16)
        w_router_lo = (w_router - w_router_hi.astype(F32)).astype(BF16)
        w_router = jnp.concatenate([w_router_hi, w_router_lo], axis=1)
        b_router = jnp.concatenate([b_router_expert[l], b_router_group[l], pad[0]]).reshape(1, LANES)
        qa, ka, vat, iq, ik, iwt, y_b, qc, kc, vct, y_d = _project(
            x2, norm1_g[l], w_a, w_t, tables, sgu_ln_g[l], sgu_ln_b[l], w_spatial[l], b_spatial[l],
            conv_w[l], seq)
        y_a = _dsa(iq, iwt, qa, ik, ka, vat, batch, seq)
        y_c = _stick_breaking(qc, kc, vct, batch, seq)
        x1, comb = _merge(x2, norm1_g[l], (y_a, y_b, y_c, y_d), w_gate, w_branch[l].astype(BF16),
                          w_out[l].astype(BF16), norm2_g[l], w_router, b_router)
        w_gu = jnp.concatenate([w_exp_gate[l], w_exp_up[l]], axis=2).astype(BF16)
        x2 = _moe(x1, norm2_g[l], comb, w_gu, w_exp_down[l].astype(BF16), norm_f_g,
                  final_norm=(l == depth - 1))
    return x2.reshape(batch, seq, d)
```

```python
import functools

import jax
import jax.numpy as jnp
from jax import lax
from jax.experimental import pallas as pl
from jax.experimental.pallas import tpu as pltpu

F32 = jnp.float32
BF16 = jnp.bfloat16

EPS = 1e-6
ROPE_THETA = 10000.0
HEAD_DIM = 64
BRANCH_WIDTH = 256
N_HEADS = 4
IDX_DIM = 32
TOPK_MAX = 256
CHUNK = 128
N_GROUPS_B = 4
CONV_WIDTH = 3
N_BRANCH = 4
N_EXPERT_GROUPS = 4
EXPERTS_PER_GROUP = 8
N_EXPERTS = N_EXPERT_GROUPS * EXPERTS_PER_GROUP
D_EXPERT = 256

LANES = 128
SUBLANES = 8
Q_BLOCK = 256
K_TILE = Q_BLOCK
TILES_PER_STEP = 2
COUNT_CHUNK = 512
COUNT_ROWS = 64
PROJ_ROWS = 256
MERGE_ROWS = 512
MOE_ROWS = 1024
EXPERTS_PER_STEP = 4
GROUP_SLOTS = 320
CONV_HALO = 8
VMEM_LIMIT = 56 * 1024 * 1024
PHANTOM_LOGIT = -1e4
NEG_BIG = -1e30
EXP_ZERO_BELOW = -110.0
INT_MIN = -2 ** 31
HALF_RANGE = 2 ** 15
ONES_ROWS = 16
BOUND_MARGIN = 1.02
BOUND_LIMIT = 40.0

_OFF = {}
_o = 0
for _name, _w in (("q", 512), ("k", 512), ("iq", 256), ("ik", 256),
                  ("ub", 256), ("vb", 256), ("qc", 256), ("kc", 256),
                  ("bd", 256), ("cd", 256), ("xd", 256)):
    _OFF[_name] = (_o, _w)
    _o += _w
W_A_COLS = _o
T_VA, T_VC, T_IW = 0, BRANCH_WIDTH, 2 * BRANCH_WIDTH
W_T_ROWS = 2 * BRANCH_WIDTH + 16

_CONTRACT_LAST = (((1,), (1,)), ((), ()))


def _params(*sem):
    return pltpu.CompilerParams(dimension_semantics=sem, vmem_limit_bytes=VMEM_LIMIT)


def _rope_table_kernel(pos_ref, inv_a_ref, inv_i_ref, cos_a, sin_a, cos_i, sin_i):
    p = pos_ref[...]
    a = p * inv_a_ref[...]
    cos_a[...] = jnp.cos(a)
    sin_a[...] = jnp.sin(a)
    b = p * inv_i_ref[...]
    cos_i[...] = jnp.cos(b)
    sin_i[...] = jnp.sin(b)


def _rope_tables(positions):
    n = positions.size
    rows = PROJ_ROWS
    pos = positions.astype(F32).reshape(n, 1)
    inv_a = ROPE_THETA ** (-jnp.arange(0, HEAD_DIM, 2, dtype=F32) / HEAD_DIM)
    inv_i = ROPE_THETA ** (-jnp.arange(0, IDX_DIM, 2, dtype=F32) / IDX_DIM)
    inv_a = jnp.tile(inv_a, LANES // inv_a.size).reshape(1, LANES)
    inv_i = jnp.tile(inv_i, LANES // inv_i.size).reshape(1, LANES)
    tab = jax.ShapeDtypeStruct((n, LANES), F32)
    row_spec = pl.BlockSpec((rows, LANES), lambda i: (i, 0))
    const = pl.BlockSpec((1, LANES), lambda i: (0, 0))
    return pl.pallas_call(
        _rope_table_kernel,
        grid=(n // rows,),
        in_specs=[pl.BlockSpec((rows, 1), lambda i: (i, 0)), const, const],
        out_specs=[row_spec] * 4,
        out_shape=[tab] * 4,
        compiler_params=_params("arbitrary"),
        name="rope_tables",
    )(pos, inv_a, inv_i)


def _rms_bf16(x, g):
    ms = jnp.mean(x * x, axis=-1, keepdims=True)
    return (x * lax.rsqrt(ms + EPS) * g).astype(BF16)


def _proj_kernel(x_ref, g_ref, w_ref, wt_ref, cos_a, sin_a, cos_i, sin_i, lng_ref, lnb_ref, ws_ref,
                 bs_ref, cw_ref,
                 qa_o, ka_o, vat_o, iq_o, ik_o, iwt_o, yb_o, qc_o, kc_o, vct_o, yd_o,
                 zbuf, *, steps_per_seq):
    rows = x_ref.shape[0]
    h = _rms_bf16(x_ref[...], g_ref[...])

    def mm(name):
        off, width = _OFF[name]
        return jnp.dot(h, w_ref[:, off:off + width], preferred_element_type=F32)

    def rope(name, cos_ref, sin_ref, out_ref, scale):
        p = mm(name)
        w = p.shape[1] // 2
        c = cos_ref[...]
        s = sin_ref[...]
        for j in range(w // LANES):
            lo = j * LANES
            r = p[:, lo:lo + LANES] * c + p[:, w + lo:w + lo + LANES] * s
            out_ref[:, lo:lo + LANES] = (r if scale == 1.0 else r * scale).astype(out_ref.dtype)

    att_scale = HEAD_DIM ** -0.5
    rope("q", cos_a, sin_a, qa_o, att_scale)
    rope("k", cos_a, sin_a, ka_o, 1.0)
    rope("iq", cos_i, sin_i, iq_o, 1.0)
    rope("ik", cos_i, sin_i, ik_o, 1.0)
    qc_o[...] = (mm("qc") * att_scale).astype(BF16)
    kc_o[...] = mm("kc").astype(BF16)
    t = lax.dot_general(wt_ref[...], h, _CONTRACT_LAST, preferred_element_type=F32)
    vat_o[...] = t[T_VA:T_VA + BRANCH_WIDTH, :].astype(BF16)
    vct_o[...] = t[T_VC:T_VC + BRANCH_WIDTH, :].astype(BF16)
    iwt_o[...] = t[T_IW:T_IW + SUBLANES, :]

    u = jax.nn.gelu(mm("ub"))
    v = jax.nn.gelu(mm("vb"))
    mu = jnp.mean(v, axis=-1, keepdims=True)
    var = jnp.mean(jnp.square(v - mu), axis=-1, keepdims=True)
    vn = ((v - mu) * lax.rsqrt(var + EPS) * lng_ref[...] + lnb_ref[...]).astype(BF16)
    t_idx = lax.broadcasted_iota(jnp.int32, (CHUNK, CHUNK), 0)
    s_idx = lax.broadcasted_iota(jnp.int32, (CHUNK, CHUNK), 1)
    causal = s_idx <= t_idx
    group_masks = [_head_lane_mask(BRANCH_WIDTH, BRANCH_WIDTH // N_GROUPS_B, g) for g in range(N_GROUPS_B)]
    w_tril = [jnp.where(causal, ws_ref[g], 0.0).astype(BF16) for g in range(N_GROUPS_B)]
    for c in range(rows // CHUNK):
        vc = vn[c * CHUNK:(c + 1) * CHUNK, :]
        mixed = bs_ref[...]
        for g in range(N_GROUPS_B):
            full = jnp.dot(w_tril[g], vc, preferred_element_type=F32)
            mixed = mixed + jnp.where(group_masks[g], full, 0.0)
        yb_o[c * CHUNK:(c + 1) * CHUNK, :] = (u[c * CHUNK:(c + 1) * CHUNK, :] * mixed).astype(BF16)

    bd = mm("bd")
    z = mm("cd") * mm("xd")

    @pl.when(pl.program_id(0) % steps_per_seq == 0)
    def _():
        zbuf[0:CONV_HALO, :] = jnp.zeros((CONV_HALO, BRANCH_WIDTH), F32)

    zbuf[CONV_HALO:CONV_HALO + rows, :] = z
    y = cw_ref[CONV_WIDTH - 1:CONV_WIDTH, :] * z
    for tap in range(CONV_WIDTH - 1):
        shift = CONV_WIDTH - 1 - tap
        y = y + cw_ref[tap:tap + 1, :] * zbuf[CONV_HALO - shift:CONV_HALO - shift + rows, :]
    yd_o[...] = (bd * y).astype(BF16)
    zbuf[0:CONV_HALO, :] = z[rows - CONV_HALO:rows, :]


def _swap_halves(w, dim):
    d_in, cols = w.shape
    wh = w.reshape(d_in, cols // dim, 2, dim // 2)
    return jnp.stack([-wh[:, :, 1], wh[:, :, 0]], axis=2).reshape(d_in, cols)


def _proj_weight(w_in):
    d = w_in.shape[0]
    sizes = (BRANCH_WIDTH,) * 3 + (N_HEADS * IDX_DIM, IDX_DIM, N_HEADS) + (BRANCH_WIDTH,) * 8
    parts, o = [], 0
    for s in sizes:
        parts.append(w_in[:, o:o + s])
        o += s
    qa, ka, va, iq, ik, iw, ub, vb, qc, kc, vc, bd, cd, xd = parts
    ik_rep = jnp.tile(ik, (1, N_HEADS))
    cols = [qa, _swap_halves(qa, HEAD_DIM), ka, _swap_halves(ka, HEAD_DIM),
            iq, _swap_halves(iq, IDX_DIM), ik_rep, _swap_halves(ik_rep, IDX_DIM),
            ub, vb, qc, kc, bd, cd, xd]
    iw_pad = jnp.concatenate([iw, jnp.zeros((d, W_T_ROWS - 2 * BRANCH_WIDTH - N_HEADS), w_in.dtype)], axis=1)
    w_t = jnp.concatenate([va, vc, iw_pad], axis=1).T
    return jnp.concatenate(cols, axis=1).astype(BF16), w_t.astype(BF16), o


def _project(x2, norm_g, w_a, w_t, tables, ln_g, ln_b, w_spatial, b_spatial, conv_w, seq):
    n, d = x2.shape
    rows = PROJ_ROWS
    cos_a, sin_a, cos_i, sin_i = tables
    bias = jnp.repeat(b_spatial.T, BRANCH_WIDTH // N_GROUPS_B, axis=1)
    row = lambda w: pl.BlockSpec((rows, w), lambda i: (i, 0))
    col = lambda r: pl.BlockSpec((r, rows), lambda i: (0, i))
    full = lambda a: pl.BlockSpec(a.shape, lambda i: (0,) * a.ndim)
    bf = lambda w: jax.ShapeDtypeStruct((n, w), BF16)
    bft = jax.ShapeDtypeStruct((BRANCH_WIDTH, n), BF16)
    args = (x2, norm_g.reshape(1, d), w_a, w_t, cos_a, sin_a, cos_i, sin_i, ln_g.reshape(1, -1),
            ln_b.reshape(1, -1), w_spatial, bias, conv_w)
    in_specs = [row(d), full(args[1]), full(w_a), full(w_t), row(LANES), row(LANES), row(LANES),
                row(LANES), full(args[8]), full(args[9]), full(w_spatial), full(bias), full(conv_w)]
    out_shape = [bf(256), bf(256), bft, bf(128), bf(128), jax.ShapeDtypeStruct((SUBLANES, n), F32),
                 bf(256), bf(256), bf(256), bft, bf(256)]
    out_specs = [row(256), row(256), col(BRANCH_WIDTH), row(128), row(128), col(SUBLANES),
                 row(256), row(256), row(256), col(BRANCH_WIDTH), row(256)]
    return pl.pallas_call(
        functools.partial(_proj_kernel, steps_per_seq=seq // rows),
        grid=(n // rows,),
        in_specs=in_specs,
        out_specs=out_specs,
        out_shape=out_shape,
        scratch_shapes=[pltpu.VMEM((CONV_HALO + rows, BRANCH_WIDTH), F32)],
        compiler_params=_params("arbitrary"),
        name="project",
    )(*args)


def _head_lane_mask(width, per_head, h, rows=1):
    lane = lax.broadcasted_iota(jnp.int32, (rows, width), 1)
    return (lane >= h * per_head) & (lane < (h + 1) * per_head)


def _masked_heads(x, per_head):
    return [jnp.where(_head_lane_mask(x.shape[1], per_head, h), x, jnp.zeros_like(x))
            for h in range(N_HEADS)]


def _float_to_key(x):
    bits = lax.bitcast_convert_type(x, jnp.int32)
    sign = bits >> 31
    return (bits ^ (sign & jnp.int32(0x7FFFFFFF))) - sign


def _key_to_float(key):
    bits = jnp.where(key < 0, (-key) | jnp.int32(INT_MIN), key)
    return lax.bitcast_convert_type(bits, F32)


def _neg_abs(x):
    bits = lax.bitcast_convert_type(x, jnp.int32) | jnp.int32(INT_MIN)
    return lax.bitcast_convert_type(bits, F32)


def _col_reduce(x, op, final):
    while x.shape[0] > SUBLANES:
        half = x.shape[0] // 2
        x = op(x[:half], x[half:])
    return final(x, axis=0, keepdims=True)


def _col_sum(x):
    return _col_reduce(x, jnp.add, jnp.sum)


def _col_max(x):
    return _col_reduce(x, jnp.maximum, jnp.max)


def _step_starts(step):
    return [pl.multiple_of((step * TILES_PER_STEP + g) * K_TILE, K_TILE) for g in range(TILES_PER_STEP)]


def _dsa_kernel(iq_ref, iwt_ref, q_ref, ik_ref, k_ref, vt_ref, o_ref, sc_ref, hi_ref, lo_ref, kmax_ref, *,
                topk):
    qb = q_ref.shape[0]
    i = pl.program_id(1)
    n_tiles = i + 1
    n_steps = (n_tiles + TILES_PER_STEP - 1) // TILES_PER_STEP

    @pl.when(i == 0)
    def _():
        sc_ref[...] = jnp.full(sc_ref.shape, -jnp.inf, F32)
        hi_ref[...] = jnp.full(hi_ref.shape, -HALF_RANGE, jnp.int16)
        lo_ref[...] = jnp.full(lo_ref.shape, -HALF_RANGE, jnp.int16)

    iq_heads = _masked_heads(iq_ref[...], IDX_DIM)
    iwt = iwt_ref[...]
    iw_heads = [iwt[h:h + 1, :] for h in range(N_HEADS)]

    def put_scores(start, sc):
        sc_ref[pl.ds(start, K_TILE), :] = sc
        key = _float_to_key(sc)
        hi_ref[pl.ds(start, K_TILE), :] = (key >> 16).astype(jnp.int16)
        lo_ref[pl.ds(start, K_TILE), :] = ((key & 0xFFFF) - HALF_RANGE).astype(jnp.int16)

    key_off = lax.broadcasted_iota(jnp.int32, (K_TILE, qb), 0)
    qry_pos = i * qb + lax.broadcasted_iota(jnp.int32, (K_TILE, qb), 1)

    def score_step(jj, masked):
        starts = _step_starts(jj)
        dots = [[lax.dot_general(ik_ref[pl.ds(st, K_TILE), :], iq_heads[h], _CONTRACT_LAST,
                                 preferred_element_type=F32) for h in range(N_HEADS)] for st in starts]
        for g in range(TILES_PER_STEP):
            sc = None
            for h in range(N_HEADS):
                t = iw_heads[h] * jnp.maximum(dots[g][h], 0.0)
                sc = t if sc is None else sc + t
            if masked:
                sc = jnp.where(starts[g] + key_off <= qry_pos, sc, -jnp.inf)
            put_scores(starts[g], sc)

    def full_step(jj, carry):
        score_step(jj, False)
        return carry

    lax.fori_loop(0, i // TILES_PER_STEP, full_step, 0)
    score_step(i // TILES_PER_STEP, True)

    n_chunks = (n_tiles * K_TILE + COUNT_CHUNK - 1) // COUNT_CHUNK
    kf = jnp.float32(topk)

    def chunk_rows(c):
        base = pl.multiple_of(c * COUNT_CHUNK, COUNT_CHUNK)
        return [pl.ds(base + r * COUNT_ROWS, COUNT_ROWS) for r in range(COUNT_CHUNK // COUNT_ROWS)]

    def count(ref, pred, dtype):
        one, zero = jnp.ones((), dtype), jnp.zeros((), dtype)

        def body(c, acc):
            for rows in chunk_rows(c):
                acc = acc + jnp.where(pred(ref[rows, :]), one, zero)
            return acc
        acc = lax.fori_loop(0, n_chunks, body, jnp.zeros((COUNT_ROWS, qb), dtype))
        return jnp.sum(acc.astype(F32), axis=0, keepdims=True)

    def bisect16(ref, target):
        def bit_step(b, carry):
            best, cnt_best, cnt_above = carry
            cand = best + lax.shift_left(jnp.int32(1), 15 - b)
            cand16 = cand.astype(jnp.int16)
            cnt = count(ref, lambda blk: blk >= cand16, jnp.int16)
            ok = cnt >= target
            return jnp.where(ok, cand, best), jnp.where(ok, cnt, cnt_best), jnp.where(ok, cnt_above, cnt)
        init = (jnp.full((1, qb), -HALF_RANGE, jnp.int32), jnp.zeros((1, qb), F32), jnp.zeros((1, qb), F32))
        return lax.fori_loop(0, 16, bit_step, init)

    hi_t, cnt_hi, above = bisect16(hi_ref, kf)
    hi_t16 = hi_t.astype(jnp.int16)

    def mask_low(c, carry):
        for rows in chunk_rows(c):
            lo_ref[rows, :] = jnp.where(hi_ref[rows, :] == hi_t16, lo_ref[rows, :], jnp.int16(-HALF_RANGE))
        return carry

    lax.fori_loop(0, n_chunks, mask_low, 0)
    lo_t, cnt_lo, _ = bisect16(lo_ref, kf - above)
    cnt_key = above + jnp.where(lo_t == -HALF_RANGE, cnt_hi - above, cnt_lo)
    key = hi_t * (2 * HALF_RANGE) + (lo_t + HALF_RANGE)
    few = i * qb + lax.broadcasted_iota(jnp.int32, (1, qb), 1) + 1 < topk
    thr = jnp.where(few, jnp.finfo(F32).min, _key_to_float(key))

    @pl.when(jnp.max(jnp.where(few, kf, cnt_key)) > kf)
    def _():
        need = kf - count(sc_ref, lambda blk: blk > thr, F32)
        before = (lax.broadcasted_iota(jnp.int32, (K_TILE, K_TILE), 1)
                  < lax.broadcasted_iota(jnp.int32, (K_TILE, K_TILE), 0))
        before = jnp.where(before, 1.0, 0.0).astype(BF16)

        def tie_body(jj, seen):
            starts = _step_starts(jj)
            blks = [sc_ref[pl.ds(st, K_TILE), :] for st in starts]
            eqs = [blk == thr for blk in blks]
            eq_fs = [jnp.where(eq, 1.0, 0.0) for eq in eqs]
            ranks = [jnp.dot(before, eq_f.astype(BF16), preferred_element_type=F32) for eq_f in eq_fs]
            for g in range(TILES_PER_STEP):
                drop = eqs[g] & (seen + ranks[g] >= need)
                sc_ref[pl.ds(starts[g], K_TILE), :] = jnp.where(drop, -jnp.inf, blks[g])
                seen = seen + _col_sum(eq_fs[g])
            return seen

        lax.fori_loop(0, n_steps, tie_body, jnp.zeros((1, qb), F32))

    q = q_ref[...]
    q_heads = _masked_heads(q, HEAD_DIM)
    head_rows = jnp.concatenate(
        [jnp.where(_head_lane_mask(BRANCH_WIDTH, HEAD_DIM, h), 1.0, 0.0) for h in range(N_HEADS)]
        + [jnp.zeros((ONES_ROWS - N_HEADS, BRANCH_WIDTH), F32)], axis=0).astype(BF16)
    qf = q.astype(F32)
    q_sq = lax.dot_general(head_rows, (qf * qf).astype(BF16), _CONTRACT_LAST,
                           preferred_element_type=F32)
    kf32 = k_ref[pl.ds(pl.multiple_of(i * K_TILE, K_TILE), K_TILE), :].astype(F32)
    k_sq = lax.dot_general(head_rows, (kf32 * kf32).astype(BF16), _CONTRACT_LAST,
                           preferred_element_type=F32)
    k_sq_max = jnp.max(k_sq, axis=1, keepdims=True)

    @pl.when(i == 0)
    def _():
        kmax_ref[...] = jnp.zeros(kmax_ref.shape, F32)

    kmax_ref[...] = jnp.maximum(kmax_ref[...], jnp.broadcast_to(k_sq_max, kmax_ref.shape))
    bound = jnp.sqrt(q_sq * kmax_ref[:, 0:1]) * BOUND_MARGIN
    bounds = [bound[h:h + 1, :] for h in range(N_HEADS)]
    ones_rows = jnp.ones((ONES_ROWS, K_TILE), BF16)

    def step_logits(jj):
        starts = _step_starts(jj)
        sels = [sc_ref[pl.ds(st, K_TILE), :] >= thr for st in starts]
        logits = [[lax.dot_general(k_ref[pl.ds(st, K_TILE), :], q_heads[h], _CONTRACT_LAST,
                                   preferred_element_type=F32) for h in range(N_HEADS)] for st in starts]
        masked = [[jnp.where(sels[g], logits[g][h], NEG_BIG) for g in range(TILES_PER_STEP)]
                  for h in range(N_HEADS)]
        return starts, masked

    def value_products(starts, probs, h):
        pv = None
        for g, st in enumerate(starts):
            vt = vt_ref[h * HEAD_DIM:(h + 1) * HEAD_DIM, pl.ds(st, K_TILE)]
            t = jnp.dot(vt, probs[g], preferred_element_type=F32)
            pv = t if pv is None else pv + t
        return pv

    def bounded_step(jj, carry):
        l, acc = carry
        starts, masked = step_logits(jj)
        probs = [[jnp.exp(s - bounds[h]).astype(BF16) for s in masked[h]] for h in range(N_HEADS)]
        l_new, acc_new = [], []
        for h in range(N_HEADS):
            lh = l[h]
            for pb in probs[h]:
                lh = lh + jnp.dot(ones_rows, pb, preferred_element_type=F32)[0:1]
            l_new.append(lh)
            acc_new.append(acc[h] + value_products(starts, probs[h], h))
        return tuple(l_new), tuple(acc_new)

    def online_step(jj, carry):
        m, l, acc = carry
        starts, masked = step_logits(jj)
        m_new, l_new, acc_new = [], [], []
        for h in range(N_HEADS):
            mh = m[h]
            for s in masked[h]:
                mh = jnp.maximum(mh, _col_max(s))
            ps = [jnp.exp(s - mh) for s in masked[h]]
            alpha = jnp.exp(m[h] - mh)
            lh = alpha * l[h]
            for p in ps:
                lh = lh + _col_sum(p)
            m_new.append(mh)
            l_new.append(lh)
            acc_new.append(alpha * acc[h] + value_products(starts, [p.astype(BF16) for p in ps], h))
        return tuple(m_new), tuple(l_new), tuple(acc_new)

    def finish(l, acc):
        out_t = jnp.concatenate([acc[h] / l[h] for h in range(N_HEADS)], axis=0)
        o_ref[...] = out_t.T.astype(o_ref.dtype)

    zeros_l = tuple(jnp.zeros((1, qb), F32) for _ in range(N_HEADS))
    zeros_acc = tuple(jnp.zeros((HEAD_DIM, qb), F32) for _ in range(N_HEADS))
    small = jnp.max(bound[0:N_HEADS, :]) <= BOUND_LIMIT

    @pl.when(small)
    def _():
        finish(*lax.fori_loop(0, n_steps, bounded_step, (zeros_l, zeros_acc)))

    @pl.when(jnp.logical_not(small))
    def _():
        init = (tuple(jnp.full((1, qb), NEG_BIG, F32) for _ in range(N_HEADS)), zeros_l, zeros_acc)
        _, l, acc = lax.fori_loop(0, n_steps, online_step, init)
        finish(l, acc)


def _dsa(iq, iwt, qa, ik, ka, vat, batch, seq):
    n = qa.shape[0]
    nq = seq // Q_BLOCK
    topk = min(TOPK_MAX, seq // 4)
    qrow = lambda w: pl.BlockSpec((Q_BLOCK, w), lambda b, i: (b * nq + i, 0))
    whole = lambda w: pl.BlockSpec((seq, w), lambda b, i: (b, 0))
    return pl.pallas_call(
        functools.partial(_dsa_kernel, topk=topk),
        grid=(batch, nq),
        in_specs=[qrow(LANES), pl.BlockSpec((SUBLANES, Q_BLOCK), lambda b, i: (0, b * nq + i)),
                  qrow(BRANCH_WIDTH), whole(LANES), whole(BRANCH_WIDTH),
                  pl.BlockSpec((BRANCH_WIDTH, seq), lambda b, i: (0, b))],
        out_specs=qrow(BRANCH_WIDTH),
        out_shape=jax.ShapeDtypeStruct((n, BRANCH_WIDTH), BF16),
        scratch_shapes=[pltpu.VMEM((seq, Q_BLOCK), F32), pltpu.VMEM((seq, Q_BLOCK), jnp.int16),
                        pltpu.VMEM((seq, Q_BLOCK), jnp.int16), pltpu.VMEM((ONES_ROWS, LANES), F32)],
        compiler_params=_params("arbitrary", "arbitrary"),
        name="dsa",
    )(iq, iwt, qa, ik, ka, vat)


def _stick_kernel(q_ref, k_ref, vt_ref, o_ref):
    qb = q_ref.shape[0]
    i = pl.program_id(1)
    q_heads = _masked_heads(q_ref[...], HEAD_DIM)
    after = (lax.broadcasted_iota(jnp.int32, (K_TILE, K_TILE), 1)
             > lax.broadcasted_iota(jnp.int32, (K_TILE, K_TILE), 0))
    after = jnp.where(after, 1.0, 0.0).astype(BF16)
    after = jnp.concatenate([after, jnp.ones((ONES_ROWS, K_TILE), BF16)], axis=0)
    key_off = lax.broadcasted_iota(jnp.int32, (K_TILE, qb), 0)
    qry_pos = i * qb + lax.broadcasted_iota(jnp.int32, (K_TILE, qb), 1)

    def step(u, carry, masked):
        tails, acc = carry
        first = i - 2 * u
        tiles = [first, jnp.maximum(first - 1, 0)]
        bias = [None, jnp.where(first >= 1, 0.0, PHANTOM_LOGIT)]
        starts = [pl.multiple_of(t * K_TILE, K_TILE) for t in tiles]
        stricts = [starts[g] + key_off < qry_pos if masked else None for g in range(2)]
        zs = [[lax.dot_general(k_ref[pl.ds(st, K_TILE), :], q_heads[h], _CONTRACT_LAST,
                               preferred_element_type=F32) for h in range(N_HEADS)] for st in starts]
        log_betas, splits = [], []
        for g in range(2):
            lbs, sps = [], []
            for h in range(N_HEADS):
                z = zs[g][h] if bias[g] is None else zs[g][h] + bias[g]
                log_beta = jnp.minimum(z, 0.0) - jnp.log(1.0 + jnp.exp(_neg_abs(z)))
                log_keep = log_beta - z
                if masked:
                    log_keep = jnp.where(stricts[g], log_keep, 0.0)
                lbs.append(log_beta)
                sps.append(log_keep.astype(BF16))
            log_betas.append(lbs)
            splits.append(sps)
        sums = [[jnp.dot(after, splits[g][h], preferred_element_type=F32) for h in range(N_HEADS)]
                for g in range(2)]
        tails = list(tails)
        new_acc = []
        for h in range(N_HEADS):
            total = acc[h]
            for g in range(2):
                a = jnp.exp(log_betas[g][h] + sums[g][h][:K_TILE] + tails[h])
                if masked:
                    a = jnp.where(stricts[g], a, 0.0)
                tails[h] = tails[h] + sums[g][h][K_TILE:K_TILE + 1]
                vt = vt_ref[h * HEAD_DIM:(h + 1) * HEAD_DIM, pl.ds(starts[g], K_TILE)]
                total = total + jnp.dot(vt, a.astype(BF16), preferred_element_type=F32)
            new_acc.append(total)
        return tuple(tails), tuple(new_acc)

    init = (tuple(jnp.zeros((1, qb), F32) for _ in range(N_HEADS)),
            tuple(jnp.zeros((HEAD_DIM, qb), F32) for _ in range(N_HEADS)))
    carry = step(0, init, True)

    def more(state):
        u, (tails, _) = state
        live = tails[0]
        for h in range(1, N_HEADS):
            live = jnp.maximum(live, tails[h])
        return (u < (i + 2) // 2) & (jnp.max(live) > EXP_ZERO_BELOW)

    _, (_, acc) = lax.while_loop(more, lambda st: (st[0] + 1, step(st[0], st[1], False)), (jnp.int32(1), carry))
    o_ref[...] = jnp.concatenate(acc, axis=0).T.astype(o_ref.dtype)


def _stick_breaking(qc, kc, vct, batch, seq):
    n = qc.shape[0]
    nq = seq // Q_BLOCK
    qrow = pl.BlockSpec((Q_BLOCK, BRANCH_WIDTH), lambda b, i: (b * nq + i, 0))
    return pl.pallas_call(
        _stick_kernel,
        grid=(batch, nq),
        in_specs=[qrow, pl.BlockSpec((seq, BRANCH_WIDTH), lambda b, i: (b, 0)),
                  pl.BlockSpec((BRANCH_WIDTH, seq), lambda b, i: (0, b))],
        out_specs=qrow,
        out_shape=jax.ShapeDtypeStruct((n, BRANCH_WIDTH), BF16),
        compiler_params=_params("arbitrary", "arbitrary"),
        name="stick_breaking",
    )(qc, kc, vct)


def _merge_kernel(x_ref, g1_ref, ya_ref, yb_ref, yc_ref, yd_ref, wg_ref, wb_ref, wo_ref,
                  g2_ref, wr_ref, br_ref, x1_o, comb_o):
    x = x_ref[...]
    d = x.shape[1]
    h = _rms_bf16(x, g1_ref[...])
    ys = (ya_ref[...], yb_ref[...], yc_ref[...], yd_ref[...])
    col = 256
    pieces = []
    for c in range(d // col):
        m = None
        for n in range(N_BRANCH):
            logit = jnp.dot(h, wg_ref[:, n * d + c * col:n * d + (c + 1) * col],
                            preferred_element_type=F32)
            br = jnp.dot(ys[n], wb_ref[n, :, c * col:(c + 1) * col], preferred_element_type=F32)
            t = jax.nn.sigmoid(logit) * br
            m = t if m is None else m + t
        pieces.append(m.astype(BF16))
    merged = jnp.concatenate(pieces, axis=1)
    x1 = x + jnp.dot(merged, wo_ref[...], preferred_element_type=F32)
    x1_o[...] = x1

    ms = jnp.mean(x1 * x1, axis=-1, keepdims=True)
    h2 = x1 * lax.rsqrt(ms + EPS) * g2_ref[...]
    h_hi = h2.astype(BF16)
    h_lo = (h2 - h_hi.astype(F32)).astype(BF16)
    both = jnp.dot(h_hi, wr_ref[...], preferred_element_type=F32)
    cross = jnp.dot(h_lo, wr_ref[:, 0:LANES], preferred_element_type=F32)
    logits = both[:, 0:LANES] + both[:, LANES:2 * LANES] + cross + br_ref[...]
    rows = logits.shape[0]
    lane = lax.broadcasted_iota(jnp.int32, (rows, LANES), 1)
    neg = -jnp.inf
    is_grp = (lane >= N_EXPERTS) & (lane < N_EXPERTS + N_EXPERT_GROUPS)
    gl = jnp.where(is_grp, logits, neg)
    gmax = jnp.max(gl, axis=1, keepdims=True)
    gidx = jnp.min(jnp.where(gl == gmax, lane, LANES), axis=1, keepdims=True) - N_EXPERTS
    grp_p = 1.0 / jnp.sum(jnp.where(is_grp, jnp.exp(gl - gmax), 0.0), axis=1, keepdims=True)
    in_grp = (lane >= gidx * EXPERTS_PER_GROUP) & (lane < (gidx + 1) * EXPERTS_PER_GROUP)
    el = jnp.where(in_grp, logits, neg)
    l1 = jnp.max(el, axis=1, keepdims=True)
    j1 = jnp.min(jnp.where(el == l1, lane, LANES), axis=1, keepdims=True)
    el2 = jnp.where(lane == j1, neg, el)
    l2 = jnp.max(el2, axis=1, keepdims=True)
    j2 = jnp.min(jnp.where(el2 == l2, lane, LANES), axis=1, keepdims=True)
    e2 = jnp.exp(l2 - l1)
    w1 = grp_p / (1.0 + e2)
    w2 = grp_p * e2 / (1.0 + e2)
    comb_o[...] = jnp.where(lane == j1, w1, 0.0) + jnp.where(lane == j2, w2, 0.0)


def _merge(x2, norm1_g, ys, w_gate, w_branch, w_out, norm2_g, w_router, b_router):
    n, d = x2.shape
    rows = MERGE_ROWS
    row = lambda w: pl.BlockSpec((rows, w), lambda i: (i, 0))
    full = lambda a: pl.BlockSpec(a.shape, lambda i: (0,) * a.ndim)
    args = (x2, norm1_g.reshape(1, d), *ys, w_gate, w_branch, w_out, norm2_g.reshape(1, d),
            w_router, b_router)
    in_specs = [row(d), full(args[1])] + [row(BRANCH_WIDTH)] * 4 + [full(a) for a in args[6:]]
    return pl.pallas_call(
        _merge_kernel,
        grid=(n // rows,),
        in_specs=in_specs,
        out_specs=[row(d), row(LANES)],
        out_shape=[jax.ShapeDtypeStruct((n, d), F32), jax.ShapeDtypeStruct((n, LANES), F32)],
        compiler_params=_params("arbitrary"),
        name="merge_router",
    )(*args)


def _moe_kernel(x_ref, g2_ref, comb_ref, wgu_ref, wd_ref, gf_ref, o_ref,
                h_scr, acc_scr, memb_scr, rank_scr, mcol_scr, rcol_scr, hg_scr, cw_scr, y_scr, flag_scr,
                *, final_norm):
    step = pl.program_id(1)
    rows = x_ref.shape[0]
    steps_per_group = EXPERTS_PER_GROUP // EXPERTS_PER_STEP
    group = step // steps_per_group
    first_of_group = step % steps_per_group == 0
    last_of_group = step % steps_per_group == steps_per_group - 1
    lane = lax.broadcasted_iota(jnp.int32, (rows, LANES), 1)

    @pl.when(step == 0)
    def _():
        h_scr[...] = _rms_bf16(x_ref[...], g2_ref[...])
        acc_scr[...] = jnp.zeros(acc_scr.shape, F32)
        comb = comb_ref[...]
        member_cols = [jnp.sum(jnp.where((lane >= g * EXPERTS_PER_GROUP) & (lane < (g + 1) * EXPERTS_PER_GROUP),
                                         comb, 0.0), axis=1, keepdims=True) > 0.0 for g in range(N_EXPERT_GROUPS)]
        mcol = jnp.zeros((rows, LANES), F32)
        for g in range(N_EXPERT_GROUPS):
            mcol = jnp.where((lane == g) & member_cols[g], 1.0, mcol)
        mcol_scr[...] = mcol
        t_row = lax.broadcasted_iota(jnp.int32, (rows, rows), 0)
        t_col = lax.broadcasted_iota(jnp.int32, (rows, rows), 1)
        earlier = jnp.where(t_col < t_row, 1.0, 0.0).astype(BF16)
        rcol_scr[...] = jnp.dot(earlier, mcol.astype(BF16), preferred_element_type=F32)
        memb = mcol.T[0:ONES_ROWS, :]
        memb_scr[...] = memb
        later = jnp.where(t_row < t_col, 1.0, 0.0).astype(BF16)
        rank_scr[...] = jnp.dot(memb.astype(BF16), later, preferred_element_type=F32)
        counts = jnp.sum(memb, axis=1, keepdims=True)
        for g in range(N_EXPERT_GROUPS):
            flag_scr[g] = (jnp.max(counts[g:g + 1, :]) > GROUP_SLOTS).astype(jnp.int32)

    crowded = flag_scr[group] == 1
    comb = comb_ref[...]

    def expert_rows(hb, cw, j):
        e = step * EXPERTS_PER_STEP + j
        lane_c = lax.broadcasted_iota(jnp.int32, cw.shape, 1)
        w = jnp.sum(jnp.where(lane_c == e, cw, 0.0), axis=1, keepdims=True)
        gu = jnp.dot(hb, wgu_ref[j], preferred_element_type=F32)
        hh = (jax.nn.silu(gu[:, :D_EXPERT]) * gu[:, D_EXPERT:] * w).astype(BF16)
        return jnp.dot(hh, wd_ref[j], preferred_element_type=F32)

    @pl.when(crowded)
    def _():
        total = None
        for j in range(EXPERTS_PER_STEP):
            t = expert_rows(h_scr[...], comb, j)
            total = t if total is None else total + t
        acc_scr[...] += total

    @pl.when(jnp.logical_not(crowded))
    def _():
        slot_rows = lax.broadcasted_iota(jnp.int32, (GROUP_SLOTS, rows), 0).astype(F32)

        @pl.when(first_of_group)
        def _():
            in_group = memb_scr[pl.ds(group, 1), :] > 0.0
            rank = rank_scr[pl.ds(group, 1), :]
            pick = jnp.where(in_group & (slot_rows == rank), 1.0, 0.0).astype(BF16)
            hg_scr[...] = jnp.dot(pick, h_scr[...], preferred_element_type=F32).astype(BF16)
            comb_hi = comb.astype(BF16)
            comb_lo = (comb - comb_hi.astype(F32)).astype(BF16)
            cw_scr[...] = (jnp.dot(pick, comb_hi, preferred_element_type=F32)
                           + jnp.dot(pick, comb_lo, preferred_element_type=F32))
            y_scr[...] = jnp.zeros(y_scr.shape, F32)

        total = None
        for j in range(EXPERTS_PER_STEP):
            t = expert_rows(hg_scr[...], cw_scr[...], j)
            total = t if total is None else total + t
        y_scr[...] += total

        @pl.when(last_of_group)
        def _():
            in_group = jnp.sum(jnp.where(lane == group, mcol_scr[...], 0.0), axis=1, keepdims=True) > 0.0
            rank = jnp.sum(jnp.where(lane == group, rcol_scr[...], 0.0), axis=1, keepdims=True)
            slot_cols = lax.broadcasted_iota(jnp.int32, (rows, GROUP_SLOTS), 1).astype(F32)
            place = jnp.where(in_group & (slot_cols == rank), 1.0, 0.0).astype(BF16)
            y = y_scr[...]
            y_hi = y.astype(BF16)
            y_lo = (y - y_hi.astype(F32)).astype(BF16)
            acc_scr[...] += (jnp.dot(place, y_hi, preferred_element_type=F32)
                             + jnp.dot(place, y_lo, preferred_element_type=F32))

    @pl.when(step == pl.num_programs(1) - 1)
    def _():
        y = x_ref[...] + acc_scr[...]
        if final_norm:
            ms = jnp.mean(y * y, axis=-1, keepdims=True)
            y = y * lax.rsqrt(ms + EPS) * gf_ref[...]
        o_ref[...] = y


def _moe(x1, norm2_g, comb, w_gu, w_d, norm_f_g, final_norm):
    n, d = x1.shape
    rows = min(MOE_ROWS, n)
    n_exp = w_gu.shape[0]
    row = lambda w: pl.BlockSpec((rows, w), lambda i, e: (i, 0))
    vec = pl.BlockSpec((1, d), lambda i, e: (0, 0))
    return pl.pallas_call(
        functools.partial(_moe_kernel, final_norm=final_norm),
        grid=(n // rows, n_exp // EXPERTS_PER_STEP),
        in_specs=[row(d), vec, row(LANES),
                  pl.BlockSpec((EXPERTS_PER_STEP, d, 2 * D_EXPERT), lambda i, e: (e, 0, 0)),
                  pl.BlockSpec((EXPERTS_PER_STEP, D_EXPERT, d), lambda i, e: (e, 0, 0)),
                  vec],
        out_specs=row(d),
        out_shape=jax.ShapeDtypeStruct((n, d), F32),
        scratch_shapes=[pltpu.VMEM((rows, d), BF16), pltpu.VMEM((rows, d), F32),
                        pltpu.VMEM((ONES_ROWS, rows), F32), pltpu.VMEM((ONES_ROWS, rows), F32),
                        pltpu.VMEM((rows, LANES), F32), pltpu.VMEM((rows, LANES), F32),
                        pltpu.VMEM((GROUP_SLOTS, d), BF16), pltpu.VMEM((GROUP_SLOTS, LANES), F32),
                        pltpu.VMEM((GROUP_SLOTS, d), F32), pltpu.SMEM((N_EXPERT_GROUPS,), jnp.int32)],
        compiler_params=_params("arbitrary", "arbitrary"),
        name="experts",
    )(x1, norm2_g.reshape(1, d), comb, w_gu, w_d, norm_f_g.reshape(1, d))


def kernel(x, positions, norm1_g, w_in, sgu_ln_g, sgu_ln_b, w_spatial, b_spatial, conv_w, w_branch,
           w_out, norm2_g, w_router_group, b_router_group, w_router_expert, b_router_expert,
           w_exp_gate, w_exp_up, w_exp_down, norm_f_g):
    batch, seq, d = x.shape
    depth = w_in.shape[0]
    n = batch * seq
    assert TILES_PER_STEP == 2
    assert seq % COUNT_CHUNK == 0 and seq % (TILES_PER_STEP * K_TILE) == 0
    assert n % PROJ_ROWS == 0 and n % MERGE_ROWS == 0
    x2 = x.reshape(n, d)
    tables = _rope_tables(positions)
    for l in range(depth):
        w_a, w_t, gate_off = _proj_weight(w_in[l])
        w_gate = w_in[l][:, gate_off:].astype(BF16)
        pad = jnp.zeros((d, LANES - N_EXPERTS - N_EXPERT_GROUPS), F32)
        w_router = jnp.concatenate([w_router_expert[l], w_router_group[l], pad], axis=1)
        w_router_hi = w_router.astype(BF16)
        w_router_lo = (w_router - w_router_hi.astype(F32)).astype(BF16)
        w_router = jnp.concatenate([w_router_hi, w_router_lo], axis=1)
        b_router = jnp.concatenate([b_router_expert[l], b_router_group[l], pad[0]]).reshape(1, LANES)
        qa, ka, vat, iq, ik, iwt, y_b, qc, kc, vct, y_d = _project(
            x2, norm1_g[l], w_a, w_t, tables, sgu_ln_g[l], sgu_ln_b[l], w_spatial[l], b_spatial[l],
            conv_w[l], seq)
        y_a = _dsa(iq, iwt, qa, ik, ka, vat, batch, seq)
        y_c = _stick_breaking(qc, kc, vct, batch, seq)
        x1, comb = _merge(x2, norm1_g[l], (y_a, y_b, y_c, y_d), w_gate, w_branch[l].astype(BF16),
                          w_out[l].astype(BF16), norm2_g[l], w_router, b_router)
        w_gu = jnp.concatenate([w_exp_gate[l], w_exp_up[l]], axis=2).astype(BF16)
        x2 = _moe(x1, norm2_g[l], comb, w_gu, w_exp_down[l].astype(BF16), norm_f_g,
                  final_norm=(l == depth - 1))
    return x2.reshape(batch, seq, d)
```

```python
import functools

import jax
import jax.numpy as jnp
from jax import lax
from jax.experimental import pallas as pl
from jax.experimental.pallas import tpu as pltpu

F32 = jnp.float32
BF16 = jnp.bfloat16

EPS = 1e-6
ROPE_THETA = 10000.0
HEAD_DIM = 64
BRANCH_WIDTH = 256
N_HEADS = 4
IDX_DIM = 32
TOPK_MAX = 256
CHUNK = 128
N_GROUPS_B = 4
CONV_WIDTH = 3
N_BRANCH = 4
N_EXPERT_GROUPS = 4
EXPERTS_PER_GROUP = 8
N_EXPERTS = N_EXPERT_GROUPS * EXPERTS_PER_GROUP
D_EXPERT = 256

LANES = 128
SUBLANES = 8
Q_BLOCK = 256
K_TILE = Q_BLOCK
TILES_PER_STEP = 2
COUNT_CHUNK = 512
COUNT_ROWS = 64
PROJ_ROWS = 256
MERGE_ROWS = 512
MOE_ROWS = 1024
EXPERTS_PER_STEP = 4
GROUP_SLOTS = 320
CONV_HALO = 8
VMEM_LIMIT = 56 * 1024 * 1024
PHANTOM_LOGIT = -1e4
NEG_BIG = -1e30
EXP_ZERO_BELOW = -110.0
INT_MIN = -2 ** 31
HALF_RANGE = 2 ** 15
ONES_ROWS = 16
BOUND_MARGIN = 1.02
BOUND_LIMIT = 40.0

_OFF = {}
_o = 0
for _name, _w in (("q", 512), ("k", 512), ("iq", 256), ("ik", 256),
                  ("ub", 256), ("vb", 256), ("qc", 256), ("kc", 256),
                  ("bd", 256), ("cd", 256), ("xd", 256)):
    _OFF[_name] = (_o, _w)
    _o += _w
W_A_COLS = _o
T_VA, T_VC, T_IW = 0, BRANCH_WIDTH, 2 * BRANCH_WIDTH
W_T_ROWS = 2 * BRANCH_WIDTH + 16

_CONTRACT_LAST = (((1,), (1,)), ((), ()))


def _params(*sem):
    return pltpu.CompilerParams(dimension_semantics=sem, vmem_limit_bytes=VMEM_LIMIT)


def _rope_table_kernel(pos_ref, inv_a_ref, inv_i_ref, cos_a, sin_a, cos_i, sin_i):
    p = pos_ref[...]
    a = p * inv_a_ref[...]
    cos_a[...] = jnp.cos(a)
    sin_a[...] = jnp.sin(a)
    b = p * inv_i_ref[...]
    cos_i[...] = jnp.cos(b)
    sin_i[...] = jnp.sin(b)


def _rope_tables(positions):
    n = positions.size
    rows = PROJ_ROWS
    pos = positions.astype(F32).reshape(n, 1)
    inv_a = ROPE_THETA ** (-jnp.arange(0, HEAD_DIM, 2, dtype=F32) / HEAD_DIM)
    inv_i = ROPE_THETA ** (-jnp.arange(0, IDX_DIM, 2, dtype=F32) / IDX_DIM)
    inv_a = jnp.tile(inv_a, LANES // inv_a.size).reshape(1, LANES)
    inv_i = jnp.tile(inv_i, LANES // inv_i.size).reshape(1, LANES)
    tab = jax.ShapeDtypeStruct((n, LANES), F32)
    row_spec = pl.BlockSpec((rows, LANES), lambda i: (i, 0))
    const = pl.BlockSpec((1, LANES), lambda i: (0, 0))
    return pl.pallas_call(
        _rope_table_kernel,
        grid=(n // rows,),
        in_specs=[pl.BlockSpec((rows, 1), lambda i: (i, 0)), const, const],
        out_specs=[row_spec] * 4,
        out_shape=[tab] * 4,
        compiler_params=_params("arbitrary"),
        name="rope_tables",
    )(pos, inv_a, inv_i)


def _rms_bf16(x, g):
    ms = jnp.mean(x * x, axis=-1, keepdims=True)
    return (x * lax.rsqrt(ms + EPS) * g).astype(BF16)


def _proj_kernel(x_ref, g_ref, w_ref, wt_ref, cos_a, sin_a, cos_i, sin_i, lng_ref, lnb_ref, ws_ref,
                 bs_ref, cw_ref,
                 qa_o, ka_o, vat_o, iq_o, ik_o, iwt_o, yb_o, qc_o, kc_o, vct_o, yd_o,
                 zbuf, *, steps_per_seq):
    rows = x_ref.shape[0]
    h = _rms_bf16(x_ref[...], g_ref[...])

    def mm(name):
        off, width = _OFF[name]
        return jnp.dot(h, w_ref[:, off:off + width], preferred_element_type=F32)

    def rope(name, cos_ref, sin_ref, out_ref, scale):
        p = mm(name)
        w = p.shape[1] // 2
        c = cos_ref[...]
        s = sin_ref[...]
        for j in range(w // LANES):
            lo = j * LANES
            r = p[:, lo:lo + LANES] * c + p[:, w + lo:w + lo + LANES] * s
            out_ref[:, lo:lo + LANES] = (r if scale == 1.0 else r * scale).astype(out_ref.dtype)

    att_scale = HEAD_DIM ** -0.5
    rope("q", cos_a, sin_a, qa_o, att_scale)
    rope("k", cos_a, sin_a, ka_o, 1.0)
    rope("iq", cos_i, sin_i, iq_o, 1.0)
    rope("ik", cos_i, sin_i, ik_o, 1.0)
    qc_o[...] = (mm("qc") * att_scale).astype(BF16)
    kc_o[...] = mm("kc").astype(BF16)
    t = lax.dot_general(wt_ref[...], h, _CONTRACT_LAST, preferred_element_type=F32)
    vat_o[...] = t[T_VA:T_VA + BRANCH_WIDTH, :].astype(BF16)
    vct_o[...] = t[T_VC:T_VC + BRANCH_WIDTH, :].astype(BF16)
    iwt_o[...] = t[T_IW:T_IW + SUBLANES, :]

    u = jax.nn.gelu(mm("ub"))
    v = jax.nn.gelu(mm("vb"))
    mu = jnp.mean(v, axis=-1, keepdims=True)
    var = jnp.mean(jnp.square(v - mu), axis=-1, keepdims=True)
    vn = ((v - mu) * lax.rsqrt(var + EPS) * lng_ref[...] + lnb_ref[...]).astype(BF16)
    t_idx = lax.broadcasted_iota(jnp.int32, (CHUNK, CHUNK), 0)
    s_idx = lax.broadcasted_iota(jnp.int32, (CHUNK, CHUNK), 1)
    causal = s_idx <= t_idx
    group_masks = [_head_lane_mask(BRANCH_WIDTH, BRANCH_WIDTH // N_GROUPS_B, g) for g in range(N_GROUPS_B)]
    w_tril = [jnp.where(causal, ws_ref[g], 0.0).astype(BF16) for g in range(N_GROUPS_B)]
    for c in range(rows // CHUNK):
        vc = vn[c * CHUNK:(c + 1) * CHUNK, :]
        mixed = bs_ref[...]
        for g in range(N_GROUPS_B):
            full = jnp.dot(w_tril[g], vc, preferred_element_type=F32)
            mixed = mixed + jnp.where(group_masks[g], full, 0.0)
        yb_o[c * CHUNK:(c + 1) * CHUNK, :] = (u[c * CHUNK:(c + 1) * CHUNK, :] * mixed).astype(BF16)

    bd = mm("bd")
    z = mm("cd") * mm("xd")

    @pl.when(pl.program_id(0) % steps_per_seq == 0)
    def _():
        zbuf[0:CONV_HALO, :] = jnp.zeros((CONV_HALO, BRANCH_WIDTH), F32)

    zbuf[CONV_HALO:CONV_HALO + rows, :] = z
    y = cw_ref[CONV_WIDTH - 1:CONV_WIDTH, :] * z
    for tap in range(CONV_WIDTH - 1):
        shift = CONV_WIDTH - 1 - tap
        y = y + cw_ref[tap:tap + 1, :] * zbuf[CONV_HALO - shift:CONV_HALO - shift + rows, :]
    yd_o[...] = (bd * y).astype(BF16)
    zbuf[0:CONV_HALO, :] = z[rows - CONV_HALO:rows, :]


def _swap_halves(w, dim):
    d_in, cols = w.shape
    wh = w.reshape(d_in, cols // dim, 2, dim // 2)
    return jnp.stack([-wh[:, :, 1], wh[:, :, 0]], axis=2).reshape(d_in, cols)


def _proj_weight(w_in):
    d = w_in.shape[0]
    sizes = (BRANCH_WIDTH,) * 3 + (N_HEADS * IDX_DIM, IDX_DIM, N_HEADS) + (BRANCH_WIDTH,) * 8
    parts, o = [], 0
    for s in sizes:
        parts.append(w_in[:, o:o + s])
        o += s
    qa, ka, va, iq, ik, iw, ub, vb, qc, kc, vc, bd, cd, xd = parts
    ik_rep = jnp.tile(ik, (1, N_HEADS))
    cols = [qa, _swap_halves(qa, HEAD_DIM), ka, _swap_halves(ka, HEAD_DIM),
            iq, _swap_halves(iq, IDX_DIM), ik_rep, _swap_halves(ik_rep, IDX_DIM),
            ub, vb, qc, kc, bd, cd, xd]
    iw_pad = jnp.concatenate([iw, jnp.zeros((d, W_T_ROWS - 2 * BRANCH_WIDTH - N_HEADS), w_in.dtype)], axis=1)
    w_t = jnp.concatenate([va, vc, iw_pad], axis=1).T
    return jnp.concatenate(cols, axis=1).astype(BF16), w_t.astype(BF16), o


def _project(x2, norm_g, w_a, w_t, tables, ln_g, ln_b, w_spatial, b_spatial, conv_w, seq):
    n, d = x2.shape
    rows = PROJ_ROWS
    cos_a, sin_a, cos_i, sin_i = tables
    bias = jnp.repeat(b_spatial.T, BRANCH_WIDTH // N_GROUPS_B, axis=1)
    row = lambda w: pl.BlockSpec((rows, w), lambda i: (i, 0))
    col = lambda r: pl.BlockSpec((r, rows), lambda i: (0, i))
    full = lambda a: pl.BlockSpec(a.shape, lambda i: (0,) * a.ndim)
    bf = lambda w: jax.ShapeDtypeStruct((n, w), BF16)
    bft = jax.ShapeDtypeStruct((BRANCH_WIDTH, n), BF16)
    args = (x2, norm_g.reshape(1, d), w_a, w_t, cos_a, sin_a, cos_i, sin_i, ln_g.reshape(1, -1),
            ln_b.reshape(1, -1), w_spatial, bias, conv_w)
    in_specs = [row(d), full(args[1]), full(w_a), full(w_t), row(LANES), row(LANES), row(LANES),
                row(LANES), full(args[8]), full(args[9]), full(w_spatial), full(bias), full(conv_w)]
    out_shape = [bf(256), bf(256), bft, bf(128), bf(128), jax.ShapeDtypeStruct((SUBLANES, n), F32),
                 bf(256), bf(256), bf(256), bft, bf(256)]
    out_specs = [row(256), row(256), col(BRANCH_WIDTH), row(128), row(128), col(SUBLANES),
                 row(256), row(256), row(256), col(BRANCH_WIDTH), row(256)]
    return pl.pallas_call(
        functools.partial(_proj_kernel, steps_per_seq=seq // rows),
        grid=(n // rows,),
        in_specs=in_specs,
        out_specs=out_specs,
        out_shape=out_shape,
        scratch_shapes=[pltpu.VMEM((CONV_HALO + rows, BRANCH_WIDTH), F32)],
        compiler_params=_params("arbitrary"),
        name="project",
    )(*args)


def _head_lane_mask(width, per_head, h, rows=1):
    lane = lax.broadcasted_iota(jnp.int32, (rows, width), 1)
    return (lane >= h * per_head) & (lane < (h + 1) * per_head)


def _masked_heads(x, per_head):
    return [jnp.where(_head_lane_mask(x.shape[1], per_head, h), x, jnp.zeros_like(x))
            for h in range(N_HEADS)]


def _float_to_key(x):
    bits = lax.bitcast_convert_type(x, jnp.int32)
    sign = bits >> 31
    return (bits ^ (sign & jnp.int32(0x7FFFFFFF))) - sign


def _key_to_float(key):
    bits = jnp.where(key < 0, (-key) | jnp.int32(INT_MIN), key)
    return lax.bitcast_convert_type(bits, F32)


def _neg_abs(x):
    bits = lax.bitcast_convert_type(x, jnp.int32) | jnp.int32(INT_MIN)
    return lax.bitcast_convert_type(bits, F32)


def _col_reduce(x, op, final):
    while x.shape[0] > SUBLANES:
        half = x.shape[0] // 2
        x = op(x[:half], x[half:])
    return final(x, axis=0, keepdims=True)


def _col_sum(x):
    return _col_reduce(x, jnp.add, jnp.sum)


def _col_max(x):
    return _col_reduce(x, jnp.maximum, jnp.max)


def _step_starts(step):
    return [pl.multiple_of((step * TILES_PER_STEP + g) * K_TILE, K_TILE) for g in range(TILES_PER_STEP)]


def _dsa_kernel(iq_ref, iwt_ref, q_ref, ik_ref, k_ref, vt_ref, o_ref, sc_ref, hi_ref, lo_ref, kmax_ref, *,
                topk):
    qb = q_ref.shape[0]
    i = pl.program_id(1)
    n_tiles = i + 1
    n_steps = (n_tiles + TILES_PER_STEP - 1) // TILES_PER_STEP

    @pl.when(i == 0)
    def _():
        sc_ref[...] = jnp.full(sc_ref.shape, -jnp.inf, F32)
        hi_ref[...] = jnp.full(hi_ref.shape, -HALF_RANGE, jnp.int16)
        lo_ref[...] = jnp.full(lo_ref.shape, -HALF_RANGE, jnp.int16)

    iq_heads = _masked_heads(iq_ref[...], IDX_DIM)
    iwt = iwt_ref[...]
    iw_heads = [iwt[h:h + 1, :] for h in range(N_HEADS)]

    def put_scores(start, sc):
        sc_ref[pl.ds(start, K_TILE), :] = sc
        key = _float_to_key(sc)
        hi_ref[pl.ds(start, K_TILE), :] = (key >> 16).astype(jnp.int16)
        lo_ref[pl.ds(start, K_TILE), :] = ((key & 0xFFFF) - HALF_RANGE).astype(jnp.int16)

    key_off = lax.broadcasted_iota(jnp.int32, (K_TILE, qb), 0)
    qry_pos = i * qb + lax.broadcasted_iota(jnp.int32, (K_TILE, qb), 1)

    def score_step(jj, masked):
        starts = _step_starts(jj)
        dots = [[lax.dot_general(ik_ref[pl.ds(st, K_TILE), :], iq_heads[h], _CONTRACT_LAST,
                                 preferred_element_type=F32) for h in range(N_HEADS)] for st in starts]
        for g in range(TILES_PER_STEP):
            sc = None
            for h in range(N_HEADS):
                t = iw_heads[h] * jnp.maximum(dots[g][h], 0.0)
                sc = t if sc is None else sc + t
            if masked:
                sc = jnp.where(starts[g] + key_off <= qry_pos, sc, -jnp.inf)
            put_scores(starts[g], sc)

    def full_step(jj, carry):
        score_step(jj, False)
        return carry

    lax.fori_loop(0, i // TILES_PER_STEP, full_step, 0)
    score_step(i // TILES_PER_STEP, True)

    n_chunks = (n_tiles * K_TILE + COUNT_CHUNK - 1) // COUNT_CHUNK
    kf = jnp.float32(topk)

    def chunk_rows(c):
        base = pl.multiple_of(c * COUNT_CHUNK, COUNT_CHUNK)
        return [pl.ds(base + r * COUNT_ROWS, COUNT_ROWS) for r in range(COUNT_CHUNK // COUNT_ROWS)]

    def count(ref, pred, dtype):
        one, zero = jnp.ones((), dtype), jnp.zeros((), dtype)

        def body(c, acc):
            for rows in chunk_rows(c):
                acc = acc + jnp.where(pred(ref[rows, :]), one, zero)
            return acc
        acc = lax.fori_loop(0, n_chunks, body, jnp.zeros((COUNT_ROWS, qb), dtype))
        return jnp.sum(acc.astype(F32), axis=0, keepdims=True)

    def bisect16(ref, target):
        def bit_step(b, carry):
            best, cnt_best, cnt_above = carry
            cand = best + lax.shift_left(jnp.int32(1), 15 - b)
            cand16 = cand.astype(jnp.int16)
            cnt = count(ref, lambda blk: blk >= cand16, jnp.int16)
            ok = cnt >= target
            return jnp.where(ok, cand, best), jnp.where(ok, cnt, cnt_best), jnp.where(ok, cnt_above, cnt)
        init = (jnp.full((1, qb), -HALF_RANGE, jnp.int32), jnp.zeros((1, qb), F32), jnp.zeros((1, qb), F32))
        return lax.fori_loop(0, 16, bit_step, init)

    hi_t, cnt_hi, above = bisect16(hi_ref, kf)
    hi_t16 = hi_t.astype(jnp.int16)

    def mask_low(c, carry):
        for rows in chunk_rows(c):
            lo_ref[rows, :] = jnp.where(hi_ref[rows, :] == hi_t16, lo_ref[rows, :], jnp.int16(-HALF_RANGE))
        return carry

    lax.fori_loop(0, n_chunks, mask_low, 0)
    lo_t, cnt_lo, _ = bisect16(lo_ref, kf - above)
    cnt_key = above + jnp.where(lo_t == -HALF_RANGE, cnt_hi - above, cnt_lo)
    key = hi_t * (2 * HALF_RANGE) + (lo_t + HALF_RANGE)
    few = i * qb + lax.broadcasted_iota(jnp.int32, (1, qb), 1) + 1 < topk
    thr = jnp.where(few, jnp.finfo(F32).min, _key_to_float(key))

    @pl.when(jnp.max(jnp.where(few, kf, cnt_key)) > kf)
    def _():
        need = kf - count(sc_ref, lambda blk: blk > thr, F32)
        before = (lax.broadcasted_iota(jnp.int32, (K_TILE, K_TILE), 1)
                  < lax.broadcasted_iota(jnp.int32, (K_TILE, K_TILE), 0))
        before = jnp.where(before, 1.0, 0.0).astype(BF16)

        def tie_body(jj, seen):
            starts = _step_starts(jj)
            blks = [sc_ref[pl.ds(st, K_TILE), :] for st in starts]
            eqs = [blk == thr for blk in blks]
            eq_fs = [jnp.where(eq, 1.0, 0.0) for eq in eqs]
            ranks = [jnp.dot(before, eq_f.astype(BF16), preferred_element_type=F32) for eq_f in eq_fs]
            for g in range(TILES_PER_STEP):
                drop = eqs[g] & (seen + ranks[g] >= need)
                sc_ref[pl.ds(starts[g], K_TILE), :] = jnp.where(drop, -jnp.inf, blks[g])
                seen = seen + _col_sum(eq_fs[g])
            return seen

        lax.fori_loop(0, n_steps, tie_body, jnp.zeros((1, qb), F32))

    q = q_ref[...]
    q_heads = _masked_heads(q, HEAD_DIM)
    head_rows = jnp.concatenate(
        [jnp.where(_head_lane_mask(BRANCH_WIDTH, HEAD_DIM, h), 1.0, 0.0) for h in range(N_HEADS)]
        + [jnp.zeros((ONES_ROWS - N_HEADS, BRANCH_WIDTH), F32)], axis=0).astype(BF16)
    qf = q.astype(F32)
    q_sq = lax.dot_general(head_rows, (qf * qf).astype(BF16), _CONTRACT_LAST,
                           preferred_element_type=F32)
    kf32 = k_ref[pl.ds(pl.multiple_of(i * K_TILE, K_TILE), K_TILE), :].astype(F32)
    k_sq = lax.dot_general(head_rows, (kf32 * kf32).astype(BF16), _CONTRACT_LAST,
                           preferred_element_type=F32)
    k_sq_max = jnp.max(k_sq, axis=1, keepdims=True)

    @pl.when(i == 0)
    def _():
        kmax_ref[...] = jnp.zeros(kmax_ref.shape, F32)

    kmax_ref[...] = jnp.maximum(kmax_ref[...], jnp.broadcast_to(k_sq_max, kmax_ref.shape))
    bound = jnp.sqrt(q_sq * kmax_ref[:, 0:1]) * BOUND_MARGIN
    bounds = [bound[h:h + 1, :] for h in range(N_HEADS)]
    ones_rows = jnp.ones((ONES_ROWS, K_TILE), BF16)

    def step_logits(jj):
        starts = _step_starts(jj)
        sels = [sc_ref[pl.ds(st, K_TILE), :] >= thr for st in starts]
        logits = [[lax.dot_general(k_ref[pl.ds(st, K_TILE), :], q_heads[h], _CONTRACT_LAST,
                                   preferred_element_type=F32) for h in range(N_HEADS)] for st in starts]
        masked = [[jnp.where(sels[g], logits[g][h], NEG_BIG) for g in range(TILES_PER_STEP)]
                  for h in range(N_HEADS)]
        return starts, masked

    def value_products(starts, probs, h):
        pv = None
        for g, st in enumerate(starts):
            vt = vt_ref[h * HEAD_DIM:(h + 1) * HEAD_DIM, pl.ds(st, K_TILE)]
            t = jnp.dot(vt, probs[g], preferred_element_type=F32)
            pv = t if pv is None else pv + t
        return pv

    def bounded_step(jj, carry):
        l, acc = carry
        starts, masked = step_logits(jj)
        probs = [[jnp.exp(s - bounds[h]).astype(BF16) for s in masked[h]] for h in range(N_HEADS)]
        l_new, acc_new = [], []
        for h in range(N_HEADS):
            lh = l[h]
            for pb in probs[h]:
                lh = lh + jnp.dot(ones_rows, pb, preferred_element_type=F32)[0:1]
            l_new.append(lh)
            acc_new.append(acc[h] + value_products(starts, probs[h], h))
        return tuple(l_new), tuple(acc_new)

    def online_step(jj, carry):
        m, l, acc = carry
        starts, masked = step_logits(jj)
        m_new, l_new, acc_new = [], [], []
        for h in range(N_HEADS):
            mh = m[h]
            for s in masked[h]:
                mh = jnp.maximum(mh, _col_max(s))
            ps = [jnp.exp(s - mh) for s in masked[h]]
            alpha = jnp.exp(m[h] - mh)
            lh = alpha * l[h]
            for p in ps:
                lh = lh + _col_sum(p)
            m_new.append(mh)
            l_new.append(lh)
            acc_new.append(alpha * acc[h] + value_products(starts, [p.astype(BF16) for p in ps], h))
        return tuple(m_new), tuple(l_new), tuple(acc_new)

    def finish(l, acc):
        out_t = jnp.concatenate([acc[h] / l[h] for h in range(N_HEADS)], axis=0)
        o_ref[...] = out_t.T.astype(o_ref.dtype)

    zeros_l = tuple(jnp.zeros((1, qb), F32) for _ in range(N_HEADS))
    zeros_acc = tuple(jnp.zeros((HEAD_DIM, qb), F32) for _ in range(N_HEADS))
    small = jnp.max(bound[0:N_HEADS, :]) <= BOUND_LIMIT

    @pl.when(small)
    def _():
        finish(*lax.fori_loop(0, n_steps, bounded_step, (zeros_l, zeros_acc)))

    @pl.when(jnp.logical_not(small))
    def _():
        init = (tuple(jnp.full((1, qb), NEG_BIG, F32) for _ in range(N_HEADS)), zeros_l, zeros_acc)
        _, l, acc = lax.fori_loop(0, n_steps, online_step, init)
        finish(l, acc)


def _dsa(iq, iwt, qa, ik, ka, vat, batch, seq):
    n = qa.shape[0]
    nq = seq // Q_BLOCK
    topk = min(TOPK_MAX, seq // 4)
    qrow = lambda w: pl.BlockSpec((Q_BLOCK, w), lambda b, i: (b * nq + i, 0))
    whole = lambda w: pl.BlockSpec((seq, w), lambda b, i: (b, 0))
    return pl.pallas_call(
        functools.partial(_dsa_kernel, topk=topk),
        grid=(batch, nq),
        in_specs=[qrow(LANES), pl.BlockSpec((SUBLANES, Q_BLOCK), lambda b, i: (0, b * nq + i)),
                  qrow(BRANCH_WIDTH), whole(LANES), whole(BRANCH_WIDTH),
                  pl.BlockSpec((BRANCH_WIDTH, seq), lambda b, i: (0, b))],
        out_specs=qrow(BRANCH_WIDTH),
        out_shape=jax.ShapeDtypeStruct((n, BRANCH_WIDTH), BF16),
        scratch_shapes=[pltpu.VMEM((seq, Q_BLOCK), F32), pltpu.VMEM((seq, Q_BLOCK), jnp.int16),
                        pltpu.VMEM((seq, Q_BLOCK), jnp.int16), pltpu.VMEM((ONES_ROWS, LANES), F32)],
        compiler_params=_params("arbitrary", "arbitrary"),
        name="dsa",
    )(iq, iwt, qa, ik, ka, vat)


def _stick_kernel(q_ref, k_ref, vt_ref, o_ref):
    qb = q_ref.shape[0]
    i = pl.program_id(1)
    q_heads = _masked_heads(q_ref[...], HEAD_DIM)
    after = (lax.broadcasted_iota(jnp.int32, (K_TILE, K_TILE), 1)
             > lax.broadcasted_iota(jnp.int32, (K_TILE, K_TILE), 0))
    after = jnp.where(after, 1.0, 0.0).astype(BF16)
    after = jnp.concatenate([after, jnp.ones((ONES_ROWS, K_TILE), BF16)], axis=0)
    key_off = lax.broadcasted_iota(jnp.int32, (K_TILE, qb), 0)
    qry_pos = i * qb + lax.broadcasted_iota(jnp.int32, (K_TILE, qb), 1)

    def step(u, carry, masked):
        tails, acc = carry
        first = i - 2 * u
        tiles = [first, jnp.maximum(first - 1, 0)]
        bias = [None, jnp.where(first >= 1, 0.0, PHANTOM_LOGIT)]
        starts = [pl.multiple_of(t * K_TILE, K_TILE) for t in tiles]
        stricts = [starts[g] + key_off < qry_pos if masked else None for g in range(2)]
        zs = [[lax.dot_general(k_ref[pl.ds(st, K_TILE), :], q_heads[h], _CONTRACT_LAST,
                               preferred_element_type=F32) for h in range(N_HEADS)] for st in starts]
        log_betas, splits = [], []
        for g in range(2):
            lbs, sps = [], []
            for h in range(N_HEADS):
                z = zs[g][h] if bias[g] is None else zs[g][h] + bias[g]
                log_beta = jnp.minimum(z, 0.0) - jnp.log(1.0 + jnp.exp(_neg_abs(z)))
                log_keep = log_beta - z
                if masked:
                    log_keep = jnp.where(stricts[g], log_keep, 0.0)
                lbs.append(log_beta)
                sps.append(log_keep.astype(BF16))
            log_betas.append(lbs)
            splits.append(sps)
        sums = [[jnp.dot(after, splits[g][h], preferred_element_type=F32) for h in range(N_HEADS)]
                for g in range(2)]
        tails = list(tails)
        new_acc = []
        for h in range(N_HEADS):
            total = acc[h]
            for g in range(2):
                a = jnp.exp(log_betas[g][h] + sums[g][h][:K_TILE] + tails[h])
                if masked:
                    a = jnp.where(stricts[g], a, 0.0)
                tails[h] = tails[h] + sums[g][h][K_TILE:K_TILE + 1]
                vt = vt_ref[h * HEAD_DIM:(h + 1) * HEAD_DIM, pl.ds(starts[g], K_TILE)]
                total = total + jnp.dot(vt, a.astype(BF16), preferred_element_type=F32)
            new_acc.append(total)
        return tuple(tails), tuple(new_acc)

    init = (tuple(jnp.zeros((1, qb), F32) for _ in range(N_HEADS)),
            tuple(jnp.zeros((HEAD_DIM, qb), F32) for _ in range(N_HEADS)))
    carry = step(0, init, True)

    def more(state):
        u, (tails, _) = state
        live = tails[0]
        for h in range(1, N_HEADS):
            live = jnp.maximum(live, tails[h])
        return (u < (i + 2) // 2) & (jnp.max(live) > EXP_ZERO_BELOW)

    _, (_, acc) = lax.while_loop(more, lambda st: (st[0] + 1, step(st[0], st[1], False)), (jnp.int32(1), carry))
    o_ref[...] = jnp.concatenate(acc, axis=0).T.astype(o_ref.dtype)


def _stick_breaking(qc, kc, vct, batch, seq):
    n = qc.shape[0]
    nq = seq // Q_BLOCK
    qrow = pl.BlockSpec((Q_BLOCK, BRANCH_WIDTH), lambda b, i: (b * nq + i, 0))
    return pl.pallas_call(
        _stick_kernel,
        grid=(batch, nq),
        in_specs=[qrow, pl.BlockSpec((seq, BRANCH_WIDTH), lambda b, i: (b, 0)),
                  pl.BlockSpec((BRANCH_WIDTH, seq), lambda b, i: (0, b))],
        out_specs=qrow,
        out_shape=jax.ShapeDtypeStruct((n, BRANCH_WIDTH), BF16),
        compiler_params=_params("arbitrary", "arbitrary"),
        name="stick_breaking",
    )(qc, kc, vct)


def _merge_kernel(x_ref, g1_ref, ya_ref, yb_ref, yc_ref, yd_ref, wg_ref, wb_ref, wo_ref,
                  g2_ref, wr_ref, br_ref, x1_o, comb_o):
    x = x_ref[...]
    d = x.shape[1]
    h = _rms_bf16(x, g1_ref[...])
    ys = (ya_ref[...], yb_ref[...], yc_ref[...], yd_ref[...])
    col = 256
    pieces = []
    for c in range(d // col):
        m = None
        for n in range(N_BRANCH):
            logit = jnp.dot(h, wg_ref[:, n * d + c * col:n * d + (c + 1) * col],
                            preferred_element_type=F32)
            br = jnp.dot(ys[n], wb_ref[n, :, c * col:(c + 1) * col], preferred_element_type=F32)
            t = jax.nn.sigmoid(logit) * br
            m = t if m is None else m + t
        pieces.append(m.astype(BF16))
    merged = jnp.concatenate(pieces, axis=1)
    x1 = x + jnp.dot(merged, wo_ref[...], preferred_element_type=F32)
    x1_o[...] = x1

    ms = jnp.mean(x1 * x1, axis=-1, keepdims=True)
    h2 = x1 * lax.rsqrt(ms + EPS) * g2_ref[...]
    h_hi = h2.astype(BF16)
    h_lo = (h2 - h_hi.astype(F32)).astype(BF16)
    both = jnp.dot(h_hi, wr_ref[...], preferred_element_type=F32)
    cross = jnp.dot(h_lo, wr_ref[:, 0:LANES], preferred_element_type=F32)
    logits = both[:, 0:LANES] + both[:, LANES:2 * LANES] + cross + br_ref[...]
    rows = logits.shape[0]
    lane = lax.broadcasted_iota(jnp.int32, (rows, LANES), 1)
    neg = -jnp.inf
    is_grp = (lane >= N_EXPERTS) & (lane < N_EXPERTS + N_EXPERT_GROUPS)
    gl = jnp.where(is_grp, logits, neg)
    gmax = jnp.max(gl, axis=1, keepdims=True)
    gidx = jnp.min(jnp.where(gl == gmax, lane, LANES), axis=1, keepdims=True) - N_EXPERTS
    grp_p = 1.0 / jnp.sum(jnp.where(is_grp, jnp.exp(gl - gmax), 0.0), axis=1, keepdims=True)
    in_grp = (lane >= gidx * EXPERTS_PER_GROUP) & (lane < (gidx + 1) * EXPERTS_PER_GROUP)
    el = jnp.where(in_grp, logits, neg)
    l1 = jnp.max(el, axis=1, keepdims=True)
    j1 = jnp.min(jnp.where(el == l1, lane, LANES), axis=1, keepdims=True)
    el2 = jnp.where(lane == j1, neg, el)
    l2 = jnp.max(el2, axis=1, keepdims=True)
    j2 = jnp.min(jnp.where(el2 == l2, lane, LANES), axis=1, keepdims=True)
    e2 = jnp.exp(l2 - l1)
    w1 = grp_p / (1.0 + e2)
    w2 = grp_p * e2 / (1.0 + e2)
    comb_o[...] = jnp.where(lane == j1, w1, 0.0) + jnp.where(lane == j2, w2, 0.0)


def _merge(x2, norm1_g, ys, w_gate, w_branch, w_out, norm2_g, w_router, b_router):
    n, d = x2.shape
    rows = MERGE_ROWS
    row = lambda w: pl.BlockSpec((rows, w), lambda i: (i, 0))
    full = lambda a: pl.BlockSpec(a.shape, lambda i: (0,) * a.ndim)
    args = (x2, norm1_g.reshape(1, d), *ys, w_gate, w_branch, w_out, norm2_g.reshape(1, d),
            w_router, b_router)
    in_specs = [row(d), full(args[1])] + [row(BRANCH_WIDTH)] * 4 + [full(a) for a in args[6:]]
    return pl.pallas_call(
        _merge_kernel,
        grid=(n // rows,),
        in_specs=in_specs,
        out_specs=[row(d), row(LANES)],
        out_shape=[jax.ShapeDtypeStruct((n, d), F32), jax.ShapeDtypeStruct((n, LANES), F32)],
        compiler_params=_params("arbitrary"),
        name="merge_router",
    )(*args)


def _moe_kernel(x_ref, g2_ref, comb_ref, wgu_ref, wd_ref, gf_ref, o_ref,
                h_scr, acc_scr, memb_scr, rank_scr, mcol_scr, rcol_scr, hg_scr, cw_scr, y_scr, flag_scr,
                *, final_norm):
    step = pl.program_id(1)
    rows = x_ref.shape[0]
    steps_per_group = EXPERTS_PER_GROUP // EXPERTS_PER_STEP
    group = step // steps_per_group
    first_of_group = step % steps_per_group == 0
    last_of_group = step % steps_per_group == steps_per_group - 1
    lane = lax.broadcasted_iota(jnp.int32, (rows, LANES), 1)

    @pl.when(step == 0)
    def _():
        h_scr[...] = _rms_bf16(x_ref[...], g2_ref[...])
        acc_scr[...] = jnp.zeros(acc_scr.shape, F32)
        comb = comb_ref[...]
        member_cols = [jnp.sum(jnp.where((lane >= g * EXPERTS_PER_GROUP) & (lane < (g + 1) * EXPERTS_PER_GROUP),
                                         comb, 0.0), axis=1, keepdims=True) > 0.0 for g in range(N_EXPERT_GROUPS)]
        mcol = jnp.zeros((rows, LANES), F32)
        for g in range(N_EXPERT_GROUPS):
            mcol = jnp.where((lane == g) & member_cols[g], 1.0, mcol)
        mcol_scr[...] = mcol
        t_row = lax.broadcasted_iota(jnp.int32, (rows, rows), 0)
        t_col = lax.broadcasted_iota(jnp.int32, (rows, rows), 1)
        earlier = jnp.where(t_col < t_row, 1.0, 0.0).astype(BF16)
        rcol_scr[...] = jnp.dot(earlier, mcol.astype(BF16), preferred_element_type=F32)
        memb = mcol.T[0:ONES_ROWS, :]
        memb_scr[...] = memb
        later = jnp.where(t_row < t_col, 1.0, 0.0).astype(BF16)
        rank_scr[...] = jnp.dot(memb.astype(BF16), later, preferred_element_type=F32)
        counts = jnp.sum(memb, axis=1, keepdims=True)
        for g in range(N_EXPERT_GROUPS):
            flag_scr[g] = (jnp.max(counts[g:g + 1, :]) > GROUP_SLOTS).astype(jnp.int32)

    crowded = flag_scr[group] == 1
    comb = comb_ref[...]

    def expert_rows(hb, cw, j):
        e = step * EXPERTS_PER_STEP + j
        lane_c = lax.broadcasted_iota(jnp.int32, cw.shape, 1)
        w = jnp.sum(jnp.where(lane_c == e, cw, 0.0), axis=1, keepdims=True)
        gu = jnp.dot(hb, wgu_ref[j], preferred_element_type=F32)
        hh = (jax.nn.silu(gu[:, :D_EXPERT]) * gu[:, D_EXPERT:] * w).astype(BF16)
        return jnp.dot(hh, wd_ref[j], preferred_element_type=F32)

    @pl.when(crowded)
    def _():
        total = None
        for j in range(EXPERTS_PER_STEP):
            t = expert_rows(h_scr[...], comb, j)
            total = t if total is None else total + t
        acc_scr[...] += total

    @pl.when(jnp.logical_not(crowded))
    def _():
        slot_rows = lax.broadcasted_iota(jnp.int32, (GROUP_SLOTS, rows), 0).astype(F32)

        @pl.when(first_of_group)
        def _():
            in_group = memb_scr[pl.ds(group, 1), :] > 0.0
            rank = rank_scr[pl.ds(group, 1), :]
            pick = jnp.where(in_group & (slot_rows == rank), 1.0, 0.0).astype(BF16)
            hg_scr[...] = jnp.dot(pick, h_scr[...], preferred_element_type=F32).astype(BF16)
            comb_hi = comb.astype(BF16)
            comb_lo = (comb - comb_hi.astype(F32)).astype(BF16)
            cw_scr[...] = (jnp.dot(pick, comb_hi, preferred_element_type=F32)
                           + jnp.dot(pick, comb_lo, preferred_element_type=F32))
            y_scr[...] = jnp.zeros(y_scr.shape, F32)

        total = None
        for j in range(EXPERTS_PER_STEP):
            t = expert_rows(hg_scr[...], cw_scr[...], j)
            total = t if total is None else total + t
        y_scr[...] += total

        @pl.when(last_of_group)
        def _():
            in_group = jnp.sum(jnp.where(lane == group, mcol_scr[...], 0.0), axis=1, keepdims=True) > 0.0
            rank = jnp.sum(jnp.where(lane == group, rcol_scr[...], 0.0), axis=1, keepdims=True)
            slot_cols = lax.broadcasted_iota(jnp.int32, (rows, GROUP_SLOTS), 1).astype(F32)
            place = jnp.where(in_group & (slot_cols == rank), 1.0, 0.0).astype(BF16)
            acc_scr[...] += jnp.dot(place, y_scr[...].astype(BF16), preferred_element_type=F32)

    @pl.when(step == pl.num_programs(1) - 1)
    def _():
        y = x_ref[...] + acc_scr[...]
        if final_norm:
            ms = jnp.mean(y * y, axis=-1, keepdims=True)
            y = y * lax.rsqrt(ms + EPS) * gf_ref[...]
        o_ref[...] = y


def _moe(x1, norm2_g, comb, w_gu, w_d, norm_f_g, final_norm):
    n, d = x1.shape
    rows = min(MOE_ROWS, n)
    n_exp = w_gu.shape[0]
    row = lambda w: pl.BlockSpec((rows, w), lambda i, e: (i, 0))
    vec = pl.BlockSpec((1, d), lambda i, e: (0, 0))
    return pl.pallas_call(
        functools.partial(_moe_kernel, final_norm=final_norm),
        grid=(n // rows, n_exp // EXPERTS_PER_STEP),
        in_specs=[row(d), vec, row(LANES),
                  pl.BlockSpec((EXPERTS_PER_STEP, d, 2 * D_EXPERT), lambda i, e: (e, 0, 0)),
                  pl.BlockSpec((EXPERTS_PER_STEP, D_EXPERT, d), lambda i, e: (e, 0, 0)),
                  vec],
        out_specs=row(d),
        out_shape=jax.ShapeDtypeStruct((n, d), F32),
        scratch_shapes=[pltpu.VMEM((rows, d), BF16), pltpu.VMEM((rows, d), F32),
                        pltpu.VMEM((ONES_ROWS, rows), F32), pltpu.VMEM((ONES_ROWS, rows), F32),
                        pltpu.VMEM((rows, LANES), F32), pltpu.VMEM((rows, LANES), F32),
                        pltpu.VMEM((GROUP_SLOTS, d), BF16), pltpu.VMEM((GROUP_SLOTS, LANES), F32),
                        pltpu.VMEM((GROUP_SLOTS, d), F32), pltpu.SMEM((N_EXPERT_GROUPS,), jnp.int32)],
        compiler_params=_params("arbitrary", "arbitrary"),
        name="experts",
    )(x1, norm2_g.reshape(1, d), comb, w_gu, w_d, norm_f_g.reshape(1, d))


def kernel(x, positions, norm1_g, w_in, sgu_ln_g, sgu_ln_b, w_spatial, b_spatial, conv_w, w_branch,
           w_out, norm2_g, w_router_group, b_router_group, w_router_expert, b_router_expert,
           w_exp_gate, w_exp_up, w_exp_down, norm_f_g):
    batch, seq, d = x.shape
    depth = w_in.shape[0]
    n = batch * seq
    assert TILES_PER_STEP == 2
    assert seq % COUNT_CHUNK == 0 and seq % (TILES_PER_STEP * K_TILE) == 0
    assert n % PROJ_ROWS == 0 and n % MERGE_ROWS == 0
    x2 = x.reshape(n, d)
    tables = _rope_tables(positions)
    for l in range(depth):
        w_a, w_t, gate_off = _proj_weight(w_in[l])
        w_gate = w_in[l][:, gate_off:].astype(BF16)
        pad = jnp.zeros((d, LANES - N_EXPERTS - N_EXPERT_GROUPS), F32)
        w_router = jnp.concatenate([w_router_expert[l], w_router_group[l], pad], axis=1)
        w_router_hi = w_router.astype(BF16)
        w_router_lo = (w_router - w_router_hi.astype(F32)).astype(BF16)
        w_router = jnp.concatenate([w_router_hi, w_router_lo], axis=1)
        b_router = jnp.concatenate([b_router_expert[l], b_router_group[l], pad[0]]).reshape(1, LANES)
        qa, ka, vat, iq, ik, iwt, y_b, qc, kc, vct, y_d = _project(
            x2, norm1_g[l], w_a, w_t, tables, sgu_ln_g[l], sgu_ln_b[l], w_spatial[l], b_spatial[l],
            conv_w[l], seq)
        y_a = _dsa(iq, iwt, qa, ik, ka, vat, batch, seq)
        y_c = _stick_breaking(qc, kc, vct, batch, seq)
        x1, comb = _merge(x2, norm1_g[l], (y_a, y_b, y_c, y_d), w_gate, w_branch[l].astype(BF16),
                          w_out[l].astype(BF16), norm2_g[l], w_router, b_router)
        w_gu = jnp.concatenate([w_exp_gate[l], w_exp_up[l]], axis=2).astype(BF16)
        x2 = _moe(x1, norm2_g[l], comb, w_gu, w_exp_down[l].astype(BF16), norm_f_g,
                  final_norm=(l == depth - 1))
    return x2.reshape(batch, seq, d)
```

```python
import functools

import jax
import jax.numpy as jnp
from jax import lax
from jax.experimental import pallas as pl
from jax.experimental.pallas import tpu as pltpu

F32 = jnp.float32
BF16 = jnp.bfloat16

EPS = 1e-6
ROPE_THETA = 10000.0
HEAD_DIM = 64
BRANCH_WIDTH = 256
N_HEADS = 4
IDX_DIM = 32
TOPK_MAX = 256
CHUNK = 128
N_GROUPS_B = 4
CONV_WIDTH = 3
N_BRANCH = 4
N_EXPERT_GROUPS = 4
EXPERTS_PER_GROUP = 8
N_EXPERTS = N_EXPERT_GROUPS * EXPERTS_PER_GROUP
D_EXPERT = 256

LANES = 128
SUBLANES = 8
Q_BLOCK = 256
K_TILE = Q_BLOCK
TILES_PER_STEP = 2
COUNT_CHUNK = 512
COUNT_ROWS = 64
PROJ_ROWS = 256
MERGE_ROWS = 512
MOE_ROWS = 1024
EXPERTS_PER_STEP = 4
GROUP_SLOTS = 320
CONV_HALO = 8
VMEM_LIMIT = 56 * 1024 * 1024
PHANTOM_LOGIT = -1e4
NEG_BIG = -1e30
EXP_ZERO_BELOW = -110.0
INT_MIN = -2 ** 31
HALF_RANGE = 2 ** 15
ONES_ROWS = 16
BOUND_MARGIN = 1.02
BOUND_LIMIT = 40.0

_OFF = {}
_o = 0
for _name, _w in (("q", 512), ("k", 512), ("iq", 256), ("ik", 256),
                  ("ub", 256), ("vb", 256), ("qc", 256), ("kc", 256),
                  ("bd", 256), ("cd", 256), ("xd", 256)):
    _OFF[_name] = (_o, _w)
    _o += _w
W_A_COLS = _o
T_VA, T_VC, T_IW = 0, BRANCH_WIDTH, 2 * BRANCH_WIDTH
W_T_ROWS = 2 * BRANCH_WIDTH + 16

_CONTRACT_LAST = (((1,), (1,)), ((), ()))


def _params(*sem):
    return pltpu.CompilerParams(dimension_semantics=sem, vmem_limit_bytes=VMEM_LIMIT)


def _rope_table_kernel(pos_ref, inv_a_ref, inv_i_ref, cos_a, sin_a, cos_i, sin_i):
    p = pos_ref[...]
    a = p * inv_a_ref[...]
    cos_a[...] = jnp.cos(a)
    sin_a[...] = jnp.sin(a)
    b = p * inv_i_ref[...]
    cos_i[...] = jnp.cos(b)
    sin_i[...] = jnp.sin(b)


def _rope_tables(positions):
    n = positions.size
    rows = PROJ_ROWS
    pos = positions.astype(F32).reshape(n, 1)
    inv_a = ROPE_THETA ** (-jnp.arange(0, HEAD_DIM, 2, dtype=F32) / HEAD_DIM)
    inv_i = ROPE_THETA ** (-jnp.arange(0, IDX_DIM, 2, dtype=F32) / IDX_DIM)
    inv_a = jnp.tile(inv_a, LANES // inv_a.size).reshape(1, LANES)
    inv_i = jnp.tile(inv_i, LANES // inv_i.size).reshape(1, LANES)
    tab = jax.ShapeDtypeStruct((n, LANES), F32)
    row_spec = pl.BlockSpec((rows, LANES), lambda i: (i, 0))
    const = pl.BlockSpec((1, LANES), lambda i: (0, 0))
    return pl.pallas_call(
        _rope_table_kernel,
        grid=(n // rows,),
        in_specs=[pl.BlockSpec((rows, 1), lambda i: (i, 0)), const, const],
        out_specs=[row_spec] * 4,
        out_shape=[tab] * 4,
        compiler_params=_params("arbitrary"),
        name="rope_tables",
    )(pos, inv_a, inv_i)


def _rms_bf16(x, g):
    ms = jnp.mean(x * x, axis=-1, keepdims=True)
    return (x * lax.rsqrt(ms + EPS) * g).astype(BF16)


def _proj_kernel(x_ref, g_ref, w_ref, wt_ref, cos_a, sin_a, cos_i, sin_i, lng_ref, lnb_ref, ws_ref,
                 bs_ref, cw_ref,
                 qa_o, ka_o, vat_o, iq_o, ik_o, iwt_o, yb_o, qc_o, kc_o, vct_o, yd_o,
                 zbuf, *, steps_per_seq):
    rows = x_ref.shape[0]
    h = _rms_bf16(x_ref[...], g_ref[...])

    def mm(name):
        off, width = _OFF[name]
        return jnp.dot(h, w_ref[:, off:off + width], preferred_element_type=F32)

    def rope(name, cos_ref, sin_ref, out_ref, scale):
        p = mm(name)
        w = p.shape[1] // 2
        c = cos_ref[...]
        s = sin_ref[...]
        for j in range(w // LANES):
            lo = j * LANES
            r = p[:, lo:lo + LANES] * c + p[:, w + lo:w + lo + LANES] * s
            out_ref[:, lo:lo + LANES] = (r if scale == 1.0 else r * scale).astype(out_ref.dtype)

    att_scale = HEAD_DIM ** -0.5
    rope("q", cos_a, sin_a, qa_o, att_scale)
    rope("k", cos_a, sin_a, ka_o, 1.0)
    rope("iq", cos_i, sin_i, iq_o, 1.0)
    rope("ik", cos_i, sin_i, ik_o, 1.0)
    qc_o[...] = (mm("qc") * att_scale).astype(BF16)
    kc_o[...] = mm("kc").astype(BF16)
    t = lax.dot_general(wt_ref[...], h, _CONTRACT_LAST, preferred_element_type=F32)
    vat_o[...] = t[T_VA:T_VA + BRANCH_WIDTH, :].astype(BF16)
    vct_o[...] = t[T_VC:T_VC + BRANCH_WIDTH, :].astype(BF16)
    iwt_o[...] = t[T_IW:T_IW + SUBLANES, :]

    u = jax.nn.gelu(mm("ub"))
    v = jax.nn.gelu(mm("vb"))
    mu = jnp.mean(v, axis=-1, keepdims=True)
    var = jnp.mean(jnp.square(v - mu), axis=-1, keepdims=True)
    vn = ((v - mu) * lax.rsqrt(var + EPS) * lng_ref[...] + lnb_ref[...]).astype(BF16)
    t_idx = lax.broadcasted_iota(jnp.int32, (CHUNK, CHUNK), 0)
    s_idx = lax.broadcasted_iota(jnp.int32, (CHUNK, CHUNK), 1)
    causal = s_idx <= t_idx
    group_masks = [_head_lane_mask(BRANCH_WIDTH, BRANCH_WIDTH // N_GROUPS_B, g) for g in range(N_GROUPS_B)]
    w_tril = [jnp.where(causal, ws_ref[g], 0.0).astype(BF16) for g in range(N_GROUPS_B)]
    for c in range(rows // CHUNK):
        vc = vn[c * CHUNK:(c + 1) * CHUNK, :]
        mixed = bs_ref[...]
        for g in range(N_GROUPS_B):
            full = jnp.dot(w_tril[g], vc, preferred_element_type=F32)
            mixed = mixed + jnp.where(group_masks[g], full, 0.0)
        yb_o[c * CHUNK:(c + 1) * CHUNK, :] = (u[c * CHUNK:(c + 1) * CHUNK, :] * mixed).astype(BF16)

    bd = mm("bd")
    z = mm("cd") * mm("xd")

    @pl.when(pl.program_id(0) % steps_per_seq == 0)
    def _():
        zbuf[0:CONV_HALO, :] = jnp.zeros((CONV_HALO, BRANCH_WIDTH), F32)

    zbuf[CONV_HALO:CONV_HALO + rows, :] = z
    y = cw_ref[CONV_WIDTH - 1:CONV_WIDTH, :] * z
    for tap in range(CONV_WIDTH - 1):
        shift = CONV_WIDTH - 1 - tap
        y = y + cw_ref[tap:tap + 1, :] * zbuf[CONV_HALO - shift:CONV_HALO - shift + rows, :]
    yd_o[...] = (bd * y).astype(BF16)
    zbuf[0:CONV_HALO, :] = z[rows - CONV_HALO:rows, :]


def _swap_halves(w, dim):
    d_in, cols = w.shape
    wh = w.reshape(d_in, cols // dim, 2, dim // 2)
    return jnp.stack([-wh[:, :, 1], wh[:, :, 0]], axis=2).reshape(d_in, cols)


def _proj_weight(w_in):
    d = w_in.shape[0]
    sizes = (BRANCH_WIDTH,) * 3 + (N_HEADS * IDX_DIM, IDX_DIM, N_HEADS) + (BRANCH_WIDTH,) * 8
    parts, o = [], 0
    for s in sizes:
        parts.append(w_in[:, o:o + s])
        o += s
    qa, ka, va, iq, ik, iw, ub, vb, qc, kc, vc, bd, cd, xd = parts
    ik_rep = jnp.tile(ik, (1, N_HEADS))
    cols = [qa, _swap_halves(qa, HEAD_DIM), ka, _swap_halves(ka, HEAD_DIM),
            iq, _swap_halves(iq, IDX_DIM), ik_rep, _swap_halves(ik_rep, IDX_DIM),
            ub, vb, qc, kc, bd, cd, xd]
    iw_pad = jnp.concatenate([iw, jnp.zeros((d, W_T_ROWS - 2 * BRANCH_WIDTH - N_HEADS), w_in.dtype)], axis=1)
    w_t = jnp.concatenate([va, vc, iw_pad], axis=1).T
    return jnp.concatenate(cols, axis=1).astype(BF16), w_t.astype(BF16), o


def _project(x2, norm_g, w_a, w_t, tables, ln_g, ln_b, w_spatial, b_spatial, conv_w, seq):
    n, d = x2.shape
    rows = PROJ_ROWS
    cos_a, sin_a, cos_i, sin_i = tables
    bias = jnp.repeat(b_spatial.T, BRANCH_WIDTH // N_GROUPS_B, axis=1)
    row = lambda w: pl.BlockSpec((rows, w), lambda i: (i, 0))
    col = lambda r: pl.BlockSpec((r, rows), lambda i: (0, i))
    full = lambda a: pl.BlockSpec(a.shape, lambda i: (0,) * a.ndim)
    bf = lambda w: jax.ShapeDtypeStruct((n, w), BF16)
    bft = jax.ShapeDtypeStruct((BRANCH_WIDTH, n), BF16)
    args = (x2, norm_g.reshape(1, d), w_a, w_t, cos_a, sin_a, cos_i, sin_i, ln_g.reshape(1, -1),
            ln_b.reshape(1, -1), w_spatial, bias, conv_w)
    in_specs = [row(d), full(args[1]), full(w_a), full(w_t), row(LANES), row(LANES), row(LANES),
                row(LANES), full(args[8]), full(args[9]), full(w_spatial), full(bias), full(conv_w)]
    out_shape = [bf(256), bf(256), bft, bf(128), bf(128), jax.ShapeDtypeStruct((SUBLANES, n), F32),
                 bf(256), bf(256), bf(256), bft, bf(256)]
    out_specs = [row(256), row(256), col(BRANCH_WIDTH), row(128), row(128), col(SUBLANES),
                 row(256), row(256), row(256), col(BRANCH_WIDTH), row(256)]
    return pl.pallas_call(
        functools.partial(_proj_kernel, steps_per_seq=seq // rows),
        grid=(n // rows,),
        in_specs=in_specs,
        out_specs=out_specs,
        out_shape=out_shape,
        scratch_shapes=[pltpu.VMEM((CONV_HALO + rows, BRANCH_WIDTH), F32)],
        compiler_params=_params("arbitrary"),
        name="project",
    )(*args)


def _head_lane_mask(width, per_head, h, rows=1):
    lane = lax.broadcasted_iota(jnp.int32, (rows, width), 1)
    return (lane >= h * per_head) & (lane < (h + 1) * per_head)


def _masked_heads(x, per_head):
    return [jnp.where(_head_lane_mask(x.shape[1], per_head, h), x, jnp.zeros_like(x))
            for h in range(N_HEADS)]


def _float_to_key(x):
    bits = lax.bitcast_convert_type(x, jnp.int32)
    sign = bits >> 31
    return (bits ^ (sign & jnp.int32(0x7FFFFFFF))) - sign


def _key_to_float(key):
    bits = jnp.where(key < 0, (-key) | jnp.int32(INT_MIN), key)
    return lax.bitcast_convert_type(bits, F32)


def _neg_abs(x):
    bits = lax.bitcast_convert_type(x, jnp.int32) | jnp.int32(INT_MIN)
    return lax.bitcast_convert_type(bits, F32)


def _col_reduce(x, op, final):
    while x.shape[0] > SUBLANES:
        half = x.shape[0] // 2
        x = op(x[:half], x[half:])
    return final(x, axis=0, keepdims=True)


def _col_sum(x):
    return _col_reduce(x, jnp.add, jnp.sum)


def _col_max(x):
    return _col_reduce(x, jnp.maximum, jnp.max)


def _step_starts(step):
    return [pl.multiple_of((step * TILES_PER_STEP + g) * K_TILE, K_TILE) for g in range(TILES_PER_STEP)]


def _dsa_kernel(iq_ref, iwt_ref, q_ref, ik_ref, k_ref, vt_ref, o_ref, sc_ref, hi_ref, lo_ref, kmax_ref, *,
                topk):
    qb = q_ref.shape[0]
    i = pl.program_id(1)
    n_tiles = i + 1
    n_steps = (n_tiles + TILES_PER_STEP - 1) // TILES_PER_STEP

    @pl.when(i == 0)
    def _():
        sc_ref[...] = jnp.full(sc_ref.shape, -jnp.inf, F32)
        hi_ref[...] = jnp.full(hi_ref.shape, -HALF_RANGE, jnp.int16)
        lo_ref[...] = jnp.full(lo_ref.shape, -HALF_RANGE, jnp.int16)

    iq_heads = _masked_heads(iq_ref[...], IDX_DIM)
    iwt = iwt_ref[...]
    iw_heads = [iwt[h:h + 1, :] for h in range(N_HEADS)]

    def put_scores(start, sc):
        sc_ref[pl.ds(start, K_TILE), :] = sc
        key = _float_to_key(sc)
        hi_ref[pl.ds(start, K_TILE), :] = (key >> 16).astype(jnp.int16)
        lo_ref[pl.ds(start, K_TILE), :] = ((key & 0xFFFF) - HALF_RANGE).astype(jnp.int16)

    key_off = lax.broadcasted_iota(jnp.int32, (K_TILE, qb), 0)
    qry_pos = i * qb + lax.broadcasted_iota(jnp.int32, (K_TILE, qb), 1)

    def score_step(jj, masked):
        starts = _step_starts(jj)
        dots = [[lax.dot_general(ik_ref[pl.ds(st, K_TILE), :], iq_heads[h], _CONTRACT_LAST,
                                 preferred_element_type=F32) for h in range(N_HEADS)] for st in starts]
        for g in range(TILES_PER_STEP):
            sc = None
            for h in range(N_HEADS):
                t = iw_heads[h] * jnp.maximum(dots[g][h], 0.0)
                sc = t if sc is None else sc + t
            if masked:
                sc = jnp.where(starts[g] + key_off <= qry_pos, sc, -jnp.inf)
            put_scores(starts[g], sc)

    def full_step(jj, carry):
        score_step(jj, False)
        return carry

    lax.fori_loop(0, i // TILES_PER_STEP, full_step, 0)
    score_step(i // TILES_PER_STEP, True)

    n_chunks = (n_tiles * K_TILE + COUNT_CHUNK - 1) // COUNT_CHUNK
    kf = jnp.float32(topk)

    def chunk_rows(c):
        base = pl.multiple_of(c * COUNT_CHUNK, COUNT_CHUNK)
        return [pl.ds(base + r * COUNT_ROWS, COUNT_ROWS) for r in range(COUNT_CHUNK // COUNT_ROWS)]

    def count(ref, pred, dtype):
        one, zero = jnp.ones((), dtype), jnp.zeros((), dtype)

        def body(c, acc):
            for rows in chunk_rows(c):
                acc = acc + jnp.where(pred(ref[rows, :]), one, zero)
            return acc
        acc = lax.fori_loop(0, n_chunks, body, jnp.zeros((COUNT_ROWS, qb), dtype))
        return jnp.sum(acc.astype(F32), axis=0, keepdims=True)

    def bisect16(ref, target):
        def bit_step(b, carry):
            best, cnt_best, cnt_above = carry
            cand = best + lax.shift_left(jnp.int32(1), 15 - b)
            cand16 = cand.astype(jnp.int16)
            cnt = count(ref, lambda blk: blk >= cand16, jnp.int16)
            ok = cnt >= target
            return jnp.where(ok, cand, best), jnp.where(ok, cnt, cnt_best), jnp.where(ok, cnt_above, cnt)
        init = (jnp.full((1, qb), -HALF_RANGE, jnp.int32), jnp.zeros((1, qb), F32), jnp.zeros((1, qb), F32))
        return lax.fori_loop(0, 16, bit_step, init)

    hi_t, cnt_hi, above = bisect16(hi_ref, kf)
    hi_t16 = hi_t.astype(jnp.int16)

    def mask_low(c, carry):
        for rows in chunk_rows(c):
            lo_ref[rows, :] = jnp.where(hi_ref[rows, :] == hi_t16, lo_ref[rows, :], jnp.int16(-HALF_RANGE))
        return carry

    lax.fori_loop(0, n_chunks, mask_low, 0)
    lo_t, cnt_lo, above_lo = bisect16(lo_ref, kf - above)
    cnt_key = above + jnp.where(lo_t == -HALF_RANGE, cnt_hi - above, cnt_lo)
    key = hi_t * (2 * HALF_RANGE) + (lo_t + HALF_RANGE)
    few = i * qb + lax.broadcasted_iota(jnp.int32, (1, qb), 1) + 1 < topk
    thr = jnp.where(few, jnp.finfo(F32).min, _key_to_float(key))

    @pl.when(jnp.max(jnp.where(few, kf, cnt_key)) > kf)
    def _():
        need = kf - (above + above_lo)
        before = (lax.broadcasted_iota(jnp.int32, (K_TILE, K_TILE), 1)
                  < lax.broadcasted_iota(jnp.int32, (K_TILE, K_TILE), 0))
        before = jnp.where(before, 1.0, 0.0).astype(BF16)

        def tie_body(jj, seen):
            starts = _step_starts(jj)
            blks = [sc_ref[pl.ds(st, K_TILE), :] for st in starts]
            eqs = [blk == thr for blk in blks]
            eq_fs = [jnp.where(eq, 1.0, 0.0) for eq in eqs]
            ranks = [jnp.dot(before, eq_f.astype(BF16), preferred_element_type=F32) for eq_f in eq_fs]
            for g in range(TILES_PER_STEP):
                drop = eqs[g] & (seen + ranks[g] >= need)
                sc_ref[pl.ds(starts[g], K_TILE), :] = jnp.where(drop, -jnp.inf, blks[g])
                seen = seen + _col_sum(eq_fs[g])
            return seen

        lax.fori_loop(0, n_steps, tie_body, jnp.zeros((1, qb), F32))

    q = q_ref[...]
    q_heads = _masked_heads(q, HEAD_DIM)
    head_rows = jnp.concatenate(
        [jnp.where(_head_lane_mask(BRANCH_WIDTH, HEAD_DIM, h), 1.0, 0.0) for h in range(N_HEADS)]
        + [jnp.zeros((ONES_ROWS - N_HEADS, BRANCH_WIDTH), F32)], axis=0).astype(BF16)
    qf = q.astype(F32)
    q_sq = lax.dot_general(head_rows, (qf * qf).astype(BF16), _CONTRACT_LAST,
                           preferred_element_type=F32)
    kf32 = k_ref[pl.ds(pl.multiple_of(i * K_TILE, K_TILE), K_TILE), :].astype(F32)
    k_sq = lax.dot_general(head_rows, (kf32 * kf32).astype(BF16), _CONTRACT_LAST,
                           preferred_element_type=F32)
    k_sq_max = jnp.max(k_sq, axis=1, keepdims=True)

    @pl.when(i == 0)
    def _():
        kmax_ref[...] = jnp.zeros(kmax_ref.shape, F32)

    kmax_ref[...] = jnp.maximum(kmax_ref[...], jnp.broadcast_to(k_sq_max, kmax_ref.shape))
    bound = jnp.sqrt(q_sq * kmax_ref[:, 0:1]) * BOUND_MARGIN
    bounds = [bound[h:h + 1, :] for h in range(N_HEADS)]
    ones_rows = jnp.ones((ONES_ROWS, K_TILE), BF16)

    def step_logits(jj):
        starts = _step_starts(jj)
        sels = [sc_ref[pl.ds(st, K_TILE), :] >= thr for st in starts]
        logits = [[lax.dot_general(k_ref[pl.ds(st, K_TILE), :], q_heads[h], _CONTRACT_LAST,
                                   preferred_element_type=F32) for h in range(N_HEADS)] for st in starts]
        masked = [[jnp.where(sels[g], logits[g][h], NEG_BIG) for g in range(TILES_PER_STEP)]
                  for h in range(N_HEADS)]
        return starts, masked

    def value_products(starts, probs, h):
        pv = None
        for g, st in enumerate(starts):
            vt = vt_ref[h * HEAD_DIM:(h + 1) * HEAD_DIM, pl.ds(st, K_TILE)]
            t = jnp.dot(vt, probs[g], preferred_element_type=F32)
            pv = t if pv is None else pv + t
        return pv

    def bounded_step(jj, carry):
        l, acc = carry
        starts, masked = step_logits(jj)
        probs = [[jnp.exp(s - bounds[h]).astype(BF16) for s in masked[h]] for h in range(N_HEADS)]
        l_new, acc_new = [], []
        for h in range(N_HEADS):
            lh = l[h]
            for pb in probs[h]:
                lh = lh + jnp.dot(ones_rows, pb, preferred_element_type=F32)[0:1]
            l_new.append(lh)
            acc_new.append(acc[h] + value_products(starts, probs[h], h))
        return tuple(l_new), tuple(acc_new)

    def online_step(jj, carry):
        m, l, acc = carry
        starts, masked = step_logits(jj)
        m_new, l_new, acc_new = [], [], []
        for h in range(N_HEADS):
            mh = m[h]
            for s in masked[h]:
                mh = jnp.maximum(mh, _col_max(s))
            ps = [jnp.exp(s - mh) for s in masked[h]]
            alpha = jnp.exp(m[h] - mh)
            lh = alpha * l[h]
            for p in ps:
                lh = lh + _col_sum(p)
            m_new.append(mh)
            l_new.append(lh)
            acc_new.append(alpha * acc[h] + value_products(starts, [p.astype(BF16) for p in ps], h))
        return tuple(m_new), tuple(l_new), tuple(acc_new)

    def finish(l, acc):
        out_t = jnp.concatenate([acc[h] / l[h] for h in range(N_HEADS)], axis=0)
        o_ref[...] = out_t.T.astype(o_ref.dtype)

    zeros_l = tuple(jnp.zeros((1, qb), F32) for _ in range(N_HEADS))
    zeros_acc = tuple(jnp.zeros((HEAD_DIM, qb), F32) for _ in range(N_HEADS))
    small = jnp.max(bound[0:N_HEADS, :]) <= BOUND_LIMIT

    @pl.when(small)
    def _():
        finish(*lax.fori_loop(0, n_steps, bounded_step, (zeros_l, zeros_acc)))

    @pl.when(jnp.logical_not(small))
    def _():
        init = (tuple(jnp.full((1, qb), NEG_BIG, F32) for _ in range(N_HEADS)), zeros_l, zeros_acc)
        _, l, acc = lax.fori_loop(0, n_steps, online_step, init)
        finish(l, acc)


def _dsa(iq, iwt, qa, ik, ka, vat, batch, seq):
    n = qa.shape[0]
    nq = seq // Q_BLOCK
    topk = min(TOPK_MAX, seq // 4)
    qrow = lambda w: pl.BlockSpec((Q_BLOCK, w), lambda b, i: (b * nq + i, 0))
    whole = lambda w: pl.BlockSpec((seq, w), lambda b, i: (b, 0))
    return pl.pallas_call(
        functools.partial(_dsa_kernel, topk=topk),
        grid=(batch, nq),
        in_specs=[qrow(LANES), pl.BlockSpec((SUBLANES, Q_BLOCK), lambda b, i: (0, b * nq + i)),
                  qrow(BRANCH_WIDTH), whole(LANES), whole(BRANCH_WIDTH),
                  pl.BlockSpec((BRANCH_WIDTH, seq), lambda b, i: (0, b))],
        out_specs=qrow(BRANCH_WIDTH),
        out_shape=jax.ShapeDtypeStruct((n, BRANCH_WIDTH), BF16),
        scratch_shapes=[pltpu.VMEM((seq, Q_BLOCK), F32), pltpu.VMEM((seq, Q_BLOCK), jnp.int16),
                        pltpu.VMEM((seq, Q_BLOCK), jnp.int16), pltpu.VMEM((ONES_ROWS, LANES), F32)],
        compiler_params=_params("arbitrary", "arbitrary"),
        name="dsa",
    )(iq, iwt, qa, ik, ka, vat)


def _stick_kernel(q_ref, k_ref, vt_ref, o_ref):
    qb = q_ref.shape[0]
    i = pl.program_id(1)
    q_heads = _masked_heads(q_ref[...], HEAD_DIM)
    after = (lax.broadcasted_iota(jnp.int32, (K_TILE, K_TILE), 1)
             > lax.broadcasted_iota(jnp.int32, (K_TILE, K_TILE), 0))
    after = jnp.where(after, 1.0, 0.0).astype(BF16)
    after = jnp.concatenate([after, jnp.ones((ONES_ROWS, K_TILE), BF16)], axis=0)
    key_off = lax.broadcasted_iota(jnp.int32, (K_TILE, qb), 0)
    qry_pos = i * qb + lax.broadcasted_iota(jnp.int32, (K_TILE, qb), 1)

    def step(u, carry, masked):
        tails, acc = carry
        first = i - 2 * u
        tiles = [first, jnp.maximum(first - 1, 0)]
        bias = [None, jnp.where(first >= 1, 0.0, PHANTOM_LOGIT)]
        starts = [pl.multiple_of(t * K_TILE, K_TILE) for t in tiles]
        stricts = [starts[g] + key_off < qry_pos if masked else None for g in range(2)]
        zs = [[lax.dot_general(k_ref[pl.ds(st, K_TILE), :], q_heads[h], _CONTRACT_LAST,
                               preferred_element_type=F32) for h in range(N_HEADS)] for st in starts]
        log_betas, splits = [], []
        for g in range(2):
            lbs, sps = [], []
            for h in range(N_HEADS):
                z = zs[g][h] if bias[g] is None else zs[g][h] + bias[g]
                log_beta = jnp.minimum(z, 0.0) - jnp.log(1.0 + jnp.exp(_neg_abs(z)))
                log_keep = log_beta - z
                if masked:
                    log_keep = jnp.where(stricts[g], log_keep, 0.0)
                lbs.append(log_beta)
                sps.append(log_keep.astype(BF16))
            log_betas.append(lbs)
            splits.append(sps)
        sums = [[jnp.dot(after, splits[g][h], preferred_element_type=F32) for h in range(N_HEADS)]
                for g in range(2)]
        tails = list(tails)
        new_acc = []
        for h in range(N_HEADS):
            total = acc[h]
            for g in range(2):
                a = jnp.exp(log_betas[g][h] + sums[g][h][:K_TILE] + tails[h])
                if masked:
                    a = jnp.where(stricts[g], a, 0.0)
                tails[h] = tails[h] + sums[g][h][K_TILE:K_TILE + 1]
                vt = vt_ref[h * HEAD_DIM:(h + 1) * HEAD_DIM, pl.ds(starts[g], K_TILE)]
                total = total + jnp.dot(vt, a.astype(BF16), preferred_element_type=F32)
            new_acc.append(total)
        return tuple(tails), tuple(new_acc)

    init = (tuple(jnp.zeros((1, qb), F32) for _ in range(N_HEADS)),
            tuple(jnp.zeros((HEAD_DIM, qb), F32) for _ in range(N_HEADS)))
    carry = step(0, init, True)

    def more(state):
        u, (tails, _) = state
        live = tails[0]
        for h in range(1, N_HEADS):
            live = jnp.maximum(live, tails[h])
        return (u < (i + 2) // 2) & (jnp.max(live) > EXP_ZERO_BELOW)

    _, (_, acc) = lax.while_loop(more, lambda st: (st[0] + 1, step(st[0], st[1], False)), (jnp.int32(1), carry))
    o_ref[...] = jnp.concatenate(acc, axis=0).T.astype(o_ref.dtype)


def _stick_breaking(qc, kc, vct, batch, seq):
    n = qc.shape[0]
    nq = seq // Q_BLOCK
    qrow = pl.BlockSpec((Q_BLOCK, BRANCH_WIDTH), lambda b, i: (b * nq + i, 0))
    return pl.pallas_call(
        _stick_kernel,
        grid=(batch, nq),
        in_specs=[qrow, pl.BlockSpec((seq, BRANCH_WIDTH), lambda b, i: (b, 0)),
                  pl.BlockSpec((BRANCH_WIDTH, seq), lambda b, i: (0, b))],
        out_specs=qrow,
        out_shape=jax.ShapeDtypeStruct((n, BRANCH_WIDTH), BF16),
        compiler_params=_params("arbitrary", "arbitrary"),
        name="stick_breaking",
    )(qc, kc, vct)


def _merge_kernel(x_ref, g1_ref, ya_ref, yb_ref, yc_ref, yd_ref, wg_ref, wb_ref, wo_ref,
                  g2_ref, wr_ref, br_ref, x1_o, comb_o):
    x = x_ref[...]
    d = x.shape[1]
    h = _rms_bf16(x, g1_ref[...])
    ys = (ya_ref[...], yb_ref[...], yc_ref[...], yd_ref[...])
    col = 256
    pieces = []
    for c in range(d // col):
        m = None
        for n in range(N_BRANCH):
            logit = jnp.dot(h, wg_ref[:, n * d + c * col:n * d + (c + 1) * col],
                            preferred_element_type=F32)
            br = jnp.dot(ys[n], wb_ref[n, :, c * col:(c + 1) * col], preferred_element_type=F32)
            t = jax.nn.sigmoid(logit) * br
            m = t if m is None else m + t
        pieces.append(m.astype(BF16))
    merged = jnp.concatenate(pieces, axis=1)
    x1 = x + jnp.dot(merged, wo_ref[...], preferred_element_type=F32)
    x1_o[...] = x1

    ms = jnp.mean(x1 * x1, axis=-1, keepdims=True)
    h2 = x1 * lax.rsqrt(ms + EPS) * g2_ref[...]
    h_hi = h2.astype(BF16)
    h_lo = (h2 - h_hi.astype(F32)).astype(BF16)
    both = jnp.dot(h_hi, wr_ref[...], preferred_element_type=F32)
    cross = jnp.dot(h_lo, wr_ref[:, 0:LANES], preferred_element_type=F32)
    logits = both[:, 0:LANES] + both[:, LANES:2 * LANES] + cross + br_ref[...]
    rows = logits.shape[0]
    lane = lax.broadcasted_iota(jnp.int32, (rows, LANES), 1)
    neg = -jnp.inf
    is_grp = (lane >= N_EXPERTS) & (lane < N_EXPERTS + N_EXPERT_GROUPS)
    gl = jnp.where(is_grp, logits, neg)
    gmax = jnp.max(gl, axis=1, keepdims=True)
    gidx = jnp.min(jnp.where(gl == gmax, lane, LANES), axis=1, keepdims=True) - N_EXPERTS
    grp_p = 1.0 / jnp.sum(jnp.where(is_grp, jnp.exp(gl - gmax), 0.0), axis=1, keepdims=True)
    in_grp = (lane >= gidx * EXPERTS_PER_GROUP) & (lane < (gidx + 1) * EXPERTS_PER_GROUP)
    el = jnp.where(in_grp, logits, neg)
    l1 = jnp.max(el, axis=1, keepdims=True)
    j1 = jnp.min(jnp.where(el == l1, lane, LANES), axis=1, keepdims=True)
    el2 = jnp.where(lane == j1, neg, el)
    l2 = jnp.max(el2, axis=1, keepdims=True)
    j2 = jnp.min(jnp.where(el2 == l2, lane, LANES), axis=1, keepdims=True)
    e2 = jnp.exp(l2 - l1)
    w1 = grp_p / (1.0 + e2)
    w2 = grp_p * e2 / (1.0 + e2)
    comb_o[...] = jnp.where(lane == j1, w1, 0.0) + jnp.where(lane == j2, w2, 0.0)


def _merge(x2, norm1_g, ys, w_gate, w_branch, w_out, norm2_g, w_router, b_router):
    n, d = x2.shape
    rows = MERGE_ROWS
    row = lambda w: pl.BlockSpec((rows, w), lambda i: (i, 0))
    full = lambda a: pl.BlockSpec(a.shape, lambda i: (0,) * a.ndim)
    args = (x2, norm1_g.reshape(1, d), *ys, w_gate, w_branch, w_out, norm2_g.reshape(1, d),
            w_router, b_router)
    in_specs = [row(d), full(args[1])] + [row(BRANCH_WIDTH)] * 4 + [full(a) for a in args[6:]]
    return pl.pallas_call(
        _merge_kernel,
        grid=(n // rows,),
        in_specs=in_specs,
        out_specs=[row(d), row(LANES)],
        out_shape=[jax.ShapeDtypeStruct((n, d), F32), jax.ShapeDtypeStruct((n, LANES), F32)],
        compiler_params=_params("arbitrary"),
        name="merge_router",
    )(*args)


def _moe_kernel(x_ref, g2_ref, comb_ref, wgu_ref, wd_ref, gf_ref, o_ref,
                h_scr, acc_scr, memb_scr, rank_scr, mcol_scr, rcol_scr, hg_scr, cw_scr, y_scr, flag_scr,
                *, final_norm):
    step = pl.program_id(1)
    rows = x_ref.shape[0]
    steps_per_group = EXPERTS_PER_GROUP // EXPERTS_PER_STEP
    group = step // steps_per_group
    first_of_group = step % steps_per_group == 0
    last_of_group = step % steps_per_group == steps_per_group - 1
    lane = lax.broadcasted_iota(jnp.int32, (rows, LANES), 1)

    @pl.when(step == 0)
    def _():
        h_scr[...] = _rms_bf16(x_ref[...], g2_ref[...])
        acc_scr[...] = jnp.zeros(acc_scr.shape, F32)
        comb = comb_ref[...]
        member_cols = [jnp.sum(jnp.where((lane >= g * EXPERTS_PER_GROUP) & (lane < (g + 1) * EXPERTS_PER_GROUP),
                                         comb, 0.0), axis=1, keepdims=True) > 0.0 for g in range(N_EXPERT_GROUPS)]
        mcol = jnp.zeros((rows, LANES), F32)
        for g in range(N_EXPERT_GROUPS):
            mcol = jnp.where((lane == g) & member_cols[g], 1.0, mcol)
        mcol_scr[...] = mcol
        t_row = lax.broadcasted_iota(jnp.int32, (rows, rows), 0)
        t_col = lax.broadcasted_iota(jnp.int32, (rows, rows), 1)
        earlier = jnp.where(t_col < t_row, 1.0, 0.0).astype(BF16)
        rcol_scr[...] = jnp.dot(earlier, mcol.astype(BF16), preferred_element_type=F32)
        memb = mcol.T[0:ONES_ROWS, :]
        memb_scr[...] = memb
        later = jnp.where(t_row < t_col, 1.0, 0.0).astype(BF16)
        rank_scr[...] = jnp.dot(memb.astype(BF16), later, preferred_element_type=F32)
        counts = jnp.sum(memb, axis=1, keepdims=True)
        for g in range(N_EXPERT_GROUPS):
            flag_scr[g] = (jnp.max(counts[g:g + 1, :]) > GROUP_SLOTS).astype(jnp.int32)

    crowded = flag_scr[group] == 1
    comb = comb_ref[...]

    def expert_rows(hb, cw, j):
        e = step * EXPERTS_PER_STEP + j
        lane_c = lax.broadcasted_iota(jnp.int32, cw.shape, 1)
        w = jnp.sum(jnp.where(lane_c == e, cw, 0.0), axis=1, keepdims=True)
        gu = jnp.dot(hb, wgu_ref[j], preferred_element_type=F32)
        hh = (jax.nn.silu(gu[:, :D_EXPERT]) * gu[:, D_EXPERT:] * w).astype(BF16)
        return jnp.dot(hh, wd_ref[j], preferred_element_type=F32)

    @pl.when(crowded)
    def _():
        total = None
        for j in range(EXPERTS_PER_STEP):
            t = expert_rows(h_scr[...], comb, j)
            total = t if total is None else total + t
        acc_scr[...] += total

    @pl.when(jnp.logical_not(crowded))
    def _():
        slot_rows = lax.broadcasted_iota(jnp.int32, (GROUP_SLOTS, rows), 0).astype(F32)

        @pl.when(first_of_group)
        def _():
            in_group = memb_scr[pl.ds(group, 1), :] > 0.0
            rank = rank_scr[pl.ds(group, 1), :]
            pick = jnp.where(in_group & (slot_rows == rank), 1.0, 0.0).astype(BF16)
            hg_scr[...] = jnp.dot(pick, h_scr[...], preferred_element_type=F32).astype(BF16)
            comb_hi = comb.astype(BF16)
            comb_lo = (comb - comb_hi.astype(F32)).astype(BF16)
            cw_scr[...] = (jnp.dot(pick, comb_hi, preferred_element_type=F32)
                           + jnp.dot(pick, comb_lo, preferred_element_type=F32))
            y_scr[...] = jnp.zeros(y_scr.shape, F32)

        total = None
        for j in range(EXPERTS_PER_STEP):
            t = expert_rows(hg_scr[...], cw_scr[...], j)
            total = t if total is None else total + t
        y_scr[...] += total

        @pl.when(last_of_group)
        def _():
            in_group = jnp.sum(jnp.where(lane == group, mcol_scr[...], 0.0), axis=1, keepdims=True) > 0.0
            rank = jnp.sum(jnp.where(lane == group, rcol_scr[...], 0.0), axis=1, keepdims=True)
            slot_cols = lax.broadcasted_iota(jnp.int32, (rows, GROUP_SLOTS), 1).astype(F32)
            place = jnp.where(in_group & (slot_cols == rank), 1.0, 0.0).astype(BF16)
            acc_scr[...] += jnp.dot(place, y_scr[...].astype(BF16), preferred_element_type=F32)

    @pl.when(step == pl.num_programs(1) - 1)
    def _():
        y = x_ref[...] + acc_scr[...]
        if final_norm:
            ms = jnp.mean(y * y, axis=-1, keepdims=True)
            y = y * lax.rsqrt(ms + EPS) * gf_ref[...]
        o_ref[...] = y


def _moe(x1, norm2_g, comb, w_gu, w_d, norm_f_g, final_norm):
    n, d = x1.shape
    rows = min(MOE_ROWS, n)
    n_exp = w_gu.shape[0]
    row = lambda w: pl.BlockSpec((rows, w), lambda i, e: (i, 0))
    vec = pl.BlockSpec((1, d), lambda i, e: (0, 0))
    return pl.pallas_call(
        functools.partial(_moe_kernel, final_norm=final_norm),
        grid=(n // rows, n_exp // EXPERTS_PER_STEP),
        in_specs=[row(d), vec, row(LANES),
                  pl.BlockSpec((EXPERTS_PER_STEP, d, 2 * D_EXPERT), lambda i, e: (e, 0, 0)),
                  pl.BlockSpec((EXPERTS_PER_STEP, D_EXPERT, d), lambda i, e: (e, 0, 0)),
                  vec],
        out_specs=row(d),
        out_shape=jax.ShapeDtypeStruct((n, d), F32),
        scratch_shapes=[pltpu.VMEM((rows, d), BF16), pltpu.VMEM((rows, d), F32),
                        pltpu.VMEM((ONES_ROWS, rows), F32), pltpu.VMEM((ONES_ROWS, rows), F32),
                        pltpu.VMEM((rows, LANES), F32), pltpu.VMEM((rows, LANES), F32),
                        pltpu.VMEM((GROUP_SLOTS, d), BF16), pltpu.VMEM((GROUP_SLOTS, LANES), F32),
                        pltpu.VMEM((GROUP_SLOTS, d), F32), pltpu.SMEM((N_EXPERT_GROUPS,), jnp.int32)],
        compiler_params=_params("arbitrary", "arbitrary"),
        name="experts",
    )(x1, norm2_g.reshape(1, d), comb, w_gu, w_d, norm_f_g.reshape(1, d))


def kernel(x, positions, norm1_g, w_in, sgu_ln_g, sgu_ln_b, w_spatial, b_spatial, conv_w, w_branch,
           w_out, norm2_g, w_router_group, b_router_group, w_router_expert, b_router_expert,
           w_exp_gate, w_exp_up, w_exp_down, norm_f_g):
    batch, seq, d = x.shape
    depth = w_in.shape[0]
    n = batch * seq
    assert TILES_PER_STEP == 2
    assert seq % COUNT_CHUNK == 0 and seq % (TILES_PER_STEP * K_TILE) == 0
    assert n % PROJ_ROWS == 0 and n % MERGE_ROWS == 0
    x2 = x.reshape(n, d)
    tables = _rope_tables(positions)
    for l in range(depth):
        w_a, w_t, gate_off = _proj_weight(w_in[l])
        w_gate = w_in[l][:, gate_off:].astype(BF16)
        pad = jnp.zeros((d, LANES - N_EXPERTS - N_EXPERT_GROUPS), F32)
        w_router = jnp.concatenate([w_router_expert[l], w_router_group[l], pad], axis=1)
        w_router_hi = w_router.astype(BF16)
        w_router_lo = (w_router - w_router_hi.astype(F32)).astype(BF16)
        w_router = jnp.concatenate([w_router_hi, w_router_lo], axis=1)
        b_router = jnp.concatenate([b_router_expert[l], b_router_group[l], pad[0]]).reshape(1, LANES)
        qa, ka, vat, iq, ik, iwt, y_b, qc, kc, vct, y_d = _project(
            x2, norm1_g[l], w_a, w_t, tables, sgu_ln_g[l], sgu_ln_b[l], w_spatial[l], b_spatial[l],
            conv_w[l], seq)
        y_a = _dsa(iq, iwt, qa, ik, ka, vat, batch, seq)
        y_c = _stick_breaking(qc, kc, vct, batch, seq)
        x1, comb = _merge(x2, norm1_g[l], (y_a, y_b, y_c, y_d), w_gate, w_branch[l].astype(BF16),
                          w_out[l].astype(BF16), norm2_g[l], w_router, b_router)
        w_gu = jnp.concatenate([w_exp_gate[l], w_exp_up[l]], axis=2).astype(BF16)
        x2 = _moe(x1, norm2_g[l], comb, w_gu, w_exp_down[l].astype(BF16), norm_f_g,
                  final_norm=(l == depth - 1))
    return x2.reshape(batch, seq, d)
```
